```python
import math
import jax, jax.numpy as jnp
from jax import lax
import numpy as np


D_MODEL = 1024
BATCH = 16
SEQ = 4096
DEPTH = 2
DEC_BATCH = 16
DEC_SEQ = 16
PAST_LEN = 1024

CHUNK = 64
EPS = 1e-6
D_FF = 2816
GLA_HEADS = 4
GLA_DK = 64
GLA_DV = 128
GLA_RANK = 16
GLA_TAU = 16.0
SSD_HEADS = 8
SSD_HEAD_DIM = 64
SSD_GROUPS = 2
SSD_STATE = 64
SSD_CONV = 4
SSD_INNER = SSD_HEADS * SSD_HEAD_DIM
SSD_CONV_DIM = SSD_INNER + 2 * SSD_GROUPS * SSD_STATE
RET_HEADS = 4
RET_DK = 64
RET_DV = 128
ROPE_BASE = 10000.0
SPLIT_SIZES = (GLA_HEADS * GLA_DK, GLA_HEADS * GLA_DK, GLA_HEADS * GLA_DV, GLA_HEADS * GLA_DV, GLA_RANK,
               SSD_INNER, SSD_CONV_DIM, SSD_HEADS,
               RET_HEADS * RET_DK, RET_HEADS * RET_DK, RET_HEADS * RET_DV, RET_HEADS * RET_DV,
               D_MODEL, D_MODEL, D_MODEL)
IN_COLS = sum(SPLIT_SIZES)

kernel_name = 'hybrid_gla_ssd_retnet_stream_step'


def split_points():
    pts, acc = [], 0
    for s in SPLIT_SIZES[:-1]:
        acc += s
        pts.append(acc)
    return pts


def rmsnorm(x, g):
    xf = x.astype(jnp.float32)
    y = xf * lax.rsqrt(jnp.mean(xf * xf, axis=-1, keepdims=True) + EPS)
    return (y * g.astype(jnp.float32)).astype(x.dtype)


def group_norm(o, g):
    mu = jnp.mean(o, axis=-1, keepdims=True)
    var = jnp.mean(jnp.square(o - mu), axis=-1, keepdims=True)
    y = (o - mu) * lax.rsqrt(var + EPS)
    B, T = o.shape[:2]
    return y.reshape(B, T, -1) * g.astype(jnp.float32)


def swiglu(x, w_in, w_out):
    gate, up = jnp.split(x @ w_in, 2, axis=-1)
    return (jax.nn.silu(gate) * up) @ w_out


def rope(x, pos):
    half = x.shape[-1] // 2
    freqs = ROPE_BASE ** (-jnp.arange(half, dtype=jnp.float32) / half)
    ang = pos.astype(jnp.float32)[:, None] * freqs[None, :]
    cos = jnp.cos(ang)[None, :, None, :]
    sin = jnp.sin(ang)[None, :, None, :]
    xf = x.astype(jnp.float32)
    x1, x2 = xf[..., :half], xf[..., half:]
    return jnp.concatenate([x1 * cos - x2 * sin, x1 * sin + x2 * cos], axis=-1)


def chunk_len(T):
    return CHUNK if T % CHUNK == 0 else T


def to_chunks(a, L):
    B, T = a.shape[:2]
    return jnp.moveaxis(a.reshape((B, T // L, L) + a.shape[2:]), 1, 0)


def from_chunks(a):
    a = jnp.moveaxis(a, 0, 1)
    return a.reshape((a.shape[0], -1) + a.shape[3:])


def scan_vector_decay(q, k, v, log_a, s0):
    L = chunk_len(q.shape[1])
    mask = jnp.tril(jnp.ones((L, L), dtype=bool))

    def step(S, inp):
        qc, kc, vc, ac = inp
        b = jnp.cumsum(ac, axis=1)
        bL = b[:, -1]
        qd = qc * jnp.exp(b)
        kd = kc * jnp.exp(-b)
        scores = jnp.where(mask, jnp.einsum('blhn,bshn->bhls', qd, kd), 0.0)
        o = jnp.einsum('bhls,bshp->blhp', scores, vc) + jnp.einsum('blhn,bhnp->blhp', qd, S)
        ks = kc * jnp.exp(bL[:, None] - b)
        S = jnp.exp(bL)[..., None] * S + jnp.einsum('bshn,bshp->bhnp', ks, vc)
        return S, o

    xs = tuple(to_chunks(t.astype(jnp.float32), L) for t in (q, k, v, log_a))
    S, o = lax.scan(step, s0.astype(jnp.float32), xs)
    return from_chunks(o), S


def scan_scalar_decay(q, k, v, log_a, s0):
    L = chunk_len(q.shape[1])
    mask = jnp.tril(jnp.ones((L, L), dtype=bool))[None, :, :, None]

    def step(S, inp):
        qc, kc, vc, ac = inp
        b = jnp.cumsum(ac, axis=1)
        bL = b[:, -1]
        seg = b[:, :, None, :] - b[:, None, :, :]
        decay = jnp.exp(jnp.where(mask, seg, -jnp.inf))
        scores = jnp.einsum('blhn,bshn->blsh', qc, kc) * decay
        o = (jnp.einsum('blsh,bshp->blhp', scores, vc)
             + jnp.einsum('blhn,bhnp->blhp', qc * jnp.exp(b)[..., None], S))
        ks = kc * jnp.exp(bL[:, None] - b)[..., None]
        S = jnp.exp(bL)[:, :, None, None] * S + jnp.einsum('bshn,bshp->bhnp', ks, vc)
        return S, o

    xs = tuple(to_chunks(t.astype(jnp.float32), L) for t in (q, k, v, log_a))
    S, o = lax.scan(step, s0.astype(jnp.float32), xs)
    return from_chunks(o), S


def token_mixers(h, pos, s_gla, s_ssd, conv_buf, s_ret, w_in, gla_w_gate2, gla_b_gate, gla_norm,
                 ssd_conv_w, ssd_conv_b, ssd_dt_bias, ssd_a_log, ssd_d, ssd_norm, ret_norm,
                 w_branch_gla, w_branch_ssd, w_branch_ret, w_out):
    B, T, _ = h.shape
    dt_ = h.dtype
    proj = h @ w_in
    (g_q, g_k, g_v, g_r, g_lr, m_z, m_xbc, m_dt,
     r_q, r_k, r_v, r_g, gate_a, gate_b, gate_c) = jnp.split(proj, split_points(), axis=-1)

    q = g_q.reshape(B, T, GLA_HEADS, GLA_DK) * (GLA_DK ** -0.5)
    k = g_k.reshape(B, T, GLA_HEADS, GLA_DK)
    v = g_v.reshape(B, T, GLA_HEADS, GLA_DV)
    log_a = jax.nn.log_sigmoid((g_lr @ gla_w_gate2 + gla_b_gate).astype(jnp.float32)) / GLA_TAU
    log_a = log_a.reshape(B, T, GLA_HEADS, GLA_DK)
    o_gla, s_gla_new = scan_vector_decay(q, k, v, log_a, s_gla)
    y_gla = (group_norm(o_gla, gla_norm) * jax.nn.silu(g_r.astype(jnp.float32))).astype(dt_)

    xpad = jnp.concatenate([conv_buf.astype(dt_), m_xbc], axis=1)
    conv = ssd_conv_b
    for j in range(SSD_CONV):
        conv = conv + xpad[:, j:j + T] * ssd_conv_w[j]
    conv_new = xpad[:, -(SSD_CONV - 1):]
    xbc = jax.nn.silu(conv)
    xs, Bm, Cm = jnp.split(xbc, [SSD_INNER, SSD_INNER + SSD_GROUPS * SSD_STATE], axis=-1)
    xs = xs.reshape(B, T, SSD_HEADS, SSD_HEAD_DIM)
    rep = SSD_HEADS // SSD_GROUPS
    Bm = jnp.repeat(Bm.reshape(B, T, SSD_GROUPS, SSD_STATE), rep, axis=2)
    Cm = jnp.repeat(Cm.reshape(B, T, SSD_GROUPS, SSD_STATE), rep, axis=2)
    dt = jax.nn.softplus(m_dt.astype(jnp.float32) + ssd_dt_bias.astype(jnp.float32))
    A = -jnp.exp(ssd_a_log.astype(jnp.float32))
    y, s_ssd_new = scan_scalar_decay(Cm, Bm * dt[..., None], xs, dt * A, s_ssd)
    y = y + ssd_d.astype(jnp.float32)[:, None] * xs.astype(jnp.float32)
    y = y.reshape(B, T, SSD_INNER) * jax.nn.silu(m_z.astype(jnp.float32))
    y_ssd = rmsnorm(y, ssd_norm).astype(dt_)

    rq = rope(r_q.reshape(B, T, RET_HEADS, RET_DK), pos)
    rk = rope(r_k.reshape(B, T, RET_HEADS, RET_DK), pos) * (RET_DK ** -0.5)
    rv = r_v.reshape(B, T, RET_HEADS, RET_DV)
    log_gamma = jnp.log1p(-jnp.exp2(-5.0 - jnp.arange(RET_HEADS, dtype=jnp.float32)))
    o_ret, s_ret_new = scan_scalar_decay(rq, rk, rv, jnp.broadcast_to(log_gamma, (B, T, RET_HEADS)), s_ret)
    y_ret = (group_norm(o_ret, ret_norm) * jax.nn.silu(r_g.astype(jnp.float32))).astype(dt_)

    m = (jax.nn.sigmoid(gate_a) * (y_gla @ w_branch_gla)
         + jax.nn.sigmoid(gate_b) * (y_ssd @ w_branch_ssd)
         + jax.nn.sigmoid(gate_c) * (y_ret @ w_branch_ret))
    out = m @ w_out
    return out, (s_gla_new.astype(dt_), s_ssd_new.astype(dt_), conv_new, s_ret_new.astype(dt_))


def setup_inputs(seed: int = 0) -> dict:
    key = jax.random.key(seed)
    ks = jax.random.split(key, 32)
    f32 = jnp.float32

    def nrm(k, shape, scale):
        return jax.random.normal(k, shape, f32) * scale

    def gain(k, shape):
        return 1.0 + 0.05 * jax.random.normal(k, shape, f32)

    dt0 = jnp.exp(jax.random.uniform(ks[14], (DEPTH, SSD_HEADS), f32, math.log(1e-3), math.log(1e-1)))
    return {
        'x_prompt': nrm(ks[0], (BATCH, SEQ, D_MODEL), 1.0),
        'x_sample': nrm(ks[1], (DEC_BATCH, DEC_SEQ, D_MODEL), 1.0),
        'state_gla': nrm(ks[2], (DEPTH, DEC_BATCH, GLA_HEADS, GLA_DK, GLA_DV), 0.1),
        'state_ssd': nrm(ks[3], (DEPTH, DEC_BATCH, SSD_HEADS, SSD_STATE, SSD_HEAD_DIM), 0.1),
        'cache_conv': nrm(ks[4], (DEPTH, DEC_BATCH, SSD_CONV - 1, SSD_CONV_DIM), 1.0),
        'state_ret': nrm(ks[5], (DEPTH, DEC_BATCH, RET_HEADS, RET_DK, RET_DV), 0.1),
        'norm_ffn1': gain(ks[6], (DEPTH, D_MODEL)),
        'ffn1_w_in': nrm(ks[7], (DEPTH, D_MODEL, 2 * D_FF), D_MODEL ** -0.5),
        'ffn1_w_out': nrm(ks[8], (DEPTH, D_FF, D_MODEL), D_FF ** -0.5),
        'norm_mix': gain(ks[9], (DEPTH, D_MODEL)),
        'w_in': nrm(ks[10], (DEPTH, D_MODEL, IN_COLS), D_MODEL ** -0.5),
        'gla_w_gate2': nrm(ks[11], (DEPTH, GLA_RANK, GLA_HEADS * GLA_DK), GLA_RANK ** -0.5),
        'gla_b_gate': nrm(ks[12], (DEPTH, GLA_HEADS * GLA_DK), 0.1),
        'gla_norm': gain(ks[13], (DEPTH, GLA_HEADS * GLA_DV)),
        'ssd_conv_w': nrm(ks[15], (DEPTH, SSD_CONV, SSD_CONV_DIM), SSD_CONV ** -0.5),
        'ssd_conv_b': nrm(ks[16], (DEPTH, SSD_CONV_DIM), 0.02),
        'ssd_dt_bias': dt0 + jnp.log(-jnp.expm1(-dt0)),
        'ssd_a_log': jnp.log(jax.random.uniform(ks[17], (DEPTH, SSD_HEADS), f32, 1.0, 16.0)),
        'ssd_d': gain(ks[18], (DEPTH, SSD_HEADS)),
        'ssd_norm': gain(ks[19], (DEPTH, SSD_INNER)),
        'ret_norm': gain(ks[20], (DEPTH, RET_HEADS * RET_DV)),
        'w_branch_gla': nrm(ks[21], (DEPTH, GLA_HEADS * GLA_DV, D_MODEL), (GLA_HEADS * GLA_DV) ** -0.5),
        'w_branch_ssd': nrm(ks[22], (DEPTH, SSD_INNER, D_MODEL), SSD_INNER ** -0.5),
        'w_branch_ret': nrm(ks[23], (DEPTH, RET_HEADS * RET_DV, D_MODEL), (RET_HEADS * RET_DV) ** -0.5),
        'w_out': nrm(ks[24], (DEPTH, D_MODEL, D_MODEL), D_MODEL ** -0.5),
        'norm_ffn2': gain(ks[25], (DEPTH, D_MODEL)),
        'ffn2_w_in': nrm(ks[26], (DEPTH, D_MODEL, 2 * D_FF), D_MODEL ** -0.5),
        'ffn2_w_out': nrm(ks[27], (DEPTH, D_FF, D_MODEL), D_FF ** -0.5),
        'norm_final': gain(ks[28], (D_MODEL,)),
    }


def reference(x_prompt, x_sample, state_gla, state_ssd, cache_conv, state_ret,
              norm_ffn1, ffn1_w_in, ffn1_w_out, norm_mix, w_in, gla_w_gate2, gla_b_gate, gla_norm,
              ssd_conv_w, ssd_conv_b, ssd_dt_bias, ssd_a_log, ssd_d, ssd_norm, ret_norm,
              w_branch_gla, w_branch_ssd, w_branch_ret, w_out, norm_ffn2, ffn2_w_in, ffn2_w_out,
              norm_final):

    def run_layer(l, x, pos, s_gla, s_ssd, buf, s_ret):
        x = x + 0.5 * swiglu(rmsnorm(x, norm_ffn1[l]), ffn1_w_in[l], ffn1_w_out[l])
        mix, st = token_mixers(rmsnorm(x, norm_mix[l]), pos, s_gla, s_ssd, buf, s_ret, w_in[l],
                               gla_w_gate2[l], gla_b_gate[l], gla_norm[l], ssd_conv_w[l], ssd_conv_b[l],
                               ssd_dt_bias[l], ssd_a_log[l], ssd_d[l], ssd_norm[l], ret_norm[l],
                               w_branch_gla[l], w_branch_ssd[l], w_branch_ret[l], w_out[l])
        x = x + mix
        x = x + 0.5 * swiglu(rmsnorm(x, norm_ffn2[l]), ffn2_w_in[l], ffn2_w_out[l])
        return x, st

    Bp, Tp = x_prompt.shape[:2]
    Ts = x_sample.shape[1]
    dt_ = x_prompt.dtype
    pos_p = jnp.arange(Tp, dtype=jnp.int32)
    pos_s = PAST_LEN + jnp.arange(Ts, dtype=jnp.int32)
    xp, xs = x_prompt, x_sample
    new_p, new_s = [], []
    for l in range(DEPTH):
        xp, st_p = run_layer(l, xp, pos_p,
                             jnp.zeros((Bp, GLA_HEADS, GLA_DK, GLA_DV), dt_),
                             jnp.zeros((Bp, SSD_HEADS, SSD_STATE, SSD_HEAD_DIM), dt_),
                             jnp.zeros((Bp, SSD_CONV - 1, SSD_CONV_DIM), dt_),
                             jnp.zeros((Bp, RET_HEADS, RET_DK, RET_DV), dt_))
        xs, st_s = run_layer(l, xs, pos_s, state_gla[l], state_ssd[l], cache_conv[l], state_ret[l])
        new_p.append(st_p)
        new_s.append(st_s)
    y_prompt = rmsnorm(xp, norm_final)
    y_sample = rmsnorm(xs, norm_final)
    gla_p = jnp.stack([s[0] for s in new_p])
    ssd_p = jnp.stack([s[1] for s in new_p])
    conv_p = jnp.stack([s[2] for s in new_p])
    ret_p = jnp.stack([s[3] for s in new_p])
    gla_s = jnp.stack([s[0] for s in new_s])
    ssd_s = jnp.stack([s[1] for s in new_s])
    conv_s = jnp.stack([s[2] for s in new_s])
    ret_s = jnp.stack([s[3] for s in new_s])
    return (y_prompt, y_sample, gla_p, ssd_p, conv_p, ret_p, gla_s, ssd_s, conv_s, ret_s)
```

```python
import functools
import math

import jax
import jax.numpy as jnp
from jax import lax
from jax.experimental import pallas as pl
from jax.experimental.pallas import tpu as pltpu

F32 = jnp.float32
BF16 = jnp.bfloat16

EPS = 1e-6
PAST_LEN = 1024
GLA_TAU = 16.0
ROPE_BASE = 10000.0
SCAN_CHUNK = 64

LANES = 128
VMEM_LIMIT_BYTES = 56 * 1024 * 1024


def _dot(a, b):
    return jnp.dot(a, b, preferred_element_type=F32)


def _dot_nt(a, b):
    return lax.dot_general(a, b, (((1,), (1,)), ((), ())), preferred_element_type=F32)


def _dot_tn(a, b):
    return lax.dot_general(a, b, (((0,), (0,)), ((), ())), preferred_element_type=F32)


def _split3(x):
    hi = x.astype(BF16)
    r1 = x - hi.astype(F32)
    mid = r1.astype(BF16)
    lo = (r1 - mid.astype(F32)).astype(BF16)
    return hi, mid, lo


def _dot_exact01(m01, x):
    hi, mid, lo = _split3(x)
    return _dot(m01, hi) + _dot(m01, mid) + _dot(m01, lo)


def _rms(x, g):
    return x * lax.rsqrt(jnp.mean(x * x, axis=-1, keepdims=True) + EPS) * g


def _silu(x):
    return x * jax.nn.sigmoid(x)


def _softplus(x):
    return jnp.maximum(x, 0.0) + jnp.log1p(jnp.exp(-jnp.abs(x)))


def _log_sigmoid(x):
    return jnp.minimum(x, 0.0) - jnp.log1p(jnp.exp(-jnp.abs(x)))


def _iota(shape, axis):
    return lax.broadcasted_iota(jnp.int32, shape, axis)


def _log2(n):
    k = int(math.log2(n))
    assert (1 << k) == n, f"{n} must be a power of two"
    return k


def _ffn_kernel(x_ref, g_ref, wg_ref, wu_ref, wo_ref, gf_ref, o_ref, h_ref, a_ref, *, fc, final_norm):
    x = x_ref[...]
    h_ref[...] = _rms(x, g_ref[...]).astype(BF16)
    d_ff = wg_ref.shape[1]
    for c in range(d_ff // fc):
        sl = slice(c * fc, (c + 1) * fc)
        gate = _dot(h_ref[...], wg_ref[:, sl])
        up = _dot(h_ref[...], wu_ref[:, sl])
        a_ref[:, sl] = (_silu(gate) * up).astype(BF16)
    y = x + 0.5 * _dot(a_ref[...], wo_ref[...])
    if final_norm:
        y = _rms(y, gf_ref[...])
    o_ref[...] = y


def _const_spec(shape):
    nd = len(shape)
    return pl.BlockSpec(shape, lambda *_: (0,) * nd, pipeline_mode=pl.Buffered(1))


def _ffn(x2d, g, w_gate, w_up, w_out, g_final, *, tm, fc, final_norm):
    m, d = x2d.shape
    d_ff = w_gate.shape[1]
    assert m % tm == 0 and d_ff % fc == 0
    kern = functools.partial(_ffn_kernel, fc=fc, final_norm=final_norm)
    return pl.pallas_call(
        kern,
        grid=(m // tm,),
        in_specs=[
            pl.BlockSpec((tm, d), lambda i: (i, 0)),
            _const_spec((1, d)),
            _const_spec((d, d_ff)),
            _const_spec((d, d_ff)),
            _const_spec((d_ff, d)),
            _const_spec((1, d)),
        ],
        out_specs=pl.BlockSpec((tm, d), lambda i: (i, 0)),
        out_shape=jax.ShapeDtypeStruct((m, d), F32),
        scratch_shapes=[pltpu.VMEM((tm, d), BF16), pltpu.VMEM((tm, d_ff), BF16)],
        compiler_params=pltpu.CompilerParams(
            dimension_semantics=("arbitrary",), vmem_limit_bytes=VMEM_LIMIT_BYTES),
        name="ffn",
    )(x2d, g, w_gate, w_up, w_out, g_final)


class _Dims:
    def __init__(self, state_gla, state_ssd, cache_conv, state_ret, gla_w_gate2, ssd_conv_w, d_model):
        _, _, self.gh, self.gn, self.gp = state_gla.shape
        _, _, self.sh, self.sn, self.sp = state_ssd.shape
        _, _, self.rh, self.rn, self.rp = state_ret.shape
        self.conv_k = ssd_conv_w.shape[1]
        self.conv_dim = cache_conv.shape[-1]
        self.rank = gla_w_gate2.shape[1]
        self.d = d_model
        self.s_inner = self.sh * self.sp
        self.sg = (self.conv_dim - self.s_inner) // (2 * self.sn)
        assert self.gh * self.gn == 2 * LANES and self.gp == LANES
        assert self.rh * self.rn == 2 * LANES and self.rp == LANES
        assert self.sp * 2 == LANES and self.sg * self.sn == LANES and self.sh <= LANES
        assert self.rank <= LANES and self.conv_k - 1 <= 8
        w = [self.gh * self.gn, self.gh * self.gn, self.gh * self.gp, self.gh * self.gp, LANES,
             self.s_inner, self.conv_dim, LANES,
             self.rh * self.rn, self.rh * self.rn, self.rh * self.rp, self.rh * self.rp,
             d_model, d_model, d_model]
        names = ["gq", "gk", "gv", "gr", "glr", "mz", "mxbc", "mdt", "rq", "rk", "rv", "rg", "ga", "gb", "gc"]
        self.seg = {}
        off = 0
        for n, wd in zip(names, w):
            self.seg[n] = (off, off + wd)
            off += wd
        self.in_cols = off


def _head_lane_mask(shape, h, width):
    lane = _iota(shape, 1)
    return (lane >> _log2(width)) == h


def _stack_heads(x, nh, width):
    parts = [jnp.where(_head_lane_mask(x.shape, h, width), x, 0.0).astype(BF16) for h in range(nh)]
    return jnp.concatenate(parts, axis=0)


def _linear_attention_chunk(q_sc, k_sc, sc_mul, q_in, k_up, v, dec_row, st_ref, nh, n, p):
    L = q_sc.shape[0]
    k_big = _stack_heads(k_sc, nh, n)
    sc = _dot_nt(q_sc.astype(BF16), k_big)
    lane = _iota(sc.shape, 1)
    row = _iota(sc.shape, 0)
    causal = (lane & (L - 1)) <= row
    sc = jnp.where(causal, sc if sc_mul is None else sc * sc_mul, 0.0)
    v_big = _stack_heads(v, nh, p)
    st = st_ref[...]
    o = _dot(sc.astype(BF16), v_big) + _dot_nt(q_in.astype(BF16), st.astype(BF16))
    upd = _dot_tn(v.astype(BF16), k_up.astype(BF16))
    blk = (_iota(upd.shape, 0) >> _log2(p)) == (_iota(upd.shape, 1) >> _log2(n))
    st_ref[...] = st * dec_row + jnp.where(blk, upd, 0.0)
    return o


def _group_norm_heads(o, nh, p):
    outs = []
    for h in range(nh):
        oh = o[:, h * p:(h + 1) * p]
        mu = jnp.mean(oh, axis=-1, keepdims=True)
        dlt = oh - mu
        var = jnp.mean(dlt * dlt, axis=-1, keepdims=True)
        outs.append(dlt * lax.rsqrt(var + EPS))
    return jnp.concatenate(outs, axis=1)


def _block_tril(tt, L):
    row = _iota((tt, tt), 0)
    col = _iota((tt, tt), 1)
    k = _log2(L)
    return jnp.where((col <= row) & ((row >> k) == (col >> k)), 1.0, 0.0).astype(BF16)


def _ret_log_gamma(head, nh):
    out = jnp.full(head.shape, math.log1p(-2.0 ** (-5.0 - (nh - 1))), F32)
    for h in range(nh - 1):
        out = jnp.where(head == h, math.log1p(-2.0 ** (-5.0 - h)), out)
    return out


def _transpose_rows(x):
    L = x.shape[0]
    if L % LANES:
        pad = LANES - L % LANES
        x = jnp.concatenate([x, jnp.zeros((pad, x.shape[1]), x.dtype)], axis=0)
    return x.T[:, :L]


def _swap_halves(x, n):
    width = x.shape[1]
    half = n // 2
    first = (_iota(x.shape, 1) & (n - 1)) < half
    return jnp.where(first, pltpu.roll(x, width - half, 1), pltpu.roll(x, half, 1))


def _mixer_kernel(x_ref, cos_ref, sin_ref, sg0_ref, ss0_ref, cv0_ref, sr0_ref,
                  nrm_ref, win_ref, wg2_ref, bg_ref, gnorm_ref, cw_ref, cb_ref, dtb_ref, alog_ref,
                  dexp_ref, snorm_ref, rnorm_ref, wbg_ref, wbs_ref, wbr_ref, wout_ref,
                  y_ref, sg_ref, ss_ref, cv_ref, sr_ref,
                  stg, sts, cbuf, strt, ogla, ossd, oret,
                  *, dm, tt, lg, ls, lr):
    t = pl.program_id(1)
    nt = pl.num_programs(1)
    ck = dm.conv_k - 1

    @pl.when(t == 0)
    def _():
        s0 = sg0_ref[...]
        stg[...] = jnp.concatenate(
            [jnp.where(_head_lane_mask(s0.shape, h, dm.gn), s0, 0.0) for h in range(dm.gh)], axis=0)
        r0 = sr0_ref[...]
        strt[...] = jnp.concatenate(
            [jnp.where(_head_lane_mask(r0.shape, h, dm.rn), r0, 0.0) for h in range(dm.rh)], axis=0)
        z0 = ss0_ref[...]
        hpg = dm.sh // dm.sg
        sts[...] = jnp.concatenate(
            [jnp.where(_head_lane_mask(z0.shape, g, dm.sp * hpg), z0, 0.0) for g in range(dm.sg)], axis=0)
        cbuf[8 - ck:8, :] = cv0_ref[...]

    x = x_ref[...]
    h = _rms(x, nrm_ref[...]).astype(BF16)

    def proj(name):
        a, b = dm.seg[name]
        return _dot(h, win_ref[:, a:b])

    q = proj("gq") * (dm.gn ** -0.5)
    k = proj("gk")
    v = proj("gv")
    lr_ = proj("glr")
    gate = _dot(lr_.astype(BF16), wg2_ref[...]) + bg_ref[...]
    log_a = _log_sigmoid(gate) * (1.0 / GLA_TAU)
    b_all = _dot_exact01(_block_tril(tt, lg), log_a)
    qd_all = q * jnp.exp(b_all)
    kd_all = k * jnp.exp(-b_all)
    for c in range(tt // lg):
        r = slice(c * lg, (c + 1) * lg)
        b = b_all[r]
        bl = b[lg - 1:lg, :]
        ks = k[r] * jnp.exp(bl - b)
        ogla[r, :] = _linear_attention_chunk(qd_all[r], kd_all[r], None, qd_all[r], ks, v[r],
                                             jnp.exp(bl), stg, dm.gh, dm.gn, dm.gp)
    y_gla = (_group_norm_heads(ogla[...], dm.gh, dm.gp) * gnorm_ref[...] * _silu(proj("gr"))).astype(BF16)

    xbc_pre = proj("mxbc")
    cbuf[8:8 + tt, :] = xbc_pre
    conv = cb_ref[...] + cw_ref[ck:ck + 1, :] * xbc_pre
    for j in range(ck):
        conv = conv + cw_ref[j:j + 1, :] * cbuf[8 - ck + j:8 - ck + j + tt, :]
    new_carry = cbuf[8 + tt - ck:8 + tt, :]
    cbuf[8 - ck:8, :] = new_carry
    xbc = _silu(conv)
    xs = xbc[:, :dm.s_inner]
    bm = xbc[:, dm.s_inner:dm.s_inner + LANES]
    cm = xbc[:, dm.s_inner + LANES:dm.s_inner + 2 * LANES]
    dt8 = _softplus(proj("mdt") + dtb_ref[...])
    a8 = dt8 * (-jnp.exp(alog_ref[...]))
    b8_all = _dot_exact01(_block_tril(tt, ls), a8)
    hpg = dm.sh // dm.sg
    for c in range(tt // ls):
        r = slice(c * ls, (c + 1) * ls)
        b8 = b8_all[r]
        dtc = dt8[r]
        bl8 = b8[ls - 1:ls, :]
        w8 = dtc * jnp.exp(bl8 - b8)
        b8t = _transpose_rows(b8)
        dtt = _transpose_rows(dtc)
        xs_c = xs[r]
        xs_bf = xs_c.astype(BF16)
        cm_c = cm[r]
        bm_c = bm[r].astype(BF16)
        lane8 = _iota(b8.shape, 1)
        row = _iota((ls, ls), 0)
        col = _iota((ls, ls), 1)
        causal = col <= row
        gmats = [_dot_nt(jnp.where(_head_lane_mask(cm_c.shape, g, dm.sn), cm_c, 0.0).astype(BF16), bm_c)
                 for g in range(dm.sg)]
        eb_cols, w_cols, y_pairs = [], [], []
        for hh in range(dm.sh):
            bcol = jnp.sum(jnp.where(lane8 == hh, b8, 0.0), axis=-1, keepdims=True)
            wcol = jnp.sum(jnp.where(lane8 == hh, w8, 0.0), axis=-1, keepdims=True)
            eb_cols.append(jnp.exp(bcol))
            w_cols.append(wcol)
            seg = jnp.where(causal, bcol - b8t[hh:hh + 1, :], -jnp.inf)
            m_h = gmats[hh // hpg] * dtt[hh:hh + 1, :] * jnp.exp(seg)
            pr = hh // 2
            y_pairs.append(_dot(m_h.astype(BF16), xs_bf[:, pr * LANES:(pr + 1) * LANES]))
        lane_p = _iota((ls, LANES), 1)
        first = lane_p < dm.sp

        def pair_lanes(cols):
            return jnp.concatenate(
                [jnp.where(first, cols[2 * j], cols[2 * j + 1]) for j in range(dm.sh // 2)], axis=1)

        y_intra = jnp.concatenate(
            [jnp.where(first, y_pairs[2 * j], y_pairs[2 * j + 1]) for j in range(dm.sh // 2)], axis=1)
        eb_exp = pair_lanes([jnp.broadcast_to(e, (ls, LANES)) for e in eb_cols])
        w_exp = pair_lanes([jnp.broadcast_to(w, (ls, LANES)) for w in w_cols])
        st = sts[...]
        y_inter = _dot(cm_c.astype(BF16), st.astype(BF16)) * eb_exp
        upd = _dot_tn(bm_c, (xs_c * w_exp).astype(BF16))
        blk = (_iota(upd.shape, 0) >> _log2(dm.sn)) == (_iota(upd.shape, 1) >> _log2(dm.sp * hpg))
        dec_cols = []
        for hh in range(dm.sh):
            dec_cols.append(jnp.broadcast_to(
                jnp.sum(jnp.where(_iota(bl8.shape, 1) == hh, jnp.exp(bl8), 0.0), axis=-1, keepdims=True),
                (1, LANES)))
        lane_1 = _iota((1, LANES), 1) < dm.sp
        dec_row = jnp.concatenate(
            [jnp.where(lane_1, dec_cols[2 * j], dec_cols[2 * j + 1]) for j in range(dm.sh // 2)], axis=1)
        sts[...] = st * dec_row + jnp.where(blk, upd, 0.0)
        ossd[r, :] = y_intra + y_inter
    y = ossd[...] + dexp_ref[...] * xs
    y = y * _silu(proj("mz"))
    y_ssd = _rms(y, snorm_ref[...]).astype(BF16)

    cos = cos_ref[...]
    sin = sin_ref[...]
    rq_pre = proj("rq")
    rk_pre = proj("rk")
    rq = rq_pre * cos + _swap_halves(rq_pre, dm.rn) * sin
    rk = (rk_pre * cos + _swap_halves(rk_pre, dm.rn) * sin) * (dm.rn ** -0.5)
    rv = proj("rv")
    nlane = dm.rh * dm.rn
    lgam_q = _ret_log_gamma(_iota((lr, nlane), 1) >> _log2(dm.rn), dm.rh)
    pos_q = _iota((lr, nlane), 0).astype(F32)
    q_dec = jnp.exp((pos_q + 1.0) * lgam_q)
    k_dec = jnp.exp((lr - 1.0 - pos_q) * lgam_q)
    st_dec = jnp.exp(lr * lgam_q[0:1, :])
    lgam_s = _ret_log_gamma(_iota((lr, dm.rh * lr), 1) >> _log2(lr), dm.rh)
    dpos = (_iota((lr, dm.rh * lr), 0) - (_iota((lr, dm.rh * lr), 1) & (lr - 1))).astype(F32)
    sc_dec = jnp.exp(jnp.maximum(dpos, 0.0) * lgam_s)
    for c in range(tt // lr):
        r = slice(c * lr, (c + 1) * lr)
        oret[r, :] = _linear_attention_chunk(rq[r], rk[r], sc_dec, rq[r] * q_dec, rk[r] * k_dec, rv[r],
                                             st_dec, strt, dm.rh, dm.rn, dm.rp)
    y_ret = (_group_norm_heads(oret[...], dm.rh, dm.rp) * rnorm_ref[...] * _silu(proj("rg"))).astype(BF16)

    m = (jax.nn.sigmoid(proj("ga")) * _dot(y_gla, wbg_ref[...])
         + jax.nn.sigmoid(proj("gb")) * _dot(y_ssd, wbs_ref[...])
         + jax.nn.sigmoid(proj("gc")) * _dot(y_ret, wbr_ref[...]))
    y_ref[...] = x + _dot(m.astype(BF16), wout_ref[...])

    @pl.when(t == nt - 1)
    def _():
        sg = stg[...]
        acc = sg[0:dm.gp]
        for hh in range(1, dm.gh):
            acc = acc + sg[hh * dm.gp:(hh + 1) * dm.gp]
        sg_ref[...] = acc
        sr = strt[...]
        acc = sr[0:dm.rp]
        for hh in range(1, dm.rh):
            acc = acc + sr[hh * dm.rp:(hh + 1) * dm.rp]
        sr_ref[...] = acc
        sz = sts[...]
        acc = sz[0:dm.sn]
        for g in range(1, dm.sg):
            acc = acc + sz[g * dm.sn:(g + 1) * dm.sn]
        ss_ref[...] = acc
        cv_ref[...] = cbuf[8 - ck:8, :]


def _mixer(x, cos, sin, sg0, ss0, cv0, sr0, wts, dm, *, tt, lg, ls, lr):
    bsz, seq, d = x.shape
    assert seq % tt == 0 and tt % lg == 0 and tt % ls == 0 and tt % lr == 0
    kern = functools.partial(_mixer_kernel, dm=dm, tt=tt, lg=lg, ls=ls, lr=lr)
    gq = dm.gh * dm.gn
    rq = dm.rh * dm.rn

    def per_b(shape):
        nd = len(shape)
        return pl.BlockSpec((None,) + shape, lambda b, t: (b,) + (0,) * nd)

    in_specs = [
        pl.BlockSpec((None, tt, d), lambda b, t: (b, t, 0)),
        pl.BlockSpec((tt, rq), lambda b, t: (t, 0)),
        pl.BlockSpec((tt, rq), lambda b, t: (t, 0)),
        per_b((dm.gp, gq)), per_b((dm.sn, dm.s_inner)), per_b((dm.conv_k - 1, dm.conv_dim)), per_b((dm.rp, rq)),
    ] + [_const_spec(w.shape) for w in wts]
    out_specs = [
        pl.BlockSpec((None, tt, d), lambda b, t: (b, t, 0)),
        per_b((dm.gp, gq)), per_b((dm.sn, dm.s_inner)), per_b((dm.conv_k - 1, dm.conv_dim)), per_b((dm.rp, rq)),
    ]
    out_shape = [
        jax.ShapeDtypeStruct((bsz, seq, d), F32),
        jax.ShapeDtypeStruct((bsz, dm.gp, gq), F32),
        jax.ShapeDtypeStruct((bsz, dm.sn, dm.s_inner), F32),
        jax.ShapeDtypeStruct((bsz, dm.conv_k - 1, dm.conv_dim), F32),
        jax.ShapeDtypeStruct((bsz, dm.rp, rq), F32),
    ]
    scratch = [
        pltpu.VMEM((dm.gh * dm.gp, gq), F32),
        pltpu.VMEM((dm.sg * dm.sn, dm.s_inner), F32),
        pltpu.VMEM((tt + 8, dm.conv_dim), F32),
        pltpu.VMEM((dm.rh * dm.rp, rq), F32),
        pltpu.VMEM((tt, dm.gh * dm.gp), F32),
        pltpu.VMEM((tt, dm.s_inner), F32),
        pltpu.VMEM((tt, dm.rh * dm.rp), F32),
    ]
    return pl.pallas_call(
        kern,
        grid=(bsz, seq // tt),
        in_specs=in_specs,
        out_specs=out_specs,
        out_shape=out_shape,
        scratch_shapes=scratch,
        compiler_params=pltpu.CompilerParams(
            dimension_semantics=("arbitrary", "arbitrary"), vmem_limit_bytes=VMEM_LIMIT_BYTES),
        name="mixer",
    )(x, cos, sin, sg0, ss0, cv0, sr0, *wts)


def _pad_cols(w, n):
    return jnp.pad(w, ((0, 0), (0, n - w.shape[1])))


def _row(v, n=None):
    v = v.reshape(1, -1).astype(F32)
    return v if n is None else _pad_cols(v, n)


def _rope_tables(pos, dm):
    half = dm.rn // 2
    freqs = ROPE_BASE ** (-jnp.arange(half, dtype=F32) / half)
    ang = pos.astype(F32)[:, None] * freqs[None, :]
    cos = jnp.cos(ang)
    sin = jnp.sin(ang)
    cos_h = jnp.concatenate([cos, cos], axis=1)
    sin_h = jnp.concatenate([-sin, sin], axis=1)
    return jnp.tile(cos_h, (1, dm.rh)), jnp.tile(sin_h, (1, dm.rh))


def _state_to_kernel_t(s):
    b, h, n, p = s.shape
    return jnp.transpose(s, (0, 3, 1, 2)).reshape(b, p, h * n)


def _state_from_kernel_t(s, h):
    b, p, hn = s.shape
    return jnp.transpose(s.reshape(b, p, h, hn // h), (0, 2, 3, 1))


def _ssd_to_kernel(s):
    b, h, n, p = s.shape
    return jnp.transpose(s, (0, 2, 1, 3)).reshape(b, n, h * p)


def _ssd_from_kernel(s, h):
    b, n, hp = s.shape
    return jnp.transpose(s.reshape(b, n, h, hp // h), (0, 2, 1, 3))


def _tiles(seq):
    if seq % SCAN_CHUNK == 0:
        tt = 256 if seq % 256 == 0 else SCAN_CHUNK
        return dict(tt=tt, lg=SCAN_CHUNK, ls=min(tt, 128), lr=min(tt, 128))
    return dict(tt=seq, lg=seq, ls=seq, lr=seq)


def kernel(x_prompt, x_sample, state_gla, state_ssd, cache_conv, state_ret, norm_ffn1, ffn1_w_in, ffn1_w_out,
           norm_mix, w_in, gla_w_gate2, gla_b_gate, gla_norm, ssd_conv_w, ssd_conv_b, ssd_dt_bias, ssd_a_log,
           ssd_d, ssd_norm, ret_norm, w_branch_gla, w_branch_ssd, w_branch_ret, w_out, norm_ffn2, ffn2_w_in,
           ffn2_w_out, norm_final):
    depth = w_in.shape[0]
    d = x_prompt.shape[-1]
    d_ff = ffn1_w_out.shape[1]
    dm = _Dims(state_gla, state_ssd, cache_conv, state_ret, gla_w_gate2, ssd_conv_w, d)

    split_sizes = (dm.gh * dm.gn, dm.gh * dm.gn, dm.gh * dm.gp, dm.gh * dm.gp, dm.rank,
                   dm.s_inner, dm.conv_dim, dm.sh,
                   dm.rh * dm.rn, dm.rh * dm.rn, dm.rh * dm.rp, dm.rh * dm.rp, d, d, d)
    pts, acc = [], 0
    for s in split_sizes[:-1]:
        acc += s
        pts.append(acc)

    def mixer_weights(l):
        segs = jnp.split(w_in[l], pts, axis=1)
        segs[4] = _pad_cols(segs[4], LANES)
        segs[7] = _pad_cols(segs[7], LANES)
        win = jnp.concatenate(segs, axis=1).astype(BF16)
        wg2 = jnp.pad(gla_w_gate2[l], ((0, LANES - dm.rank), (0, 0))).astype(BF16)
        return [
            _row(norm_mix[l]), win, wg2, _row(gla_b_gate[l]), _row(gla_norm[l]),
            ssd_conv_w[l].astype(F32), _row(ssd_conv_b[l]), _row(ssd_dt_bias[l], LANES), _row(ssd_a_log[l], LANES),
            _row(jnp.repeat(ssd_d[l], dm.sp)), _row(ssd_norm[l]), _row(ret_norm[l]),
            w_branch_gla[l].astype(BF16), w_branch_ssd[l].astype(BF16), w_branch_ret[l].astype(BF16),
            w_out[l].astype(BF16),
        ]

    def ffn_weights(nrm, w_i, w_o, l):
        return _row(nrm[l]), w_i[l][:, :d_ff].astype(BF16), w_i[l][:, d_ff:].astype(BF16), w_o[l].astype(BF16)

    g_final = _row(norm_final)

    def run_group(x, pos, states):
        bsz, seq, _ = x.shape
        m = bsz * seq
        tm = 512 if m % 512 == 0 else m
        cos, sin = _rope_tables(pos, dm)
        tl = _tiles(seq)
        new_states = []
        for l in range(depth):
            sg0, ss0, cv0, sr0 = states(l)
            g1, wg1, wu1, wo1 = ffn_weights(norm_ffn1, ffn1_w_in, ffn1_w_out, l)
            x = _ffn(x.reshape(m, d), g1, wg1, wu1, wo1, g_final, tm=tm, fc=256, final_norm=False).reshape(bsz, seq, d)
            x, sg, ss, cv, sr = _mixer(x, cos, sin, _state_to_kernel_t(sg0), _ssd_to_kernel(ss0), cv0,
                                       _state_to_kernel_t(sr0), mixer_weights(l), dm, **tl)
            g2, wg2_, wu2, wo2 = ffn_weights(norm_ffn2, ffn2_w_in, ffn2_w_out, l)
            x = _ffn(x.reshape(m, d), g2, wg2_, wu2, wo2, g_final, tm=tm, fc=256,
                     final_norm=(l == depth - 1)).reshape(bsz, seq, d)
            new_states.append((_state_from_kernel_t(sg, dm.gh), _ssd_from_kernel(ss, dm.sh), cv,
                               _state_from_kernel_t(sr, dm.rh)))
        return x, new_states

    bp, tp = x_prompt.shape[:2]
    ts = x_sample.shape[1]
    dt_ = x_prompt.dtype

    def zero_states(_):
        return (jnp.zeros((bp, dm.gh, dm.gn, dm.gp), dt_), jnp.zeros((bp, dm.sh, dm.sn, dm.sp), dt_),
                jnp.zeros((bp, dm.conv_k - 1, dm.conv_dim), dt_), jnp.zeros((bp, dm.rh, dm.rn, dm.rp), dt_))

    def carried_states(l):
        return state_gla[l], state_ssd[l], cache_conv[l], state_ret[l]

    y_p, st_p = run_group(x_prompt, jnp.arange(tp, dtype=jnp.int32), zero_states)
    y_s, st_s = run_group(x_sample, PAST_LEN + jnp.arange(ts, dtype=jnp.int32), carried_states)

    def stack(sts, i):
        return jnp.stack([s[i] for s in sts])

    return (y_p, y_s, stack(st_p, 0), stack(st_p, 1), stack(st_p, 2), stack(st_p, 3),
            stack(st_s, 0), stack(st_s, 1), stack(st_s, 2), stack(st_s, 3))
```

```python
import functools
import math

import jax
import jax.numpy as jnp
from jax import lax
from jax.experimental import pallas as pl
from jax.experimental.pallas import tpu as pltpu

F32 = jnp.float32
BF16 = jnp.bfloat16

EPS = 1e-6
PAST_LEN = 1024
GLA_TAU = 16.0
ROPE_BASE = 10000.0
SCAN_CHUNK = 64

LANES = 128
SUBLANES = 8
MXU_COLS = 256
VMEM_LIMIT_BYTES = 56 * 1024 * 1024


def _dot(a, b):
    return jnp.dot(a, b, preferred_element_type=F32)


def _dot_nt(a, b):
    return lax.dot_general(a, b, (((1,), (1,)), ((), ())), preferred_element_type=F32)


def _dot_tn(a, b):
    return lax.dot_general(a, b, (((0,), (0,)), ((), ())), preferred_element_type=F32)


def _split3(x):
    hi = x.astype(BF16)
    r1 = x - hi.astype(F32)
    mid = r1.astype(BF16)
    lo = (r1 - mid.astype(F32)).astype(BF16)
    return hi, mid, lo


def _dot_exact01(m01, x):
    hi, mid, lo = _split3(x)
    return _dot(m01, hi) + _dot(m01, mid) + _dot(m01, lo)


def _rms(x, g):
    return x * lax.rsqrt(jnp.mean(x * x, axis=-1, keepdims=True) + EPS) * g


def _silu(x):
    return x * jax.nn.sigmoid(x)


def _softplus(x):
    return jnp.maximum(x, 0.0) + jnp.log1p(jnp.exp(-jnp.abs(x)))


def _log_sigmoid(x):
    return jnp.minimum(x, 0.0) - jnp.log1p(jnp.exp(-jnp.abs(x)))


def _iota(shape, axis):
    return lax.broadcasted_iota(jnp.int32, shape, axis)


def _log2(n):
    k = int(math.log2(n))
    assert (1 << k) == n, f"{n} must be a power of two"
    return k


def _ffn_kernel(x_ref, g_ref, wg_ref, wu_ref, wo_ref, gf_ref, o_ref, h_ref, a_ref, *, fc, final_norm):
    x = x_ref[...]
    h_ref[...] = _rms(x, g_ref[...]).astype(BF16)
    d_ff = wg_ref.shape[1]
    for c in range(d_ff // fc):
        sl = slice(c * fc, (c + 1) * fc)
        gate = _dot(h_ref[...], wg_ref[:, sl])
        up = _dot(h_ref[...], wu_ref[:, sl])
        a_ref[:, sl] = (_silu(gate) * up).astype(BF16)
    y = x + 0.5 * _dot(a_ref[...], wo_ref[...])
    if final_norm:
        y = _rms(y, gf_ref[...])
    o_ref[...] = y


def _const_spec(shape):
    nd = len(shape)
    return pl.BlockSpec(shape, lambda *_: (0,) * nd, pipeline_mode=pl.Buffered(1))


def _ffn(x2d, g, w_gate, w_up, w_out, g_final, *, tm, fc, final_norm):
    m, d = x2d.shape
    d_ff = w_gate.shape[1]
    assert m % tm == 0 and d_ff % fc == 0
    kern = functools.partial(_ffn_kernel, fc=fc, final_norm=final_norm)
    return pl.pallas_call(
        kern,
        grid=(m // tm,),
        in_specs=[
            pl.BlockSpec((tm, d), lambda i: (i, 0)),
            _const_spec((1, d)),
            _const_spec((d, d_ff)),
            _const_spec((d, d_ff)),
            _const_spec((d_ff, d)),
            _const_spec((1, d)),
        ],
        out_specs=pl.BlockSpec((tm, d), lambda i: (i, 0)),
        out_shape=jax.ShapeDtypeStruct((m, d), F32),
        scratch_shapes=[pltpu.VMEM((tm, d), BF16), pltpu.VMEM((tm, d_ff), BF16)],
        compiler_params=pltpu.CompilerParams(
            dimension_semantics=("arbitrary",), vmem_limit_bytes=VMEM_LIMIT_BYTES),
        name="ffn",
    )(x2d, g, w_gate, w_up, w_out, g_final)


class _Dims:
    def __init__(self, state_gla, state_ssd, cache_conv, state_ret, gla_w_gate2, ssd_conv_w, d_model):
        _, _, self.gh, self.gn, self.gp = state_gla.shape
        _, _, self.sh, self.sn, self.sp = state_ssd.shape
        _, _, self.rh, self.rn, self.rp = state_ret.shape
        self.conv_k = ssd_conv_w.shape[1]
        self.conv_dim = cache_conv.shape[-1]
        self.rank = gla_w_gate2.shape[1]
        self.d = d_model
        self.s_inner = self.sh * self.sp
        self.sg = (self.conv_dim - self.s_inner) // (2 * self.sn)
        assert self.gh * self.gn == 2 * LANES and self.gp == LANES
        assert self.rh * self.rn == 2 * LANES and self.rp == LANES
        assert self.sp * 2 == LANES and self.sg * self.sn == LANES and self.sh <= LANES
        assert self.rank <= LANES and self.conv_k - 1 <= SUBLANES
        w = [self.gh * self.gn, self.gh * self.gn, 2 * LANES, self.gh * self.gp, self.gh * self.gp,
             self.s_inner, self.conv_dim,
             self.rh * self.rn, self.rh * self.rn, self.rh * self.rp, self.rh * self.rp,
             d_model, d_model, d_model]
        self.seg_names = ["gq", "gk", "lrdt", "gv", "gr", "mz", "mxbc", "rq", "rk", "rv", "rg", "ga", "gb", "gc"]
        self.seg = {}
        off = 0
        for n, wd in zip(self.seg_names, w):
            assert wd % MXU_COLS == 0
            self.seg[n] = (off, off + wd)
            off += wd
        self.in_cols = off


def _head_lane_mask(shape, h, width):
    lane = _iota(shape, 1)
    return (lane >> _log2(width)) == h


def _stack_heads(x, nh, width):
    parts = [jnp.where(_head_lane_mask(x.shape, h, width), x, 0.0).astype(BF16) for h in range(nh)]
    return jnp.concatenate(parts, axis=0)


def _linear_attention_chunk(q_sc, k_sc, sc_mul, q_in, k_up, v, dec_row, st_ref, nh, n, p, issue):
    L = q_sc.shape[0]
    k_big = _stack_heads(k_sc, nh, n)
    sc = _dot_nt(q_sc.astype(BF16), k_big)
    lane = _iota(sc.shape, 1)
    row = _iota(sc.shape, 0)
    causal = (lane & (L - 1)) <= row
    sc = jnp.where(causal, sc if sc_mul is None else sc * sc_mul, 0.0)
    v_big = _stack_heads(v, nh, p)
    issue()
    st = st_ref[...]
    o = _dot(sc.astype(BF16), v_big) + _dot_nt(q_in.astype(BF16), st.astype(BF16))
    upd = _dot_tn(v.astype(BF16), k_up.astype(BF16))
    blk = (_iota(upd.shape, 0) >> _log2(p)) == (_iota(upd.shape, 1) >> _log2(n))
    st_ref[...] = st * dec_row + jnp.where(blk, upd, 0.0)
    return o


def _group_norm_heads(o, nh, p):
    outs = []
    for h in range(nh):
        oh = o[:, h * p:(h + 1) * p]
        mu = jnp.mean(oh, axis=-1, keepdims=True)
        dlt = oh - mu
        var = jnp.mean(dlt * dlt, axis=-1, keepdims=True)
        outs.append(dlt * lax.rsqrt(var + EPS))
    return jnp.concatenate(outs, axis=1)


def _block_tril(tt, L):
    row = _iota((tt, tt), 0)
    col = _iota((tt, tt), 1)
    k = _log2(L)
    return jnp.where((col <= row) & ((row >> k) == (col >> k)), 1.0, 0.0).astype(BF16)


def _ret_log_gamma(head, nh):
    out = jnp.full(head.shape, math.log1p(-2.0 ** (-5.0 - (nh - 1))), F32)
    for h in range(nh - 1):
        out = jnp.where(head == h, math.log1p(-2.0 ** (-5.0 - h)), out)
    return out


def _retention_tables(dm, L):
    nlane = dm.rh * dm.rn
    lgam_q = _ret_log_gamma(_iota((L, nlane), 1) >> _log2(dm.rn), dm.rh)
    pos_q = _iota((L, nlane), 0).astype(F32)
    q_dec = jnp.exp((pos_q + 1.0) * lgam_q)
    k_dec = jnp.exp((L - 1.0 - pos_q) * lgam_q)
    st_dec = jnp.exp(L * lgam_q[0:1, :])
    lgam_s = _ret_log_gamma(_iota((L, dm.rh * L), 1) >> _log2(L), dm.rh)
    dpos = (_iota((L, dm.rh * L), 0) - (_iota((L, dm.rh * L), 1) & (L - 1))).astype(F32)
    sc_dec = jnp.exp(jnp.maximum(dpos, 0.0) * lgam_s)
    return q_dec, k_dec, st_dec, sc_dec


def _transpose_rows(x):
    L = x.shape[0]
    if L % LANES:
        pad = LANES - L % LANES
        x = jnp.concatenate([x, jnp.zeros((pad, x.shape[1]), x.dtype)], axis=0)
    return x.T[:, :L]


def _swap_halves(x, n):
    width = x.shape[1]
    half = n // 2
    first = (_iota(x.shape, 1) & (n - 1)) < half
    return jnp.where(first, pltpu.roll(x, width - half, 1), pltpu.roll(x, half, 1))


def _mixer_kernel(x_ref, xn_ref, cos_ref, sin_ref, sg0_ref, ss0_ref, cv0_ref, sr0_ref,
                  nrm_ref, win_ref, wg2_ref, bg_ref, gnorm_ref, cw_ref, cb_ref, dtb_ref, alog_ref,
                  dexp_ref, snorm_ref, rnorm_ref, wbg_ref, wbs_ref, wbr_ref, wout_ref,
                  y_ref, sg_ref, ss_ref, cv_ref, sr_ref,
                  stg, sts, cbuf, strt, ogla, ossd, oret, hbuf,
                  *, dm, tt, lg, ls, lr):
    t = pl.program_id(1)
    nt = pl.num_programs(1)
    ck = dm.conv_k - 1

    @pl.when(t == 0)
    def _():
        s0 = sg0_ref[...]
        stg[...] = jnp.concatenate(
            [jnp.where(_head_lane_mask(s0.shape, h, dm.gn), s0, 0.0) for h in range(dm.gh)], axis=0)
        r0 = sr0_ref[...]
        strt[...] = jnp.concatenate(
            [jnp.where(_head_lane_mask(r0.shape, h, dm.rn), r0, 0.0) for h in range(dm.rh)], axis=0)
        z0 = ss0_ref[...]
        hpg = dm.sh // dm.sg
        sts[...] = jnp.concatenate(
            [jnp.where(_head_lane_mask(z0.shape, g, dm.sp * hpg), z0, 0.0) for g in range(dm.sg)], axis=0)
        cbuf[SUBLANES - ck:SUBLANES, :] = cv0_ref[...]

    step = pl.program_id(0) * nt + t
    slot = lax.rem(step, 2)

    @pl.when(step == 0)
    def _():
        hbuf[0] = _rms(x_ref[...], nrm_ref[...]).astype(BF16)

    pending = []
    for name in ["gq", "gk", "lrdt", "gv", "gr", "mxbc", "mz", "rq", "rk", "rv", "rg", "ga", "gb", "gc"]:
        a, b = dm.seg[name]
        pending += [(name, s) for s in range(a, b, MXU_COLS)]
    parts = {name: [] for name in dm.seg_names}

    def issue(n=1):
        for _ in range(n):
            if pending:
                name, s = pending.pop(0)
                parts[name].append(_dot(hbuf[slot], win_ref[:, s:s + MXU_COLS]))

    def take(name):
        a, b = dm.seg[name]
        while len(parts[name]) * MXU_COLS < b - a:
            issue()
        return parts[name][0] if len(parts[name]) == 1 else jnp.concatenate(parts[name], axis=1)

    q = take("gq") * (dm.gn ** -0.5)
    k = take("gk")
    lrdt = take("lrdt")
    hbuf[1 - slot] = _rms(xn_ref[...], nrm_ref[...]).astype(BF16)
    ret_tabs = _retention_tables(dm, lr)
    issue(2)
    gate = _dot(lrdt[:, :LANES].astype(BF16), wg2_ref[...]) + bg_ref[...]
    issue(2)
    log_a = _log_sigmoid(gate) * (1.0 / GLA_TAU)
    issue()
    b_all = _dot_exact01(_block_tril(tt, lg), log_a)
    qd_all = q * jnp.exp(b_all)
    issue()
    kd_all = k * jnp.exp(-b_all)
    v = take("gv")
    for c in range(tt // lg):
        r = slice(c * lg, (c + 1) * lg)
        b = b_all[r]
        bl = b[lg - 1:lg, :]
        ks = k[r] * jnp.exp(bl - b)
        ogla[r, :] = _linear_attention_chunk(qd_all[r], kd_all[r], None, qd_all[r], ks, v[r],
                                             jnp.exp(bl), stg, dm.gh, dm.gn, dm.gp,
                                             issue if c else (lambda: None))
    y_gla = (_group_norm_heads(ogla[...], dm.gh, dm.gp) * gnorm_ref[...] * _silu(take("gr"))).astype(BF16)
    issue(2)

    xbc_pre = take("mxbc")
    cbuf[SUBLANES:SUBLANES + tt, :] = xbc_pre
    conv = cb_ref[...] + cw_ref[ck:ck + 1, :] * xbc_pre
    for j in range(ck):
        conv = conv + cw_ref[j:j + 1, :] * cbuf[SUBLANES - ck + j:SUBLANES - ck + j + tt, :]
    new_carry = cbuf[SUBLANES + tt - ck:SUBLANES + tt, :]
    cbuf[SUBLANES - ck:SUBLANES, :] = new_carry
    issue(2)
    xbc = _silu(conv)
    issue(2)
    xs = xbc[:, :dm.s_inner]
    bm = xbc[:, dm.s_inner:dm.s_inner + LANES]
    cm = xbc[:, dm.s_inner + LANES:dm.s_inner + 2 * LANES]
    dt8 = _softplus(lrdt[:, LANES:] + dtb_ref[...])
    a8 = dt8 * (-jnp.exp(alog_ref[...]))
    b8_all = _dot_exact01(_block_tril(tt, ls), a8)
    hpg = dm.sh // dm.sg
    for c in range(tt // ls):
        r = slice(c * ls, (c + 1) * ls)
        b8 = b8_all[r]
        dtc = dt8[r]
        bl8 = b8[ls - 1:ls, :]
        w8 = dtc * jnp.exp(bl8 - b8)
        b8t = _transpose_rows(b8)
        dtt = _transpose_rows(dtc)
        xs_c = xs[r]
        xs_bf = xs_c.astype(BF16)
        cm_c = cm[r]
        bm_c = bm[r].astype(BF16)
        lane8 = _iota(b8.shape, 1)
        row = _iota((ls, ls), 0)
        col = _iota((ls, ls), 1)
        causal = col <= row
        gmats = [_dot_nt(jnp.where(_head_lane_mask(cm_c.shape, g, dm.sn), cm_c, 0.0).astype(BF16), bm_c)
                 for g in range(dm.sg)]
        eb_cols, w_cols, y_pairs = [], [], []
        for hh in range(dm.sh):
            bcol = jnp.sum(jnp.where(lane8 == hh, b8, 0.0), axis=-1, keepdims=True)
            wcol = jnp.sum(jnp.where(lane8 == hh, w8, 0.0), axis=-1, keepdims=True)
            eb_cols.append(jnp.exp(bcol))
            w_cols.append(wcol)
            seg = jnp.where(causal, bcol - b8t[hh:hh + 1, :], -jnp.inf)
            m_h = gmats[hh // hpg] * dtt[hh:hh + 1, :] * jnp.exp(seg)
            pr = hh // 2
            y_pairs.append(_dot(m_h.astype(BF16), xs_bf[:, pr * LANES:(pr + 1) * LANES]))
            if hh % 2:
                issue()
        lane_p = _iota((ls, LANES), 1)
        first = lane_p < dm.sp

        def pair_lanes(cols):
            return jnp.concatenate(
                [jnp.where(first, cols[2 * j], cols[2 * j + 1]) for j in range(dm.sh // 2)], axis=1)

        y_intra = jnp.concatenate(
            [jnp.where(first, y_pairs[2 * j], y_pairs[2 * j + 1]) for j in range(dm.sh // 2)], axis=1)
        eb_exp = pair_lanes([jnp.broadcast_to(e, (ls, LANES)) for e in eb_cols])
        w_exp = pair_lanes([jnp.broadcast_to(w, (ls, LANES)) for w in w_cols])
        st = sts[...]
        y_inter = _dot(cm_c.astype(BF16), st.astype(BF16)) * eb_exp
        upd = _dot_tn(bm_c, (xs_c * w_exp).astype(BF16))
        blk = (_iota(upd.shape, 0) >> _log2(dm.sn)) == (_iota(upd.shape, 1) >> _log2(dm.sp * hpg))
        dec_cols = []
        for hh in range(dm.sh):
            dec_cols.append(jnp.broadcast_to(
                jnp.sum(jnp.where(_iota(bl8.shape, 1) == hh, jnp.exp(bl8), 0.0), axis=-1, keepdims=True),
                (1, LANES)))
        lane_1 = _iota((1, LANES), 1) < dm.sp
        dec_row = jnp.concatenate(
            [jnp.where(lane_1, dec_cols[2 * j], dec_cols[2 * j + 1]) for j in range(dm.sh // 2)], axis=1)
        sts[...] = st * dec_row + jnp.where(blk, upd, 0.0)
        ossd[r, :] = y_intra + y_inter
        issue()
    y = ossd[...] + dexp_ref[...] * xs
    y = y * _silu(take("mz"))
    issue()
    y_ssd = _rms(y, snorm_ref[...]).astype(BF16)

    late = [(y_gla, wbg_ref, s) for s in range(0, dm.d, MXU_COLS)]
    late += [(y_ssd, wbs_ref, s) for s in range(0, dm.d, MXU_COLS)]
    yb_parts = []

    def issue_late(n=1):
        for _ in range(n):
            if late:
                yv, w_ref, s = late.pop(0)
                yb_parts.append(_dot(yv, w_ref[:, s:s + MXU_COLS]))

    cos = cos_ref[...]
    sin = sin_ref[...]
    rq_pre = take("rq")
    rk_pre = take("rk")
    issue(len(pending))
    rq = rq_pre * cos + _swap_halves(rq_pre, dm.rn) * sin
    issue_late()
    rk = (rk_pre * cos + _swap_halves(rk_pre, dm.rn) * sin) * (dm.rn ** -0.5)
    issue_late()
    rv = take("rv")
    q_dec, k_dec, st_dec, sc_dec = ret_tabs
    for c in range(tt // lr):
        r = slice(c * lr, (c + 1) * lr)
        oret[r, :] = _linear_attention_chunk(rq[r], rk[r], sc_dec, rq[r] * q_dec, rk[r] * k_dec, rv[r],
                                             st_dec, strt, dm.rh, dm.rn, dm.rp, issue_late)
        issue_late()
    issue_late(len(late))
    ncol = dm.d // MXU_COLS
    yb_gla = jnp.concatenate(yb_parts[:ncol], axis=1)
    yb_ssd = jnp.concatenate(yb_parts[ncol:], axis=1)
    ga, gb, gc = take("ga"), take("gb"), take("gc")
    m2 = jax.nn.sigmoid(ga) * yb_gla + jax.nn.sigmoid(gb) * yb_ssd
    y_ret = (_group_norm_heads(oret[...], dm.rh, dm.rp) * rnorm_ref[...] * _silu(take("rg"))).astype(BF16)

    halves = 2 if tt % (2 * SUBLANES * 2) == 0 else 1
    hr = tt // halves
    for i in range(halves):
        r = slice(i * hr, (i + 1) * hr)
        m = m2[r] + jax.nn.sigmoid(gc[r]) * _dot(y_ret[r], wbr_ref[...])
        y_ref[r, :] = x_ref[r, :] + _dot(m.astype(BF16), wout_ref[...])

    @pl.when(t == nt - 1)
    def _():
        sg = stg[...]
        acc = sg[0:dm.gp]
        for hh in range(1, dm.gh):
            acc = acc + sg[hh * dm.gp:(hh + 1) * dm.gp]
        sg_ref[...] = acc
        sr = strt[...]
        acc = sr[0:dm.rp]
        for hh in range(1, dm.rh):
            acc = acc + sr[hh * dm.rp:(hh + 1) * dm.rp]
        sr_ref[...] = acc
        sz = sts[...]
        acc = sz[0:dm.sn]
        for g in range(1, dm.sg):
            acc = acc + sz[g * dm.sn:(g + 1) * dm.sn]
        ss_ref[...] = acc
        cv_ref[...] = cbuf[SUBLANES - ck:SUBLANES, :]


def _mixer(x, cos, sin, sg0, ss0, cv0, sr0, wts, dm, *, tt, lg, ls, lr):
    bsz, seq, d = x.shape
    assert seq % tt == 0 and tt % lg == 0 and tt % ls == 0 and tt % lr == 0
    kern = functools.partial(_mixer_kernel, dm=dm, tt=tt, lg=lg, ls=ls, lr=lr)
    gq = dm.gh * dm.gn
    rq = dm.rh * dm.rn

    def per_b(shape):
        nd = len(shape)
        return pl.BlockSpec((None,) + shape, lambda b, t: (b,) + (0,) * nd)

    nb, nt = bsz, seq // tt

    def next_tile(b, t):
        wrap = t + 1 >= nt
        return jnp.where(wrap, jnp.minimum(b + 1, nb - 1), b), jnp.where(wrap, 0, t + 1), 0

    in_specs = [
        pl.BlockSpec((None, tt, d), lambda b, t: (b, t, 0)),
        pl.BlockSpec((None, tt, d), next_tile),
        pl.BlockSpec((tt, rq), lambda b, t: (t, 0)),
        pl.BlockSpec((tt, rq), lambda b, t: (t, 0)),
        per_b((dm.gp, gq)), per_b((dm.sn, dm.s_inner)), per_b((dm.conv_k - 1, dm.conv_dim)), per_b((dm.rp, rq)),
    ] + [_const_spec(w.shape) for w in wts]
    out_specs = [
        pl.BlockSpec((None, tt, d), lambda b, t: (b, t, 0)),
        per_b((dm.gp, gq)), per_b((dm.sn, dm.s_inner)), per_b((dm.conv_k - 1, dm.conv_dim)), per_b((dm.rp, rq)),
    ]
    out_shape = [
        jax.ShapeDtypeStruct((bsz, seq, d), F32),
        jax.ShapeDtypeStruct((bsz, dm.gp, gq), F32),
        jax.ShapeDtypeStruct((bsz, dm.sn, dm.s_inner), F32),
        jax.ShapeDtypeStruct((bsz, dm.conv_k - 1, dm.conv_dim), F32),
        jax.ShapeDtypeStruct((bsz, dm.rp, rq), F32),
    ]
    scratch = [
        pltpu.VMEM((dm.gh * dm.gp, gq), F32),
        pltpu.VMEM((dm.sg * dm.sn, dm.s_inner), F32),
        pltpu.VMEM((tt + SUBLANES, dm.conv_dim), F32),
        pltpu.VMEM((dm.rh * dm.rp, rq), F32),
        pltpu.VMEM((tt, dm.gh * dm.gp), F32),
        pltpu.VMEM((tt, dm.s_inner), F32),
        pltpu.VMEM((tt, dm.rh * dm.rp), F32),
        pltpu.VMEM((2, tt, d), BF16),
    ]
    return pl.pallas_call(
        kern,
        grid=(bsz, seq // tt),
        in_specs=in_specs,
        out_specs=out_specs,
        out_shape=out_shape,
        scratch_shapes=scratch,
        compiler_params=pltpu.CompilerParams(
            dimension_semantics=("arbitrary", "arbitrary"), vmem_limit_bytes=VMEM_LIMIT_BYTES),
        name="mixer",
    )(x, x, cos, sin, sg0, ss0, cv0, sr0, *wts)


def _pad_cols(w, n):
    return jnp.pad(w, ((0, 0), (0, n - w.shape[1])))


def _row(v, n=None):
    v = v.reshape(1, -1).astype(F32)
    return v if n is None else _pad_cols(v, n)


def _rope_tables(pos, dm):
    half = dm.rn // 2
    freqs = ROPE_BASE ** (-jnp.arange(half, dtype=F32) / half)
    ang = pos.astype(F32)[:, None] * freqs[None, :]
    cos = jnp.cos(ang)
    sin = jnp.sin(ang)
    cos_h = jnp.concatenate([cos, cos], axis=1)
    sin_h = jnp.concatenate([-sin, sin], axis=1)
    return jnp.tile(cos_h, (1, dm.rh)), jnp.tile(sin_h, (1, dm.rh))


def _state_to_kernel_t(s):
    b, h, n, p = s.shape
    return jnp.transpose(s, (0, 3, 1, 2)).reshape(b, p, h * n)


def _state_from_kernel_t(s, h):
    b, p, hn = s.shape
    return jnp.transpose(s.reshape(b, p, h, hn // h), (0, 2, 3, 1))


def _ssd_to_kernel(s):
    b, h, n, p = s.shape
    return jnp.transpose(s, (0, 2, 1, 3)).reshape(b, n, h * p)


def _ssd_from_kernel(s, h):
    b, n, hp = s.shape
    return jnp.transpose(s.reshape(b, n, h, hp // h), (0, 2, 1, 3))


def _tiles(seq):
    if seq % SCAN_CHUNK == 0:
        tt = 256 if seq % 256 == 0 else SCAN_CHUNK
        return dict(tt=tt, lg=SCAN_CHUNK, ls=min(tt, 128), lr=min(tt, 128))
    return dict(tt=seq, lg=seq, ls=seq, lr=seq)


def kernel(x_prompt, x_sample, state_gla, state_ssd, cache_conv, state_ret, norm_ffn1, ffn1_w_in, ffn1_w_out,
           norm_mix, w_in, gla_w_gate2, gla_b_gate, gla_norm, ssd_conv_w, ssd_conv_b, ssd_dt_bias, ssd_a_log,
           ssd_d, ssd_norm, ret_norm, w_branch_gla, w_branch_ssd, w_branch_ret, w_out, norm_ffn2, ffn2_w_in,
           ffn2_w_out, norm_final):
    depth = w_in.shape[0]
    d = x_prompt.shape[-1]
    d_ff = ffn1_w_out.shape[1]
    dm = _Dims(state_gla, state_ssd, cache_conv, state_ret, gla_w_gate2, ssd_conv_w, d)

    split_sizes = (dm.gh * dm.gn, dm.gh * dm.gn, dm.gh * dm.gp, dm.gh * dm.gp, dm.rank,
                   dm.s_inner, dm.conv_dim, dm.sh,
                   dm.rh * dm.rn, dm.rh * dm.rn, dm.rh * dm.rp, dm.rh * dm.rp, d, d, d)
    pts, acc = [], 0
    for s in split_sizes[:-1]:
        acc += s
        pts.append(acc)

    def mixer_weights(l):
        (s_gq, s_gk, s_gv, s_gr, s_glr, s_mz, s_mxbc, s_mdt,
         s_rq, s_rk, s_rv, s_rg, s_ga, s_gb, s_gc) = jnp.split(w_in[l], pts, axis=1)
        win = jnp.concatenate(
            [s_gq, s_gk, _pad_cols(s_glr, LANES), _pad_cols(s_mdt, LANES), s_gv, s_gr, s_mz, s_mxbc,
             s_rq, s_rk, s_rv, s_rg, s_ga, s_gb, s_gc], axis=1).astype(BF16)
        assert win.shape[1] == dm.in_cols
        wg2 = jnp.pad(gla_w_gate2[l], ((0, LANES - dm.rank), (0, 0))).astype(BF16)
        return [
            _row(norm_mix[l]), win, wg2, _row(gla_b_gate[l]), _row(gla_norm[l]),
            ssd_conv_w[l].astype(F32), _row(ssd_conv_b[l]), _row(ssd_dt_bias[l], LANES), _row(ssd_a_log[l], LANES),
            _row(jnp.repeat(ssd_d[l], dm.sp)), _row(ssd_norm[l]), _row(ret_norm[l]),
            w_branch_gla[l].astype(BF16), w_branch_ssd[l].astype(BF16), w_branch_ret[l].astype(BF16),
            w_out[l].astype(BF16),
        ]

    def ffn_weights(nrm, w_i, w_o, l):
        return _row(nrm[l]), w_i[l][:, :d_ff].astype(BF16), w_i[l][:, d_ff:].astype(BF16), w_o[l].astype(BF16)

    g_final = _row(norm_final)

    def run_group(x, pos, states):
        bsz, seq, _ = x.shape
        m = bsz * seq
        tm = 512 if m % 512 == 0 else m
        cos, sin = _rope_tables(pos, dm)
        tl = _tiles(seq)
        new_states = []
        for l in range(depth):
            sg0, ss0, cv0, sr0 = states(l)
            g1, wg1, wu1, wo1 = ffn_weights(norm_ffn1, ffn1_w_in, ffn1_w_out, l)
            x = _ffn(x.reshape(m, d), g1, wg1, wu1, wo1, g_final, tm=tm, fc=256, final_norm=False).reshape(bsz, seq, d)
            x, sg, ss, cv, sr = _mixer(x, cos, sin, _state_to_kernel_t(sg0), _ssd_to_kernel(ss0), cv0,
                                       _state_to_kernel_t(sr0), mixer_weights(l), dm, **tl)
            g2, wg2_, wu2, wo2 = ffn_weights(norm_ffn2, ffn2_w_in, ffn2_w_out, l)
            x = _ffn(x.reshape(m, d), g2, wg2_, wu2, wo2, g_final, tm=tm, fc=256,
                     final_norm=(l == depth - 1)).reshape(bsz, seq, d)
            new_states.append((_state_from_kernel_t(sg, dm.gh), _ssd_from_kernel(ss, dm.sh), cv,
                               _state_from_kernel_t(sr, dm.rh)))
        return x, new_states

    bp, tp = x_prompt.shape[:2]
    ts = x_sample.shape[1]
    dt_ = x_prompt.dtype

    def zero_states(_):
        return (jnp.zeros((bp, dm.gh, dm.gn, dm.gp), dt_), jnp.zeros((bp, dm.sh, dm.sn, dm.sp), dt_),
                jnp.zeros((bp, dm.conv_k - 1, dm.conv_dim), dt_), jnp.zeros((bp, dm.rh, dm.rn, dm.rp), dt_))

    def carried_states(l):
        return state_gla[l], state_ssd[l], cache_conv[l], state_ret[l]

    y_p, st_p = run_group(x_prompt, jnp.arange(tp, dtype=jnp.int32), zero_states)
    y_s, st_s = run_group(x_sample, PAST_LEN + jnp.arange(ts, dtype=jnp.int32), carried_states)

    def stack(sts, i):
        return jnp.stack([s[i] for s in sts])

    return (y_p, y_s, stack(st_p, 0), stack(st_p, 1), stack(st_p, 2), stack(st_p, 3),
            stack(st_s, 0), stack(st_s, 1), stack(st_s, 2), stack(st_s, 3))
```

```python
import functools
import math

import jax
import jax.numpy as jnp
from jax import lax
from jax.experimental import pallas as pl
from jax.experimental.pallas import tpu as pltpu

F32 = jnp.float32
BF16 = jnp.bfloat16

EPS = 1e-6
PAST_LEN = 1024
GLA_TAU = 16.0
ROPE_BASE = 10000.0
SCAN_CHUNK = 64

LANES = 128
SUBLANES = 8
MXU_COLS = 256
VMEM_LIMIT_BYTES = 56 * 1024 * 1024


def _dot(a, b):
    return jnp.dot(a, b, preferred_element_type=F32)


def _dot_nt(a, b):
    return lax.dot_general(a, b, (((1,), (1,)), ((), ())), preferred_element_type=F32)


def _dot_tn(a, b):
    return lax.dot_general(a, b, (((0,), (0,)), ((), ())), preferred_element_type=F32)


def _split3(x):
    hi = x.astype(BF16)
    r1 = x - hi.astype(F32)
    mid = r1.astype(BF16)
    lo = (r1 - mid.astype(F32)).astype(BF16)
    return hi, mid, lo


def _dot_exact01(m01, x):
    hi, mid, lo = _split3(x)
    return _dot(m01, hi) + _dot(m01, mid) + _dot(m01, lo)


def _rms(x, g):
    return x * lax.rsqrt(jnp.mean(x * x, axis=-1, keepdims=True) + EPS) * g


def _silu(x):
    return x * jax.nn.sigmoid(x)


def _softplus(x):
    return jnp.maximum(x, 0.0) + jnp.log(1.0 + jnp.exp(-jnp.abs(x)))


def _log_sigmoid(x):
    return jnp.minimum(x, 0.0) - jnp.log(1.0 + jnp.exp(-jnp.abs(x)))


def _iota(shape, axis):
    return lax.broadcasted_iota(jnp.int32, shape, axis)


def _log2(n):
    k = int(math.log2(n))
    assert (1 << k) == n, f"{n} must be a power of two"
    return k


def _ffn_kernel(x_ref, g_ref, wg_ref, wu_ref, wo_ref, gf_ref, o_ref, h_ref, a_ref, *, fc, final_norm):
    x = x_ref[...]
    h_ref[...] = _rms(x, g_ref[...]).astype(BF16)
    d_ff = wg_ref.shape[1]
    for c in range(d_ff // fc):
        sl = slice(c * fc, (c + 1) * fc)
        gate = _dot(h_ref[...], wg_ref[:, sl])
        up = _dot(h_ref[...], wu_ref[:, sl])
        a_ref[:, sl] = (_silu(gate) * up).astype(BF16)
    y = x + 0.5 * _dot(a_ref[...], wo_ref[...])
    if final_norm:
        y = _rms(y, gf_ref[...])
    o_ref[...] = y


def _const_spec(shape):
    nd = len(shape)
    return pl.BlockSpec(shape, lambda *_: (0,) * nd, pipeline_mode=pl.Buffered(1))


def _ffn(x2d, g, w_gate, w_up, w_out, g_final, *, tm, fc, final_norm):
    m, d = x2d.shape
    d_ff = w_gate.shape[1]
    assert m % tm == 0 and d_ff % fc == 0
    kern = functools.partial(_ffn_kernel, fc=fc, final_norm=final_norm)
    return pl.pallas_call(
        kern,
        grid=(m // tm,),
        in_specs=[
            pl.BlockSpec((tm, d), lambda i: (i, 0)),
            _const_spec((1, d)),
            _const_spec((d, d_ff)),
            _const_spec((d, d_ff)),
            _const_spec((d_ff, d)),
            _const_spec((1, d)),
        ],
        out_specs=pl.BlockSpec((tm, d), lambda i: (i, 0)),
        out_shape=jax.ShapeDtypeStruct((m, d), F32),
        scratch_shapes=[pltpu.VMEM((tm, d), BF16), pltpu.VMEM((tm, d_ff), BF16)],
        compiler_params=pltpu.CompilerParams(
            dimension_semantics=("arbitrary",), vmem_limit_bytes=VMEM_LIMIT_BYTES),
        name="ffn",
    )(x2d, g, w_gate, w_up, w_out, g_final)


class _Dims:
    def __init__(self, state_gla, state_ssd, cache_conv, state_ret, gla_w_gate2, ssd_conv_w, d_model):
        _, _, self.gh, self.gn, self.gp = state_gla.shape
        _, _, self.sh, self.sn, self.sp = state_ssd.shape
        _, _, self.rh, self.rn, self.rp = state_ret.shape
        self.conv_k = ssd_conv_w.shape[1]
        self.conv_dim = cache_conv.shape[-1]
        self.rank = gla_w_gate2.shape[1]
        self.d = d_model
        self.s_inner = self.sh * self.sp
        self.sg = (self.conv_dim - self.s_inner) // (2 * self.sn)
        assert self.gh * self.gn == 2 * LANES and self.gp == LANES
        assert self.rh * self.rn == 2 * LANES and self.rp == LANES
        assert self.sp * 2 == LANES and self.sg * self.sn == LANES and self.sh <= LANES
        assert self.rank <= LANES and self.conv_k - 1 <= SUBLANES
        w = [self.gh * self.gn, self.gh * self.gn, 2 * LANES, self.gh * self.gp, self.gh * self.gp,
             self.s_inner, self.conv_dim,
             self.rh * self.rn, self.rh * self.rn, self.rh * self.rp, self.rh * self.rp,
             d_model, d_model, d_model]
        self.seg_names = ["gq", "gk", "lrdt", "gv", "gr", "mz", "mxbc", "rq", "rk", "rv", "rg", "ga", "gb", "gc"]
        self.seg = {}
        off = 0
        for n, wd in zip(self.seg_names, w):
            assert wd % MXU_COLS == 0
            self.seg[n] = (off, off + wd)
            off += wd
        self.in_cols = off


def _head_lane_mask(shape, h, width):
    lane = _iota(shape, 1)
    return (lane >> _log2(width)) == h


def _stack_heads(x, nh, width):
    parts = [jnp.where(_head_lane_mask(x.shape, h, width), x, 0.0).astype(BF16) for h in range(nh)]
    return jnp.concatenate(parts, axis=0)


def _linear_attention_chunk(q_sc, k_sc, sc_mul, q_in, k_up, v, dec_row, dec_col, st_ref, nh, n, p, issue):
    L = q_sc.shape[0]
    k_big = _stack_heads(k_sc, nh, n)
    sc = _dot_nt(q_sc.astype(BF16), k_big)
    lane = _iota(sc.shape, 1)
    row = _iota(sc.shape, 0)
    causal = (lane & (L - 1)) <= row
    sc = jnp.where(causal, sc if sc_mul is None else sc * sc_mul, 0.0)
    v_big = _stack_heads(v, nh, p)
    issue()
    st = st_ref[...]
    rowhead = _iota(st.shape, 0) >> _log2(n)
    st_bd = jnp.concatenate([jnp.where(rowhead == h, st, 0.0).astype(BF16) for h in range(nh)], axis=1)
    o = _dot(sc.astype(BF16), v_big) + _dot(q_in.astype(BF16), st_bd)
    rows = [k_up]
    if dec_col is None:
        rows.append(jnp.broadcast_to(dec_row, (SUBLANES, nh * n)))
    pad = LANES - L - (len(rows) - 1) * SUBLANES
    assert pad >= 0
    if pad:
        rows.append(jnp.zeros((pad, nh * n), F32))
    kt = jnp.concatenate(rows, axis=0).T if len(rows) > 1 else k_up.T
    if dec_col is None:
        dec_col = jnp.sum(jnp.where(_iota(kt.shape, 1) == L, kt, 0.0), axis=-1, keepdims=True)
    kt = kt.astype(BF16)
    upd = []
    for h in range(nh):
        v_h = v[:, h * p:(h + 1) * p].astype(BF16)
        if L < LANES:
            v_h = jnp.concatenate([v_h, jnp.zeros((LANES - L, p), BF16)], axis=0)
        upd.append(_dot(kt[h * n:(h + 1) * n, :], v_h))
    st_ref[...] = st * dec_col + jnp.concatenate(upd, axis=0)
    return o


def _group_norm_heads(o, nh, p):
    outs = []
    for h in range(nh):
        oh = o[:, h * p:(h + 1) * p]
        mu = jnp.mean(oh, axis=-1, keepdims=True)
        dlt = oh - mu
        var = jnp.mean(dlt * dlt, axis=-1, keepdims=True)
        outs.append(dlt * lax.rsqrt(var + EPS))
    return jnp.concatenate(outs, axis=1)


def _block_tril(tt, L):
    row = _iota((tt, tt), 0)
    col = _iota((tt, tt), 1)
    k = _log2(L)
    return jnp.where((col <= row) & ((row >> k) == (col >> k)), 1.0, 0.0).astype(BF16)


def _ret_log_gamma(head, nh):
    out = jnp.full(head.shape, math.log1p(-2.0 ** (-5.0 - (nh - 1))), F32)
    for h in range(nh - 1):
        out = jnp.where(head == h, math.log1p(-2.0 ** (-5.0 - h)), out)
    return out


def _retention_tables(dm, L):
    nlane = dm.rh * dm.rn
    lgam_q = _ret_log_gamma(_iota((L, nlane), 1) >> _log2(dm.rn), dm.rh)
    pos_q = _iota((L, nlane), 0).astype(F32)
    q_dec = jnp.exp((pos_q + 1.0) * lgam_q)
    k_dec = jnp.exp((L - 1.0 - pos_q) * lgam_q)
    lgam_r = _ret_log_gamma(_iota((nlane, dm.rp), 0) >> _log2(dm.rn), dm.rh)
    st_dec = jnp.exp(L * lgam_r)
    lgam_s = _ret_log_gamma(_iota((L, dm.rh * L), 1) >> _log2(L), dm.rh)
    dpos = (_iota((L, dm.rh * L), 0) - (_iota((L, dm.rh * L), 1) & (L - 1))).astype(F32)
    sc_dec = jnp.exp(jnp.maximum(dpos, 0.0) * lgam_s)
    return q_dec, k_dec, st_dec, sc_dec


def _transpose_rows(x):
    L = x.shape[0]
    if L % LANES:
        pad = LANES - L % LANES
        x = jnp.concatenate([x, jnp.zeros((pad, x.shape[1]), x.dtype)], axis=0)
    return x.T[:, :L]


def _swap_halves(x, n):
    width = x.shape[1]
    half = n // 2
    first = (_iota(x.shape, 1) & (n - 1)) < half
    return jnp.where(first, pltpu.roll(x, width - half, 1), pltpu.roll(x, half, 1))


def _mixer_kernel(x_ref, xn_ref, cos_ref, sin_ref, sg0_ref, ss0_ref, cv0_ref, sr0_ref,
                  nrm_ref, win_ref, wg2_ref, bg_ref, gnorm_ref, cw_ref, cb_ref, dtb_ref, alog_ref,
                  dexp_ref, snorm_ref, rnorm_ref, wbg_ref, wbs_ref, wbr_ref, wout_ref,
                  y_ref, sg_ref, ss_ref, cv_ref, sr_ref,
                  stg, sts, cbuf, strt, ogla, ossd, oret, hbuf,
                  *, dm, tt, lg, ls, lr):
    t = pl.program_id(1)
    nt = pl.num_programs(1)
    ck = dm.conv_k - 1

    @pl.when(t == 0)
    def _():
        stg[...] = sg0_ref[...]
        strt[...] = sr0_ref[...]
        z0 = ss0_ref[...]
        hpg = dm.sh // dm.sg
        sts[...] = jnp.concatenate(
            [jnp.where(_head_lane_mask(z0.shape, g, dm.sp * hpg), z0, 0.0) for g in range(dm.sg)], axis=0)
        cbuf[SUBLANES - ck:SUBLANES, :] = cv0_ref[...]

    step = pl.program_id(0) * nt + t
    slot = lax.rem(step, 2)

    @pl.when(step == 0)
    def _():
        hbuf[0] = _rms(x_ref[...], nrm_ref[...]).astype(BF16)

    pending = []
    for name in ["gq", "gk", "lrdt", "gv", "gr", "mxbc", "mz", "rq", "rk", "rv", "rg", "ga", "gb", "gc"]:
        a, b = dm.seg[name]
        pending += [(name, s) for s in range(a, b, MXU_COLS)]
    parts = {name: [] for name in dm.seg_names}

    def issue(n=1):
        for _ in range(n):
            if pending:
                name, s = pending.pop(0)
                parts[name].append(_dot(hbuf[slot], win_ref[:, s:s + MXU_COLS]))

    def take(name):
        a, b = dm.seg[name]
        while len(parts[name]) * MXU_COLS < b - a:
            issue()
        return parts[name][0] if len(parts[name]) == 1 else jnp.concatenate(parts[name], axis=1)

    q = take("gq") * (dm.gn ** -0.5)
    k = take("gk")
    lrdt = take("lrdt")
    hbuf[1 - slot] = _rms(xn_ref[...], nrm_ref[...]).astype(BF16)
    ret_tabs = _retention_tables(dm, lr)
    issue(2)
    gate = _dot(lrdt[:, :LANES].astype(BF16), wg2_ref[...]) + bg_ref[...]
    issue(2)
    log_a = _log_sigmoid(gate) * (1.0 / GLA_TAU)
    issue()
    b_all = _dot_exact01(_block_tril(tt, lg), log_a)
    qd_all = q * jnp.exp(b_all)
    issue()
    kd_all = k * jnp.exp(-b_all)
    v = take("gv")
    for c in range(tt // lg):
        r = slice(c * lg, (c + 1) * lg)
        b = b_all[r]
        bl = b[lg - 1:lg, :]
        ks = k[r] * jnp.exp(bl - b)
        ogla[r, :] = _linear_attention_chunk(qd_all[r], kd_all[r], None, qd_all[r], ks, v[r],
                                             jnp.exp(bl), None, stg, dm.gh, dm.gn, dm.gp,
                                             issue if c else (lambda: None))
    y_gla = (_group_norm_heads(ogla[...], dm.gh, dm.gp) * gnorm_ref[...] * _silu(take("gr"))).astype(BF16)
    issue(2)

    xbc_pre = take("mxbc")
    cbuf[SUBLANES:SUBLANES + tt, :] = xbc_pre
    conv = cb_ref[...] + cw_ref[ck:ck + 1, :] * xbc_pre
    for j in range(ck):
        conv = conv + cw_ref[j:j + 1, :] * cbuf[SUBLANES - ck + j:SUBLANES - ck + j + tt, :]
    new_carry = cbuf[SUBLANES + tt - ck:SUBLANES + tt, :]
    cbuf[SUBLANES - ck:SUBLANES, :] = new_carry
    issue(2)
    xbc = _silu(conv)
    issue(2)
    xs = xbc[:, :dm.s_inner]
    bm = xbc[:, dm.s_inner:dm.s_inner + LANES]
    cm = xbc[:, dm.s_inner + LANES:dm.s_inner + 2 * LANES]
    dt8 = _softplus(lrdt[:, LANES:] + dtb_ref[...])
    a8 = dt8 * (-jnp.exp(alog_ref[...]))
    b8_all = _dot_exact01(_block_tril(tt, ls), a8)
    hpg = dm.sh // dm.sg
    for c in range(tt // ls):
        r = slice(c * ls, (c + 1) * ls)
        b8 = b8_all[r]
        dtc = dt8[r]
        bl8 = b8[ls - 1:ls, :]
        w8 = dtc * jnp.exp(bl8 - b8)
        b8t = _transpose_rows(b8)
        dtt = _transpose_rows(dtc)
        xs_c = xs[r]
        xs_bf = xs_c.astype(BF16)
        cm_c = cm[r]
        bm_c = bm[r].astype(BF16)
        lane8 = _iota(b8.shape, 1)
        row = _iota((ls, ls), 0)
        col = _iota((ls, ls), 1)
        causal = col <= row
        gmats = [_dot_nt(jnp.where(_head_lane_mask(cm_c.shape, g, dm.sn), cm_c, 0.0).astype(BF16), bm_c)
                 for g in range(dm.sg)]
        eb_cols, w_cols, y_pairs = [], [], []
        for hh in range(dm.sh):
            bcol = jnp.sum(jnp.where(lane8 == hh, b8, 0.0), axis=-1, keepdims=True)
            wcol = jnp.sum(jnp.where(lane8 == hh, w8, 0.0), axis=-1, keepdims=True)
            eb_cols.append(jnp.exp(bcol))
            w_cols.append(wcol)
            seg = jnp.where(causal, bcol - b8t[hh:hh + 1, :], -jnp.inf)
            m_h = gmats[hh // hpg] * dtt[hh:hh + 1, :] * jnp.exp(seg)
            pr = hh // 2
            y_pairs.append(_dot(m_h.astype(BF16), xs_bf[:, pr * LANES:(pr + 1) * LANES]))
            if hh % 2:
                issue()
        lane_p = _iota((ls, LANES), 1)
        first = lane_p < dm.sp

        def pair_lanes(cols):
            return jnp.concatenate(
                [jnp.where(first, cols[2 * j], cols[2 * j + 1]) for j in range(dm.sh // 2)], axis=1)

        y_intra = jnp.concatenate(
            [jnp.where(first, y_pairs[2 * j], y_pairs[2 * j + 1]) for j in range(dm.sh // 2)], axis=1)
        eb_exp = pair_lanes([jnp.broadcast_to(e, (ls, LANES)) for e in eb_cols])
        w_exp = pair_lanes([jnp.broadcast_to(w, (ls, LANES)) for w in w_cols])
        st = sts[...]
        y_inter = _dot(cm_c.astype(BF16), st.astype(BF16)) * eb_exp
        upd = _dot_tn(bm_c, (xs_c * w_exp).astype(BF16))
        blk = (_iota(upd.shape, 0) >> _log2(dm.sn)) == (_iota(upd.shape, 1) >> _log2(dm.sp * hpg))
        dec_cols = []
        for hh in range(dm.sh):
            dec_cols.append(jnp.broadcast_to(
                jnp.sum(jnp.where(_iota(bl8.shape, 1) == hh, jnp.exp(bl8), 0.0), axis=-1, keepdims=True),
                (1, LANES)))
        lane_1 = _iota((1, LANES), 1) < dm.sp
        dec_row = jnp.concatenate(
            [jnp.where(lane_1, dec_cols[2 * j], dec_cols[2 * j + 1]) for j in range(dm.sh // 2)], axis=1)
        sts[...] = st * dec_row + jnp.where(blk, upd, 0.0)
        ossd[r, :] = y_intra + y_inter
        issue()
    y = ossd[...] + dexp_ref[...] * xs
    y = y * _silu(take("mz"))
    issue()
    y_ssd = _rms(y, snorm_ref[...]).astype(BF16)

    late = [(y_gla, wbg_ref, s) for s in range(0, dm.d, MXU_COLS)]
    late += [(y_ssd, wbs_ref, s) for s in range(0, dm.d, MXU_COLS)]
    yb_parts = []

    def issue_late(n=1):
        for _ in range(n):
            if late:
                yv, w_ref, s = late.pop(0)
                yb_parts.append(_dot(yv, w_ref[:, s:s + MXU_COLS]))

    cos = cos_ref[...]
    sin = sin_ref[...]
    rq_pre = take("rq")
    rk_pre = take("rk")
    issue(len(pending))
    rq = rq_pre * cos + _swap_halves(rq_pre, dm.rn) * sin
    issue_late()
    rk = (rk_pre * cos + _swap_halves(rk_pre, dm.rn) * sin) * (dm.rn ** -0.5)
    issue_late()
    rv = take("rv")
    q_dec, k_dec, st_dec, sc_dec = ret_tabs
    for c in range(tt // lr):
        r = slice(c * lr, (c + 1) * lr)
        oret[r, :] = _linear_attention_chunk(rq[r], rk[r], sc_dec, rq[r] * q_dec, rk[r] * k_dec, rv[r],
                                             None, st_dec, strt, dm.rh, dm.rn, dm.rp, issue_late)
        issue_late()
    issue_late(len(late))
    ncol = dm.d // MXU_COLS
    yb_gla = jnp.concatenate(yb_parts[:ncol], axis=1)
    yb_ssd = jnp.concatenate(yb_parts[ncol:], axis=1)
    ga, gb, gc = take("ga"), take("gb"), take("gc")
    m2 = jax.nn.sigmoid(ga) * yb_gla + jax.nn.sigmoid(gb) * yb_ssd
    y_ret = (_group_norm_heads(oret[...], dm.rh, dm.rp) * rnorm_ref[...] * _silu(take("rg"))).astype(BF16)

    halves = 2 if tt % (2 * SUBLANES * 2) == 0 else 1
    hr = tt // halves
    for i in range(halves):
        r = slice(i * hr, (i + 1) * hr)
        m = m2[r] + jax.nn.sigmoid(gc[r]) * _dot(y_ret[r], wbr_ref[...])
        y_ref[r, :] = x_ref[r, :] + _dot(m.astype(BF16), wout_ref[...])

    @pl.when(t == nt - 1)
    def _():
        sg_ref[...] = stg[...]
        sr_ref[...] = strt[...]
        sz = sts[...]
        acc = sz[0:dm.sn]
        for g in range(1, dm.sg):
            acc = acc + sz[g * dm.sn:(g + 1) * dm.sn]
        ss_ref[...] = acc
        cv_ref[...] = cbuf[SUBLANES - ck:SUBLANES, :]


def _mixer(x, cos, sin, sg0, ss0, cv0, sr0, wts, dm, *, tt, lg, ls, lr):
    bsz, seq, d = x.shape
    assert seq % tt == 0 and tt % lg == 0 and tt % ls == 0 and tt % lr == 0
    kern = functools.partial(_mixer_kernel, dm=dm, tt=tt, lg=lg, ls=ls, lr=lr)
    gq = dm.gh * dm.gn
    rq = dm.rh * dm.rn

    def per_b(shape):
        nd = len(shape)
        return pl.BlockSpec((None,) + shape, lambda b, t: (b,) + (0,) * nd)

    nb, nt = bsz, seq // tt

    def next_tile(b, t):
        wrap = t + 1 >= nt
        return jnp.where(wrap, jnp.minimum(b + 1, nb - 1), b), jnp.where(wrap, 0, t + 1), 0

    in_specs = [
        pl.BlockSpec((None, tt, d), lambda b, t: (b, t, 0)),
        pl.BlockSpec((None, tt, d), next_tile),
        pl.BlockSpec((tt, rq), lambda b, t: (t, 0)),
        pl.BlockSpec((tt, rq), lambda b, t: (t, 0)),
        per_b((gq, dm.gp)), per_b((dm.sn, dm.s_inner)), per_b((dm.conv_k - 1, dm.conv_dim)), per_b((rq, dm.rp)),
    ] + [_const_spec(w.shape) for w in wts]
    out_specs = [
        pl.BlockSpec((None, tt, d), lambda b, t: (b, t, 0)),
        per_b((gq, dm.gp)), per_b((dm.sn, dm.s_inner)), per_b((dm.conv_k - 1, dm.conv_dim)), per_b((rq, dm.rp)),
    ]
    out_shape = [
        jax.ShapeDtypeStruct((bsz, seq, d), F32),
        jax.ShapeDtypeStruct((bsz, gq, dm.gp), F32),
        jax.ShapeDtypeStruct((bsz, dm.sn, dm.s_inner), F32),
        jax.ShapeDtypeStruct((bsz, dm.conv_k - 1, dm.conv_dim), F32),
        jax.ShapeDtypeStruct((bsz, rq, dm.rp), F32),
    ]
    scratch = [
        pltpu.VMEM((gq, dm.gp), F32),
        pltpu.VMEM((dm.sg * dm.sn, dm.s_inner), F32),
        pltpu.VMEM((tt + SUBLANES, dm.conv_dim), F32),
        pltpu.VMEM((rq, dm.rp), F32),
        pltpu.VMEM((tt, dm.gh * dm.gp), F32),
        pltpu.VMEM((tt, dm.s_inner), F32),
        pltpu.VMEM((tt, dm.rh * dm.rp), F32),
        pltpu.VMEM((2, tt, d), BF16),
    ]
    return pl.pallas_call(
        kern,
        grid=(bsz, seq // tt),
        in_specs=in_specs,
        out_specs=out_specs,
        out_shape=out_shape,
        scratch_shapes=scratch,
        compiler_params=pltpu.CompilerParams(
            dimension_semantics=("arbitrary", "arbitrary"), vmem_limit_bytes=VMEM_LIMIT_BYTES),
        name="mixer",
    )(x, x, cos, sin, sg0, ss0, cv0, sr0, *wts)


def _pad_cols(w, n):
    return jnp.pad(w, ((0, 0), (0, n - w.shape[1])))


def _row(v, n=None):
    v = v.reshape(1, -1).astype(F32)
    return v if n is None else _pad_cols(v, n)


def _rope_tables(pos, dm):
    half = dm.rn // 2
    freqs = ROPE_BASE ** (-jnp.arange(half, dtype=F32) / half)
    ang = pos.astype(F32)[:, None] * freqs[None, :]
    cos = jnp.cos(ang)
    sin = jnp.sin(ang)
    cos_h = jnp.concatenate([cos, cos], axis=1)
    sin_h = jnp.concatenate([-sin, sin], axis=1)
    return jnp.tile(cos_h, (1, dm.rh)), jnp.tile(sin_h, (1, dm.rh))


def _state_to_kernel_t(s):
    b, h, n, p = s.shape
    return s.reshape(b, h * n, p)


def _state_from_kernel_t(s, h):
    b, hn, p = s.shape
    return s.reshape(b, h, hn // h, p)


def _ssd_to_kernel(s):
    b, h, n, p = s.shape
    return jnp.transpose(s, (0, 2, 1, 3)).reshape(b, n, h * p)


def _ssd_from_kernel(s, h):
    b, n, hp = s.shape
    return jnp.transpose(s.reshape(b, n, h, hp // h), (0, 2, 1, 3))


def _tiles(seq):
    if seq % SCAN_CHUNK == 0:
        tt = 256 if seq % 256 == 0 else SCAN_CHUNK
        return dict(tt=tt, lg=SCAN_CHUNK, ls=min(tt, 128), lr=min(tt, 128))
    return dict(tt=seq, lg=seq, ls=seq, lr=seq)


def kernel(x_prompt, x_sample, state_gla, state_ssd, cache_conv, state_ret, norm_ffn1, ffn1_w_in, ffn1_w_out,
           norm_mix, w_in, gla_w_gate2, gla_b_gate, gla_norm, ssd_conv_w, ssd_conv_b, ssd_dt_bias, ssd_a_log,
           ssd_d, ssd_norm, ret_norm, w_branch_gla, w_branch_ssd, w_branch_ret, w_out, norm_ffn2, ffn2_w_in,
           ffn2_w_out, norm_final):
    depth = w_in.shape[0]
    d = x_prompt.shape[-1]
    d_ff = ffn1_w_out.shape[1]
    dm = _Dims(state_gla, state_ssd, cache_conv, state_ret, gla_w_gate2, ssd_conv_w, d)

    split_sizes = (dm.gh * dm.gn, dm.gh * dm.gn, dm.gh * dm.gp, dm.gh * dm.gp, dm.rank,
                   dm.s_inner, dm.conv_dim, dm.sh,
                   dm.rh * dm.rn, dm.rh * dm.rn, dm.rh * dm.rp, dm.rh * dm.rp, d, d, d)
    pts, acc = [], 0
    for s in split_sizes[:-1]:
        acc += s
        pts.append(acc)

    def mixer_weights(l):
        (s_gq, s_gk, s_gv, s_gr, s_glr, s_mz, s_mxbc, s_mdt,
         s_rq, s_rk, s_rv, s_rg, s_ga, s_gb, s_gc) = jnp.split(w_in[l], pts, axis=1)
        win = jnp.concatenate(
            [s_gq, s_gk, _pad_cols(s_glr, LANES), _pad_cols(s_mdt, LANES), s_gv, s_gr, s_mz, s_mxbc,
             s_rq, s_rk, s_rv, s_rg, s_ga, s_gb, s_gc], axis=1).astype(BF16)
        assert win.shape[1] == dm.in_cols
        wg2 = jnp.pad(gla_w_gate2[l], ((0, LANES - dm.rank), (0, 0))).astype(BF16)
        return [
            _row(norm_mix[l]), win, wg2, _row(gla_b_gate[l]), _row(gla_norm[l]),
            ssd_conv_w[l].astype(F32), _row(ssd_conv_b[l]), _row(ssd_dt_bias[l], LANES), _row(ssd_a_log[l], LANES),
            _row(jnp.repeat(ssd_d[l], dm.sp)), _row(ssd_norm[l]), _row(ret_norm[l]),
            w_branch_gla[l].astype(BF16), w_branch_ssd[l].astype(BF16), w_branch_ret[l].astype(BF16),
            w_out[l].astype(BF16),
        ]

    def ffn_weights(nrm, w_i, w_o, l):
        return _row(nrm[l]), w_i[l][:, :d_ff].astype(BF16), w_i[l][:, d_ff:].astype(BF16), w_o[l].astype(BF16)

    g_final = _row(norm_final)

    def run_group(x, pos, states):
        bsz, seq, _ = x.shape
        m = bsz * seq
        tm = 1024 if m % 1024 == 0 else m
        cos, sin = _rope_tables(pos, dm)
        tl = _tiles(seq)
        new_states = []
        for l in range(depth):
            sg0, ss0, cv0, sr0 = states(l)
            g1, wg1, wu1, wo1 = ffn_weights(norm_ffn1, ffn1_w_in, ffn1_w_out, l)
            x = _ffn(x.reshape(m, d), g1, wg1, wu1, wo1, g_final, tm=tm, fc=256, final_norm=False).reshape(bsz, seq, d)
            x, sg, ss, cv, sr = _mixer(x, cos, sin, _state_to_kernel_t(sg0), _ssd_to_kernel(ss0), cv0,
                                       _state_to_kernel_t(sr0), mixer_weights(l), dm, **tl)
            g2, wg2_, wu2, wo2 = ffn_weights(norm_ffn2, ffn2_w_in, ffn2_w_out, l)
            x = _ffn(x.reshape(m, d), g2, wg2_, wu2, wo2, g_final, tm=tm, fc=256,
                     final_norm=(l == depth - 1)).reshape(bsz, seq, d)
            new_states.append((_state_from_kernel_t(sg, dm.gh), _ssd_from_kernel(ss, dm.sh), cv,
                               _state_from_kernel_t(sr, dm.rh)))
        return x, new_states

    bp, tp = x_prompt.shape[:2]
    ts = x_sample.shape[1]
    dt_ = x_prompt.dtype

    def zero_states(_):
        return (jnp.zeros((bp, dm.gh, dm.gn, dm.gp), dt_), jnp.zeros((bp, dm.sh, dm.sn, dm.sp), dt_),
                jnp.zeros((bp, dm.conv_k - 1, dm.conv_dim), dt_), jnp.zeros((bp, dm.rh, dm.rn, dm.rp), dt_))

    def carried_states(l):
        return state_gla[l], state_ssd[l], cache_conv[l], state_ret[l]

    y_p, st_p = run_group(x_prompt, jnp.arange(tp, dtype=jnp.int32), zero_states)
    y_s, st_s = run_group(x_sample, PAST_LEN + jnp.arange(ts, dtype=jnp.int32), carried_states)

    def stack(sts, i):
        return jnp.stack([s[i] for s in sts])

    return (y_p, y_s, stack(st_p, 0), stack(st_p, 1), stack(st_p, 2), stack(st_p, 3),
            stack(st_s, 0), stack(st_s, 1), stack(st_s, 2), stack(st_s, 3))
```

```python
import functools
import math

import jax
import jax.numpy as jnp
from jax import lax
from jax.experimental import pallas as pl
from jax.experimental.pallas import tpu as pltpu

F32 = jnp.float32
BF16 = jnp.bfloat16

EPS = 1e-6
PAST_LEN = 1024
GLA_TAU = 16.0
ROPE_BASE = 10000.0
SCAN_CHUNK = 64

LANES = 128
SUBLANES = 8
MXU_COLS = 256
VMEM_LIMIT_BYTES = 56 * 1024 * 1024


def _dot(a, b):
    return jnp.dot(a, b, preferred_element_type=F32)


def _dot_nt(a, b):
    return lax.dot_general(a, b, (((1,), (1,)), ((), ())), preferred_element_type=F32)


def _dot_tn(a, b):
    return lax.dot_general(a, b, (((0,), (0,)), ((), ())), preferred_element_type=F32)


def _split3(x):
    hi = x.astype(BF16)
    r1 = x - hi.astype(F32)
    mid = r1.astype(BF16)
    lo = (r1 - mid.astype(F32)).astype(BF16)
    return hi, mid, lo


def _dot_exact01(m01, x):
    hi, mid, lo = _split3(x)
    return _dot(m01, hi) + _dot(m01, mid) + _dot(m01, lo)


def _rms(x, g):
    return x * lax.rsqrt(jnp.mean(x * x, axis=-1, keepdims=True) + EPS) * g


def _silu(x):
    return x * jax.nn.sigmoid(x)


def _softplus(x):
    return jnp.maximum(x, 0.0) + jnp.log(1.0 + jnp.exp(-jnp.abs(x)))


def _log_sigmoid(x):
    return jnp.minimum(x, 0.0) - jnp.log(1.0 + jnp.exp(-jnp.abs(x)))


def _iota(shape, axis):
    return lax.broadcasted_iota(jnp.int32, shape, axis)


def _log2(n):
    k = int(math.log2(n))
    assert (1 << k) == n, f"{n} must be a power of two"
    return k


def _ffn_kernel(x_ref, g_ref, wg_ref, wu_ref, wo_ref, gf_ref, o_ref, h_ref, a_ref, *, fc, final_norm):
    x = x_ref[...]
    h_ref[...] = _rms(x, g_ref[...]).astype(BF16)
    d_ff = wg_ref.shape[1]
    for c in range(d_ff // fc):
        sl = slice(c * fc, (c + 1) * fc)
        gate = _dot(h_ref[...], wg_ref[:, sl])
        up = _dot(h_ref[...], wu_ref[:, sl])
        a_ref[:, sl] = (_silu(gate) * up).astype(BF16)
    y = x + 0.5 * _dot(a_ref[...], wo_ref[...])
    if final_norm:
        y = _rms(y, gf_ref[...])
    o_ref[...] = y


def _const_spec(shape):
    nd = len(shape)
    return pl.BlockSpec(shape, lambda *_: (0,) * nd, pipeline_mode=pl.Buffered(1))


def _ffn(x2d, g, w_gate, w_up, w_out, g_final, *, tm, fc, final_norm):
    m, d = x2d.shape
    d_ff = w_gate.shape[1]
    assert m % tm == 0 and d_ff % fc == 0
    kern = functools.partial(_ffn_kernel, fc=fc, final_norm=final_norm)
    return pl.pallas_call(
        kern,
        grid=(m // tm,),
        in_specs=[
            pl.BlockSpec((tm, d), lambda i: (i, 0)),
            _const_spec((1, d)),
            _const_spec((d, d_ff)),
            _const_spec((d, d_ff)),
            _const_spec((d_ff, d)),
            _const_spec((1, d)),
        ],
        out_specs=pl.BlockSpec((tm, d), lambda i: (i, 0)),
        out_shape=jax.ShapeDtypeStruct((m, d), F32),
        scratch_shapes=[pltpu.VMEM((tm, d), BF16), pltpu.VMEM((tm, d_ff), BF16)],
        compiler_params=pltpu.CompilerParams(
            dimension_semantics=("arbitrary",), vmem_limit_bytes=VMEM_LIMIT_BYTES),
        name="ffn",
    )(x2d, g, w_gate, w_up, w_out, g_final)


class _Dims:
    def __init__(self, state_gla, state_ssd, cache_conv, state_ret, gla_w_gate2, ssd_conv_w, d_model):
        _, _, self.gh, self.gn, self.gp = state_gla.shape
        _, _, self.sh, self.sn, self.sp = state_ssd.shape
        _, _, self.rh, self.rn, self.rp = state_ret.shape
        self.conv_k = ssd_conv_w.shape[1]
        self.conv_dim = cache_conv.shape[-1]
        self.rank = gla_w_gate2.shape[1]
        self.d = d_model
        self.s_inner = self.sh * self.sp
        self.sg = (self.conv_dim - self.s_inner) // (2 * self.sn)
        assert self.gh * self.gn == 2 * LANES and self.gp == LANES
        assert self.rh * self.rn == 2 * LANES and self.rp == LANES
        assert self.sp * 2 == LANES and self.sg * self.sn == LANES and self.sh <= LANES
        assert self.rank <= LANES and self.conv_k - 1 <= SUBLANES
        w = [self.gh * self.gn, self.gh * self.gn, 2 * LANES, self.gh * self.gp, self.gh * self.gp,
             self.s_inner, self.conv_dim,
             self.rh * self.rn, self.rh * self.rn, self.rh * self.rp, self.rh * self.rp,
             d_model, d_model, d_model]
        self.seg_names = ["gq", "gk", "lrdt", "gv", "gr", "mz", "mxbc", "rq", "rk", "rv", "rg", "ga", "gb", "gc"]
        self.seg = {}
        off = 0
        for n, wd in zip(self.seg_names, w):
            assert wd % MXU_COLS == 0
            self.seg[n] = (off, off + wd)
            off += wd
        self.in_cols = off


def _head_lane_mask(shape, h, width):
    lane = _iota(shape, 1)
    return (lane >> _log2(width)) == h


def _stack_heads(x, nh, width):
    parts = [jnp.where(_head_lane_mask(x.shape, h, width), x, 0.0).astype(BF16) for h in range(nh)]
    return jnp.concatenate(parts, axis=0)


def _linear_attention_chunk(q_sc, k_sc, sc_mul, q_in, k_up, v, dec_row, dec_col, st_ref, nh, n, p, issue):
    L = q_sc.shape[0]
    k_big = _stack_heads(k_sc, nh, n)
    sc = _dot_nt(q_sc.astype(BF16), k_big)
    lane = _iota(sc.shape, 1)
    row = _iota(sc.shape, 0)
    causal = (lane & (L - 1)) <= row
    sc = jnp.where(causal, sc if sc_mul is None else sc * sc_mul, 0.0)
    v_big = _stack_heads(v, nh, p)
    issue()
    st = st_ref[...]
    rowhead = _iota(st.shape, 0) >> _log2(n)
    st_bd = jnp.concatenate([jnp.where(rowhead == h, st, 0.0).astype(BF16) for h in range(nh)], axis=1)
    o = _dot(sc.astype(BF16), v_big) + _dot(q_in.astype(BF16), st_bd)
    rows = [k_up]
    if dec_col is None:
        rows.append(jnp.broadcast_to(dec_row, (SUBLANES, nh * n)))
    pad = LANES - L - (len(rows) - 1) * SUBLANES
    assert pad >= 0
    if pad:
        rows.append(jnp.zeros((pad, nh * n), F32))
    kt = jnp.concatenate(rows, axis=0).T if len(rows) > 1 else k_up.T
    if dec_col is None:
        dec_col = jnp.sum(jnp.where(_iota(kt.shape, 1) == L, kt, 0.0), axis=-1, keepdims=True)
    kt = kt.astype(BF16)
    upd = []
    for h in range(nh):
        v_h = v[:, h * p:(h + 1) * p].astype(BF16)
        if L < LANES:
            v_h = jnp.concatenate([v_h, jnp.zeros((LANES - L, p), BF16)], axis=0)
        upd.append(_dot(kt[h * n:(h + 1) * n, :], v_h))
    st_ref[...] = st * dec_col + jnp.concatenate(upd, axis=0)
    return o


def _group_norm_heads(o, nh, p):
    outs = []
    for h in range(nh):
        oh = o[:, h * p:(h + 1) * p]
        mu = jnp.mean(oh, axis=-1, keepdims=True)
        dlt = oh - mu
        var = jnp.mean(dlt * dlt, axis=-1, keepdims=True)
        outs.append(dlt * lax.rsqrt(var + EPS))
    return jnp.concatenate(outs, axis=1)


def _block_tril(tt, L):
    row = _iota((tt, tt), 0)
    col = _iota((tt, tt), 1)
    k = _log2(L)
    return jnp.where((col <= row) & ((row >> k) == (col >> k)), 1.0, 0.0).astype(BF16)


def _ret_log_gamma(head, nh):
    out = jnp.full(head.shape, math.log1p(-2.0 ** (-5.0 - (nh - 1))), F32)
    for h in range(nh - 1):
        out = jnp.where(head == h, math.log1p(-2.0 ** (-5.0 - h)), out)
    return out


def _ssd_decay_terms(b8, dtc, dm):
    L = b8.shape[0]
    bl8 = b8[L - 1:L, :]
    w8 = dtc * jnp.exp(bl8 - b8)
    b8t = _transpose_rows(b8)
    dtt = _transpose_rows(dtc)
    lane8 = _iota(b8.shape, 1)
    ebl8 = jnp.exp(bl8)
    bcols, eb_cols, w_cols, dec_cols = [], [], [], []
    for hh in range(dm.sh):
        bcol = jnp.sum(jnp.where(lane8 == hh, b8, 0.0), axis=-1, keepdims=True)
        bcols.append(bcol)
        eb_cols.append(jnp.broadcast_to(jnp.exp(bcol), (L, LANES)))
        w_cols.append(jnp.broadcast_to(
            jnp.sum(jnp.where(lane8 == hh, w8, 0.0), axis=-1, keepdims=True), (L, LANES)))
        dec_cols.append(jnp.broadcast_to(
            jnp.sum(jnp.where(_iota(bl8.shape, 1) == hh, ebl8, 0.0), axis=-1, keepdims=True), (1, LANES)))

    def pair_lanes(cols):
        first = _iota(cols[0].shape, 1) < dm.sp
        return jnp.concatenate(
            [jnp.where(first, cols[2 * j], cols[2 * j + 1]) for j in range(dm.sh // 2)], axis=1)

    return bcols, b8t, dtt, pair_lanes(eb_cols), pair_lanes(w_cols), pair_lanes(dec_cols)


def _retention_tables(dm, L):
    nlane = dm.rh * dm.rn
    lgam_q = _ret_log_gamma(_iota((L, nlane), 1) >> _log2(dm.rn), dm.rh)
    pos_q = _iota((L, nlane), 0).astype(F32)
    q_dec = jnp.exp((pos_q + 1.0) * lgam_q)
    k_dec = jnp.exp((L - 1.0 - pos_q) * lgam_q)
    lgam_r = _ret_log_gamma(_iota((nlane, dm.rp), 0) >> _log2(dm.rn), dm.rh)
    st_dec = jnp.exp(L * lgam_r)
    lgam_s = _ret_log_gamma(_iota((L, dm.rh * L), 1) >> _log2(L), dm.rh)
    dpos = (_iota((L, dm.rh * L), 0) - (_iota((L, dm.rh * L), 1) & (L - 1))).astype(F32)
    sc_dec = jnp.exp(jnp.maximum(dpos, 0.0) * lgam_s)
    return q_dec, k_dec, st_dec, sc_dec


def _transpose_rows(x):
    L = x.shape[0]
    if L % LANES:
        pad = LANES - L % LANES
        x = jnp.concatenate([x, jnp.zeros((pad, x.shape[1]), x.dtype)], axis=0)
    return x.T[:, :L]


def _swap_halves(x, n):
    width = x.shape[1]
    half = n // 2
    first = (_iota(x.shape, 1) & (n - 1)) < half
    return jnp.where(first, pltpu.roll(x, width - half, 1), pltpu.roll(x, half, 1))


def _mixer_kernel(x_ref, xn_ref, cos_ref, sin_ref, sg0_ref, ss0_ref, cv0_ref, sr0_ref,
                  nrm_ref, win_ref, wg2_ref, bg_ref, gnorm_ref, cw_ref, cb_ref, dtb_ref, alog_ref,
                  dexp_ref, snorm_ref, rnorm_ref, wbg_ref, wbs_ref, wbr_ref, wout_ref,
                  y_ref, sg_ref, ss_ref, cv_ref, sr_ref,
                  stg, sts, cbuf, strt, ogla, ossd, oret, hbuf,
                  *, dm, tt, lg, ls, lr):
    t = pl.program_id(1)
    nt = pl.num_programs(1)
    ck = dm.conv_k - 1

    @pl.when(t == 0)
    def _():
        stg[...] = sg0_ref[...]
        strt[...] = sr0_ref[...]
        z0 = ss0_ref[...]
        hpg = dm.sh // dm.sg
        sts[...] = jnp.concatenate(
            [jnp.where(_head_lane_mask(z0.shape, g, dm.sp * hpg), z0, 0.0) for g in range(dm.sg)], axis=0)
        cbuf[SUBLANES - ck:SUBLANES, :] = cv0_ref[...]

    step = pl.program_id(0) * nt + t
    slot = lax.rem(step, 2)

    @pl.when(step == 0)
    def _():
        hbuf[0] = _rms(x_ref[...], nrm_ref[...]).astype(BF16)

    pending = []
    for name in ["gq", "gk", "lrdt", "gv", "gr", "mxbc", "mz", "rq", "rk", "rv", "rg", "ga", "gb", "gc"]:
        a, b = dm.seg[name]
        pending += [(name, s) for s in range(a, b, MXU_COLS)]
    parts = {name: [] for name in dm.seg_names}

    def issue(n=1):
        for _ in range(n):
            if pending:
                name, s = pending.pop(0)
                parts[name].append(_dot(hbuf[slot], win_ref[:, s:s + MXU_COLS]))

    def take(name):
        a, b = dm.seg[name]
        while len(parts[name]) * MXU_COLS < b - a:
            issue()
        return parts[name][0] if len(parts[name]) == 1 else jnp.concatenate(parts[name], axis=1)

    q = take("gq") * (dm.gn ** -0.5)
    k = take("gk")
    lrdt = take("lrdt")
    hbuf[1 - slot] = _rms(xn_ref[...], nrm_ref[...]).astype(BF16)
    ret_tabs = _retention_tables(dm, lr)
    gate = _dot(lrdt[:, :LANES].astype(BF16), wg2_ref[...]) + bg_ref[...]
    dt8 = _softplus(lrdt[:, LANES:] + dtb_ref[...])
    a8 = dt8 * (-jnp.exp(alog_ref[...]))
    b8_all = _dot_exact01(_block_tril(tt, ls), a8)
    issue(3)
    log_a = _log_sigmoid(gate) * (1.0 / GLA_TAU)
    b_all = _dot_exact01(_block_tril(tt, lg), log_a)
    issue(3)
    qd_all = q * jnp.exp(b_all)
    kd_all = k * jnp.exp(-b_all)
    v = take("gv")
    ssd_terms = []
    for c in range(tt // lg):
        r = slice(c * lg, (c + 1) * lg)
        b = b_all[r]
        bl = b[lg - 1:lg, :]
        ks = k[r] * jnp.exp(bl - b)
        ogla[r, :] = _linear_attention_chunk(qd_all[r], kd_all[r], None, qd_all[r], ks, v[r],
                                             jnp.exp(bl), None, stg, dm.gh, dm.gn, dm.gp,
                                             issue if c else (lambda: None))
        if c < tt // ls:
            rs = slice(c * ls, (c + 1) * ls)
            ssd_terms.append(_ssd_decay_terms(b8_all[rs], dt8[rs], dm))
    for c in range(len(ssd_terms), tt // ls):
        rs = slice(c * ls, (c + 1) * ls)
        ssd_terms.append(_ssd_decay_terms(b8_all[rs], dt8[rs], dm))
    y_gla = (_group_norm_heads(ogla[...], dm.gh, dm.gp) * gnorm_ref[...] * _silu(take("gr"))).astype(BF16)
    issue(2)

    xbc_pre = take("mxbc")
    cbuf[SUBLANES:SUBLANES + tt, :] = xbc_pre
    conv = cb_ref[...] + cw_ref[ck:ck + 1, :] * xbc_pre
    for j in range(ck):
        conv = conv + cw_ref[j:j + 1, :] * cbuf[SUBLANES - ck + j:SUBLANES - ck + j + tt, :]
    new_carry = cbuf[SUBLANES + tt - ck:SUBLANES + tt, :]
    cbuf[SUBLANES - ck:SUBLANES, :] = new_carry
    xbc = _silu(conv)
    xs = xbc[:, :dm.s_inner]
    bm = xbc[:, dm.s_inner:dm.s_inner + LANES]
    cm = xbc[:, dm.s_inner + LANES:dm.s_inner + 2 * LANES]
    hpg = dm.sh // dm.sg
    for c in range(tt // ls):
        r = slice(c * ls, (c + 1) * ls)
        bcols, b8t, dtt, eb_exp, w_exp, dec_row = ssd_terms[c]
        xs_c = xs[r]
        xs_bf = xs_c.astype(BF16)
        cm_c = cm[r]
        bm_c = bm[r].astype(BF16)
        causal = _iota((ls, ls), 1) <= _iota((ls, ls), 0)
        gmats = [_dot_nt(jnp.where(_head_lane_mask(cm_c.shape, g, dm.sn), cm_c, 0.0).astype(BF16), bm_c)
                 for g in range(dm.sg)]
        st = sts[...]
        y_inter = _dot(cm_c.astype(BF16), st.astype(BF16)) * eb_exp
        upd = _dot_tn(bm_c, (xs_c * w_exp).astype(BF16))
        blk = (_iota(upd.shape, 0) >> _log2(dm.sn)) == (_iota(upd.shape, 1) >> _log2(dm.sp * hpg))
        sts[...] = st * dec_row + jnp.where(blk, upd, 0.0)
        issue(2)
        y_pairs = []
        for hh in range(dm.sh):
            seg = jnp.where(causal, bcols[hh] - b8t[hh:hh + 1, :], -jnp.inf)
            m_h = gmats[hh // hpg] * dtt[hh:hh + 1, :] * jnp.exp(seg)
            pr = hh // 2
            y_pairs.append(_dot(m_h.astype(BF16), xs_bf[:, pr * LANES:(pr + 1) * LANES]))
            if hh % 2:
                issue()
        first = _iota((ls, LANES), 1) < dm.sp
        y_intra = jnp.concatenate(
            [jnp.where(first, y_pairs[2 * j], y_pairs[2 * j + 1]) for j in range(dm.sh // 2)], axis=1)
        ossd[r, :] = y_intra + y_inter
    y = ossd[...] + dexp_ref[...] * xs
    y = y * _silu(take("mz"))
    issue()
    y_ssd = _rms(y, snorm_ref[...]).astype(BF16)

    late = [(y_gla, wbg_ref, s) for s in range(0, dm.d, MXU_COLS)]
    late += [(y_ssd, wbs_ref, s) for s in range(0, dm.d, MXU_COLS)]
    yb_parts = []

    def issue_late(n=1):
        for _ in range(n):
            if late:
                yv, w_ref, s = late.pop(0)
                yb_parts.append(_dot(yv, w_ref[:, s:s + MXU_COLS]))

    cos = cos_ref[...]
    sin = sin_ref[...]
    rq_pre = take("rq")
    rk_pre = take("rk")
    issue(len(pending))
    rq = rq_pre * cos + _swap_halves(rq_pre, dm.rn) * sin
    issue_late()
    rk = (rk_pre * cos + _swap_halves(rk_pre, dm.rn) * sin) * (dm.rn ** -0.5)
    issue_late()
    rv = take("rv")
    q_dec, k_dec, st_dec, sc_dec = ret_tabs
    for c in range(tt // lr):
        r = slice(c * lr, (c + 1) * lr)
        oret[r, :] = _linear_attention_chunk(rq[r], rk[r], sc_dec, rq[r] * q_dec, rk[r] * k_dec, rv[r],
                                             None, st_dec, strt, dm.rh, dm.rn, dm.rp, issue_late)
        issue_late()
    issue_late(len(late))
    ncol = dm.d // MXU_COLS
    yb_gla = jnp.concatenate(yb_parts[:ncol], axis=1)
    yb_ssd = jnp.concatenate(yb_parts[ncol:], axis=1)
    ga, gb, gc = take("ga"), take("gb"), take("gc")
    m2 = jax.nn.sigmoid(ga) * yb_gla + jax.nn.sigmoid(gb) * yb_ssd
    y_ret = (_group_norm_heads(oret[...], dm.rh, dm.rp) * rnorm_ref[...] * _silu(take("rg"))).astype(BF16)

    halves = 2 if tt % (2 * SUBLANES * 2) == 0 else 1
    hr = tt // halves
    for i in range(halves):
        r = slice(i * hr, (i + 1) * hr)
        m = m2[r] + jax.nn.sigmoid(gc[r]) * _dot(y_ret[r], wbr_ref[...])
        y_ref[r, :] = x_ref[r, :] + _dot(m.astype(BF16), wout_ref[...])

    @pl.when(t == nt - 1)
    def _():
        sg_ref[...] = stg[...]
        sr_ref[...] = strt[...]
        sz = sts[...]
        acc = sz[0:dm.sn]
        for g in range(1, dm.sg):
            acc = acc + sz[g * dm.sn:(g + 1) * dm.sn]
        ss_ref[...] = acc
        cv_ref[...] = cbuf[SUBLANES - ck:SUBLANES, :]


def _mixer(x, cos, sin, sg0, ss0, cv0, sr0, wts, dm, *, tt, lg, ls, lr):
    bsz, seq, d = x.shape
    assert seq % tt == 0 and tt % lg == 0 and tt % ls == 0 and tt % lr == 0
    kern = functools.partial(_mixer_kernel, dm=dm, tt=tt, lg=lg, ls=ls, lr=lr)
    gq = dm.gh * dm.gn
    rq = dm.rh * dm.rn

    def per_b(shape):
        nd = len(shape)
        return pl.BlockSpec((None,) + shape, lambda b, t: (b,) + (0,) * nd)

    nb, nt = bsz, seq // tt

    def next_tile(b, t):
        wrap = t + 1 >= nt
        return jnp.where(wrap, jnp.minimum(b + 1, nb - 1), b), jnp.where(wrap, 0, t + 1), 0

    in_specs = [
        pl.BlockSpec((None, tt, d), lambda b, t: (b, t, 0)),
        pl.BlockSpec((None, tt, d), next_tile),
        pl.BlockSpec((tt, rq), lambda b, t: (t, 0)),
        pl.BlockSpec((tt, rq), lambda b, t: (t, 0)),
        per_b((gq, dm.gp)), per_b((dm.sn, dm.s_inner)), per_b((dm.conv_k - 1, dm.conv_dim)), per_b((rq, dm.rp)),
    ] + [_const_spec(w.shape) for w in wts]
    out_specs = [
        pl.BlockSpec((None, tt, d), lambda b, t: (b, t, 0)),
        per_b((gq, dm.gp)), per_b((dm.sn, dm.s_inner)), per_b((dm.conv_k - 1, dm.conv_dim)), per_b((rq, dm.rp)),
    ]
    out_shape = [
        jax.ShapeDtypeStruct((bsz, seq, d), F32),
        jax.ShapeDtypeStruct((bsz, gq, dm.gp), F32),
        jax.ShapeDtypeStruct((bsz, dm.sn, dm.s_inner), F32),
        jax.ShapeDtypeStruct((bsz, dm.conv_k - 1, dm.conv_dim), F32),
        jax.ShapeDtypeStruct((bsz, rq, dm.rp), F32),
    ]
    scratch = [
        pltpu.VMEM((gq, dm.gp), F32),
        pltpu.VMEM((dm.sg * dm.sn, dm.s_inner), F32),
        pltpu.VMEM((tt + SUBLANES, dm.conv_dim), F32),
        pltpu.VMEM((rq, dm.rp), F32),
        pltpu.VMEM((tt, dm.gh * dm.gp), F32),
        pltpu.VMEM((tt, dm.s_inner), F32),
        pltpu.VMEM((tt, dm.rh * dm.rp), F32),
        pltpu.VMEM((2, tt, d), BF16),
    ]
    return pl.pallas_call(
        kern,
        grid=(bsz, seq // tt),
        in_specs=in_specs,
        out_specs=out_specs,
        out_shape=out_shape,
        scratch_shapes=scratch,
        compiler_params=pltpu.CompilerParams(
            dimension_semantics=("arbitrary", "arbitrary"), vmem_limit_bytes=VMEM_LIMIT_BYTES),
        name="mixer",
    )(x, x, cos, sin, sg0, ss0, cv0, sr0, *wts)


def _pad_cols(w, n):
    return jnp.pad(w, ((0, 0), (0, n - w.shape[1])))


def _row(v, n=None):
    v = v.reshape(1, -1).astype(F32)
    return v if n is None else _pad_cols(v, n)


def _rope_tables(pos, dm):
    half = dm.rn // 2
    freqs = ROPE_BASE ** (-jnp.arange(half, dtype=F32) / half)
    ang = pos.astype(F32)[:, None] * freqs[None, :]
    cos = jnp.cos(ang)
    sin = jnp.sin(ang)
    cos_h = jnp.concatenate([cos, cos], axis=1)
    sin_h = jnp.concatenate([-sin, sin], axis=1)
    return jnp.tile(cos_h, (1, dm.rh)), jnp.tile(sin_h, (1, dm.rh))


def _state_to_kernel_t(s):
    b, h, n, p = s.shape
    return s.reshape(b, h * n, p)


def _state_from_kernel_t(s, h):
    b, hn, p = s.shape
    return s.reshape(b, h, hn // h, p)


def _ssd_to_kernel(s):
    b, h, n, p = s.shape
    return jnp.transpose(s, (0, 2, 1, 3)).reshape(b, n, h * p)


def _ssd_from_kernel(s, h):
    b, n, hp = s.shape
    return jnp.transpose(s.reshape(b, n, h, hp // h), (0, 2, 1, 3))


def _tiles(seq):
    if seq % SCAN_CHUNK == 0:
        tt = 256 if seq % 256 == 0 else SCAN_CHUNK
        return dict(tt=tt, lg=SCAN_CHUNK, ls=min(tt, 128), lr=min(tt, 128))
    return dict(tt=seq, lg=seq, ls=seq, lr=seq)


def kernel(x_prompt, x_sample, state_gla, state_ssd, cache_conv, state_ret, norm_ffn1, ffn1_w_in, ffn1_w_out,
           norm_mix, w_in, gla_w_gate2, gla_b_gate, gla_norm, ssd_conv_w, ssd_conv_b, ssd_dt_bias, ssd_a_log,
           ssd_d, ssd_norm, ret_norm, w_branch_gla, w_branch_ssd, w_branch_ret, w_out, norm_ffn2, ffn2_w_in,
           ffn2_w_out, norm_final):
    depth = w_in.shape[0]
    d = x_prompt.shape[-1]
    d_ff = ffn1_w_out.shape[1]
    dm = _Dims(state_gla, state_ssd, cache_conv, state_ret, gla_w_gate2, ssd_conv_w, d)

    split_sizes = (dm.gh * dm.gn, dm.gh * dm.gn, dm.gh * dm.gp, dm.gh * dm.gp, dm.rank,
                   dm.s_inner, dm.conv_dim, dm.sh,
                   dm.rh * dm.rn, dm.rh * dm.rn, dm.rh * dm.rp, dm.rh * dm.rp, d, d, d)
    pts, acc = [], 0
    for s in split_sizes[:-1]:
        acc += s
        pts.append(acc)

    def mixer_weights(l):
        (s_gq, s_gk, s_gv, s_gr, s_glr, s_mz, s_mxbc, s_mdt,
         s_rq, s_rk, s_rv, s_rg, s_ga, s_gb, s_gc) = jnp.split(w_in[l], pts, axis=1)
        win = jnp.concatenate(
            [s_gq, s_gk, _pad_cols(s_glr, LANES), _pad_cols(s_mdt, LANES), s_gv, s_gr, s_mz, s_mxbc,
             s_rq, s_rk, s_rv, s_rg, s_ga, s_gb, s_gc], axis=1).astype(BF16)
        assert win.shape[1] == dm.in_cols
        wg2 = jnp.pad(gla_w_gate2[l], ((0, LANES - dm.rank), (0, 0))).astype(BF16)
        return [
            _row(norm_mix[l]), win, wg2, _row(gla_b_gate[l]), _row(gla_norm[l]),
            ssd_conv_w[l].astype(F32), _row(ssd_conv_b[l]), _row(ssd_dt_bias[l], LANES), _row(ssd_a_log[l], LANES),
            _row(jnp.repeat(ssd_d[l], dm.sp)), _row(ssd_norm[l]), _row(ret_norm[l]),
            w_branch_gla[l].astype(BF16), w_branch_ssd[l].astype(BF16), w_branch_ret[l].astype(BF16),
            w_out[l].astype(BF16),
        ]

    def ffn_weights(nrm, w_i, w_o, l):
        return _row(nrm[l]), w_i[l][:, :d_ff].astype(BF16), w_i[l][:, d_ff:].astype(BF16), w_o[l].astype(BF16)

    g_final = _row(norm_final)

    def run_group(x, pos, states):
        bsz, seq, _ = x.shape
        m = bsz * seq
        tm = 1024 if m % 1024 == 0 else m
        cos, sin = _rope_tables(pos, dm)
        tl = _tiles(seq)
        new_states = []
        for l in range(depth):
            sg0, ss0, cv0, sr0 = states(l)
            g1, wg1, wu1, wo1 = ffn_weights(norm_ffn1, ffn1_w_in, ffn1_w_out, l)
            x = _ffn(x.reshape(m, d), g1, wg1, wu1, wo1, g_final, tm=tm, fc=256, final_norm=False).reshape(bsz, seq, d)
            x, sg, ss, cv, sr = _mixer(x, cos, sin, _state_to_kernel_t(sg0), _ssd_to_kernel(ss0), cv0,
                                       _state_to_kernel_t(sr0), mixer_weights(l), dm, **tl)
            g2, wg2_, wu2, wo2 = ffn_weights(norm_ffn2, ffn2_w_in, ffn2_w_out, l)
            x = _ffn(x.reshape(m, d), g2, wg2_, wu2, wo2, g_final, tm=tm, fc=256,
                     final_norm=(l == depth - 1)).reshape(bsz, seq, d)
            new_states.append((_state_from_kernel_t(sg, dm.gh), _ssd_from_kernel(ss, dm.sh), cv,
                               _state_from_kernel_t(sr, dm.rh)))
        return x, new_states

    bp, tp = x_prompt.shape[:2]
    ts = x_sample.shape[1]
    dt_ = x_prompt.dtype

    def zero_states(_):
        return (jnp.zeros((bp, dm.gh, dm.gn, dm.gp), dt_), jnp.zeros((bp, dm.sh, dm.sn, dm.sp), dt_),
                jnp.zeros((bp, dm.conv_k - 1, dm.conv_dim), dt_), jnp.zeros((bp, dm.rh, dm.rn, dm.rp), dt_))

    def carried_states(l):
        return state_gla[l], state_ssd[l], cache_conv[l], state_ret[l]

    y_p, st_p = run_group(x_prompt, jnp.arange(tp, dtype=jnp.int32), zero_states)
    y_s, st_s = run_group(x_sample, PAST_LEN + jnp.arange(ts, dtype=jnp.int32), carried_states)

    def stack(sts, i):
        return jnp.stack([s[i] for s in sts])

    return (y_p, y_s, stack(st_p, 0), stack(st_p, 1), stack(st_p, 2), stack(st_p, 3),
            stack(st_s, 0), stack(st_s, 1), stack(st_s, 2), stack(st_s, 3))
```

```python
import functools
import math

import jax
import jax.numpy as jnp
from jax import lax
from jax.experimental import pallas as pl
from jax.experimental.pallas import tpu as pltpu

F32 = jnp.float32
BF16 = jnp.bfloat16

EPS = 1e-6
PAST_LEN = 1024
GLA_TAU = 16.0
ROPE_BASE = 10000.0
SCAN_CHUNK = 64

LANES = 128
SUBLANES = 8
MXU_COLS = 256
FFN_OUT_ROWS = 256
VMEM_LIMIT_BYTES = 56 * 1024 * 1024


def _dot(a, b):
    return jnp.dot(a, b, preferred_element_type=F32)


def _dot_nt(a, b):
    return lax.dot_general(a, b, (((1,), (1,)), ((), ())), preferred_element_type=F32)


def _dot_tn(a, b):
    return lax.dot_general(a, b, (((0,), (0,)), ((), ())), preferred_element_type=F32)


def _split3(x):
    hi = x.astype(BF16)
    r1 = x - hi.astype(F32)
    mid = r1.astype(BF16)
    lo = (r1 - mid.astype(F32)).astype(BF16)
    return hi, mid, lo


def _dot_exact01(m01, x):
    hi, mid, lo = _split3(x)
    return _dot(m01, hi) + _dot(m01, mid) + _dot(m01, lo)


def _rms(x, g):
    return x * lax.rsqrt(jnp.mean(x * x, axis=-1, keepdims=True) + EPS) * g


def _silu(x):
    return x * jax.nn.sigmoid(x)


def _softplus(x):
    return jnp.maximum(x, 0.0) + jnp.log(1.0 + jnp.exp(-jnp.abs(x)))


def _log_sigmoid(x):
    return jnp.minimum(x, 0.0) - jnp.log(1.0 + jnp.exp(-jnp.abs(x)))


def _iota(shape, axis):
    return lax.broadcasted_iota(jnp.int32, shape, axis)


def _log2(n):
    k = int(math.log2(n))
    assert (1 << k) == n, f"{n} must be a power of two"
    return k


def _ffn_kernel(*refs, fc, post):
    if post is None:
        x_ref, g_ref, wg_ref, wu_ref, wo_ref, o_ref, h_ref, a_ref = refs
    elif post == "norm":
        x_ref, g_ref, wg_ref, wu_ref, wo_ref, g2_ref, o_ref, h_ref, a_ref = refs
    else:
        x_ref, g_ref, wg_ref, wu_ref, wo_ref, g2_ref, o_ref, hn_ref, h_ref, a_ref = refs
    x = x_ref[...]
    h_ref[...] = _rms(x, g_ref[...]).astype(BF16)
    d_ff = wg_ref.shape[1]
    for c in range(d_ff // fc):
        sl = slice(c * fc, (c + 1) * fc)
        gate = _dot(h_ref[...], wg_ref[:, sl])
        up = _dot(h_ref[...], wu_ref[:, sl])
        a_ref[:, sl] = (_silu(gate) * up).astype(BF16)
    tm = x_ref.shape[0]
    rb = FFN_OUT_ROWS if tm % FFN_OUT_ROWS == 0 else tm
    for i in range(tm // rb):
        r = slice(i * rb, (i + 1) * rb)
        y = x_ref[r, :] + 0.5 * _dot(a_ref[r, :], wo_ref[...])
        if post == "norm":
            y = _rms(y, g2_ref[...])
        o_ref[r, :] = y
        if post == "emit":
            hn_ref[r, :] = _rms(y, g2_ref[...]).astype(BF16)


def _const_spec(shape):
    nd = len(shape)
    return pl.BlockSpec(shape, lambda *_: (0,) * nd, pipeline_mode=pl.Buffered(1))


def _ffn(x2d, g, w_gate, w_up, w_out, g2, *, tm, fc, post):
    m, d = x2d.shape
    d_ff = w_gate.shape[1]
    assert m % tm == 0 and d_ff % fc == 0 and (g2 is None) == (post is None)
    kern = functools.partial(_ffn_kernel, fc=fc, post=post)
    row_spec = pl.BlockSpec((tm, d), lambda i: (i, 0))
    in_specs = [row_spec, _const_spec((1, d)), _const_spec((d, d_ff)), _const_spec((d, d_ff)),
                _const_spec((d_ff, d))]
    args = [x2d, g, w_gate, w_up, w_out]
    if post is not None:
        in_specs.append(_const_spec((1, d)))
        args.append(g2)
    out_specs, out_shape = row_spec, jax.ShapeDtypeStruct((m, d), F32)
    if post == "emit":
        out_specs, out_shape = [row_spec, row_spec], [out_shape, jax.ShapeDtypeStruct((m, d), BF16)]
    return pl.pallas_call(
        kern,
        grid=(m // tm,),
        in_specs=in_specs,
        out_specs=out_specs,
        out_shape=out_shape,
        scratch_shapes=[pltpu.VMEM((tm, d), BF16), pltpu.VMEM((tm, d_ff), BF16)],
        compiler_params=pltpu.CompilerParams(
            dimension_semantics=("arbitrary",), vmem_limit_bytes=VMEM_LIMIT_BYTES),
        name="ffn",
    )(*args)


class _Dims:
    def __init__(self, state_gla, state_ssd, cache_conv, state_ret, gla_w_gate2, ssd_conv_w, d_model):
        _, _, self.gh, self.gn, self.gp = state_gla.shape
        _, _, self.sh, self.sn, self.sp = state_ssd.shape
        _, _, self.rh, self.rn, self.rp = state_ret.shape
        self.conv_k = ssd_conv_w.shape[1]
        self.conv_dim = cache_conv.shape[-1]
        self.rank = gla_w_gate2.shape[1]
        self.d = d_model
        self.s_inner = self.sh * self.sp
        self.sg = (self.conv_dim - self.s_inner) // (2 * self.sn)
        assert self.gh * self.gn == 2 * LANES and self.gp == LANES
        assert self.rh * self.rn == 2 * LANES and self.rp == LANES
        assert self.sp * 2 == LANES and self.sg * self.sn == LANES and self.sh <= LANES
        assert self.rank <= LANES and self.conv_k - 1 <= SUBLANES
        w = [self.gh * self.gn, self.gh * self.gn, 2 * LANES, self.gh * self.gp, self.gh * self.gp,
             self.s_inner, self.conv_dim,
             self.rh * self.rn, self.rh * self.rn, self.rh * self.rp, self.rh * self.rp,
             d_model, d_model, d_model]
        self.seg_names = ["gq", "gk", "lrdt", "gv", "gr", "mz", "mxbc", "rq", "rk", "rv", "rg", "ga", "gb", "gc"]
        self.seg = {}
        off = 0
        for n, wd in zip(self.seg_names, w):
            assert wd % MXU_COLS == 0
            self.seg[n] = (off, off + wd)
            off += wd
        self.in_cols = off


def _head_lane_mask(shape, h, width):
    lane = _iota(shape, 1)
    return (lane >> _log2(width)) == h


def _stack_heads(x, nh, width):
    parts = [jnp.where(_head_lane_mask(x.shape, h, width), x, 0.0).astype(BF16) for h in range(nh)]
    return jnp.concatenate(parts, axis=0)


def _linear_attention_chunk(q_sc, k_sc, sc_mul, q_in, k_up, v, dec_row, dec_col, st_ref, nh, n, p, issue):
    L = q_sc.shape[0]
    k_big = _stack_heads(k_sc, nh, n)
    sc = _dot_nt(q_sc.astype(BF16), k_big)
    lane = _iota(sc.shape, 1)
    row = _iota(sc.shape, 0)
    causal = (lane & (L - 1)) <= row
    sc = jnp.where(causal, sc if sc_mul is None else sc * sc_mul, 0.0)
    v_big = _stack_heads(v, nh, p)
    issue()
    st = st_ref[...]
    rowhead = _iota(st.shape, 0) >> _log2(n)
    st_bd = jnp.concatenate([jnp.where(rowhead == h, st, 0.0).astype(BF16) for h in range(nh)], axis=1)
    o = _dot(sc.astype(BF16), v_big) + _dot(q_in.astype(BF16), st_bd)
    rows = [k_up]
    if dec_col is None:
        rows.append(jnp.broadcast_to(dec_row, (SUBLANES, nh * n)))
    pad = LANES - L - (len(rows) - 1) * SUBLANES
    assert pad >= 0
    if pad:
        rows.append(jnp.zeros((pad, nh * n), F32))
    kt = jnp.concatenate(rows, axis=0).T if len(rows) > 1 else k_up.T
    if dec_col is None:
        dec_col = jnp.sum(jnp.where(_iota(kt.shape, 1) == L, kt, 0.0), axis=-1, keepdims=True)
    kt = kt.astype(BF16)
    upd = []
    for h in range(nh):
        v_h = v[:, h * p:(h + 1) * p].astype(BF16)
        if L < LANES:
            v_h = jnp.concatenate([v_h, jnp.zeros((LANES - L, p), BF16)], axis=0)
        upd.append(_dot(kt[h * n:(h + 1) * n, :], v_h))
    st_ref[...] = st * dec_col + jnp.concatenate(upd, axis=0)
    return o


def _group_norm_heads(o, nh, p):
    outs = []
    for h in range(nh):
        oh = o[:, h * p:(h + 1) * p]
        mu = jnp.mean(oh, axis=-1, keepdims=True)
        dlt = oh - mu
        var = jnp.mean(dlt * dlt, axis=-1, keepdims=True)
        outs.append(dlt * lax.rsqrt(var + EPS))
    return jnp.concatenate(outs, axis=1)


def _block_tril(tt, L):
    row = _iota((tt, tt), 0)
    col = _iota((tt, tt), 1)
    k = _log2(L)
    return jnp.where((col <= row) & ((row >> k) == (col >> k)), 1.0, 0.0).astype(BF16)


def _ret_log_gamma(head, nh):
    out = jnp.full(head.shape, math.log1p(-2.0 ** (-5.0 - (nh - 1))), F32)
    for h in range(nh - 1):
        out = jnp.where(head == h, math.log1p(-2.0 ** (-5.0 - h)), out)
    return out


def _ssd_decay_terms(b8, dtc, dm):
    L = b8.shape[0]
    bl8 = b8[L - 1:L, :]
    w8 = dtc * jnp.exp(bl8 - b8)
    b8t = _transpose_rows(b8)
    dtt = _transpose_rows(dtc)
    lane8 = _iota(b8.shape, 1)
    ebl8 = jnp.exp(bl8)
    bcols, eb_cols, w_cols, dec_cols = [], [], [], []
    for hh in range(dm.sh):
        bcol = jnp.sum(jnp.where(lane8 == hh, b8, 0.0), axis=-1, keepdims=True)
        bcols.append(bcol)
        eb_cols.append(jnp.broadcast_to(jnp.exp(bcol), (L, LANES)))
        w_cols.append(jnp.broadcast_to(
            jnp.sum(jnp.where(lane8 == hh, w8, 0.0), axis=-1, keepdims=True), (L, LANES)))
        dec_cols.append(jnp.broadcast_to(
            jnp.sum(jnp.where(_iota(bl8.shape, 1) == hh, ebl8, 0.0), axis=-1, keepdims=True), (1, LANES)))

    def pair_lanes(cols):
        first = _iota(cols[0].shape, 1) < dm.sp
        return jnp.concatenate(
            [jnp.where(first, cols[2 * j], cols[2 * j + 1]) for j in range(dm.sh // 2)], axis=1)

    return bcols, b8t, dtt, pair_lanes(eb_cols), pair_lanes(w_cols), pair_lanes(dec_cols)


def _retention_tables(dm, L):
    nlane = dm.rh * dm.rn
    lgam_q = _ret_log_gamma(_iota((L, nlane), 1) >> _log2(dm.rn), dm.rh)
    pos_q = _iota((L, nlane), 0).astype(F32)
    q_dec = jnp.exp((pos_q + 1.0) * lgam_q)
    k_dec = jnp.exp((L - 1.0 - pos_q) * lgam_q)
    lgam_r = _ret_log_gamma(_iota((nlane, dm.rp), 0) >> _log2(dm.rn), dm.rh)
    st_dec = jnp.exp(L * lgam_r)
    lgam_s = _ret_log_gamma(_iota((L, dm.rh * L), 1) >> _log2(L), dm.rh)
    dpos = (_iota((L, dm.rh * L), 0) - (_iota((L, dm.rh * L), 1) & (L - 1))).astype(F32)
    sc_dec = jnp.exp(jnp.maximum(dpos, 0.0) * lgam_s)
    return q_dec, k_dec, st_dec, sc_dec


def _transpose_rows(x):
    L = x.shape[0]
    if L % LANES:
        pad = LANES - L % LANES
        x = jnp.concatenate([x, jnp.zeros((pad, x.shape[1]), x.dtype)], axis=0)
    return x.T[:, :L]


def _swap_halves(x, n):
    width = x.shape[1]
    half = n // 2
    first = (_iota(x.shape, 1) & (n - 1)) < half
    return jnp.where(first, pltpu.roll(x, width - half, 1), pltpu.roll(x, half, 1))


def _mixer_kernel(x_ref, h_ref, cos_ref, sin_ref, sg0_ref, ss0_ref, cv0_ref, sr0_ref,
                  win_ref, wg2_ref, bg_ref, gnorm_ref, cw_ref, cb_ref, dtb_ref, alog_ref,
                  dexp_ref, snorm_ref, rnorm_ref, wbg_ref, wbs_ref, wbr_ref, wout_ref,
                  y_ref, sg_ref, ss_ref, cv_ref, sr_ref,
                  stg, sts, cbuf, strt, ogla, ossd, oret,
                  *, dm, tt, lg, ls, lr):
    t = pl.program_id(1)
    nt = pl.num_programs(1)
    ck = dm.conv_k - 1

    @pl.when(t == 0)
    def _():
        stg[...] = sg0_ref[...]
        strt[...] = sr0_ref[...]
        z0 = ss0_ref[...]
        hpg = dm.sh // dm.sg
        sts[...] = jnp.concatenate(
            [jnp.where(_head_lane_mask(z0.shape, g, dm.sp * hpg), z0, 0.0) for g in range(dm.sg)], axis=0)
        cbuf[SUBLANES - ck:SUBLANES, :] = cv0_ref[...]

    pending = []
    for name in ["gq", "gk", "lrdt", "gv", "gr", "mxbc", "mz", "rq", "rk", "rv", "rg", "ga", "gb", "gc"]:
        a, b = dm.seg[name]
        pending += [(name, s) for s in range(a, b, MXU_COLS)]
    parts = {name: [] for name in dm.seg_names}

    def issue(n=1):
        for _ in range(n):
            if pending:
                name, s = pending.pop(0)
                parts[name].append(_dot(h_ref[...], win_ref[:, s:s + MXU_COLS]))

    def take(name):
        a, b = dm.seg[name]
        while len(parts[name]) * MXU_COLS < b - a:
            issue()
        return parts[name][0] if len(parts[name]) == 1 else jnp.concatenate(parts[name], axis=1)

    q = take("gq") * (dm.gn ** -0.5)
    k = take("gk")
    lrdt = take("lrdt")
    ret_tabs = _retention_tables(dm, lr)
    gate = _dot(lrdt[:, :LANES].astype(BF16), wg2_ref[...]) + bg_ref[...]
    dt8 = _softplus(lrdt[:, LANES:] + dtb_ref[...])
    a8 = dt8 * (-jnp.exp(alog_ref[...]))
    b8_all = _dot_exact01(_block_tril(tt, ls), a8)
    issue(3)
    log_a = _log_sigmoid(gate) * (1.0 / GLA_TAU)
    b_all = _dot_exact01(_block_tril(tt, lg), log_a)
    issue(3)
    qd_all = q * jnp.exp(b_all)
    kd_all = k * jnp.exp(-b_all)
    v = take("gv")
    for c in range(tt // lg):
        r = slice(c * lg, (c + 1) * lg)
        b = b_all[r]
        bl = b[lg - 1:lg, :]
        ks = k[r] * jnp.exp(bl - b)
        ogla[r, :] = _linear_attention_chunk(qd_all[r], kd_all[r], None, qd_all[r], ks, v[r],
                                             jnp.exp(bl), None, stg, dm.gh, dm.gn, dm.gp,
                                             issue if c else (lambda: None))
    y_gla = (_group_norm_heads(ogla[...], dm.gh, dm.gp) * gnorm_ref[...] * _silu(take("gr"))).astype(BF16)
    issue(2)

    xbc_pre = take("mxbc")
    cbuf[SUBLANES:SUBLANES + tt, :] = xbc_pre
    conv = cb_ref[...] + cw_ref[ck:ck + 1, :] * xbc_pre
    for j in range(ck):
        conv = conv + cw_ref[j:j + 1, :] * cbuf[SUBLANES - ck + j:SUBLANES - ck + j + tt, :]
    new_carry = cbuf[SUBLANES + tt - ck:SUBLANES + tt, :]
    cbuf[SUBLANES - ck:SUBLANES, :] = new_carry
    xbc = _silu(conv)
    xs = xbc[:, :dm.s_inner]
    bm = xbc[:, dm.s_inner:dm.s_inner + LANES]
    cm = xbc[:, dm.s_inner + LANES:dm.s_inner + 2 * LANES]
    hpg = dm.sh // dm.sg
    for c in range(tt // ls):
        r = slice(c * ls, (c + 1) * ls)
        bcols, b8t, dtt, eb_exp, w_exp, dec_row = _ssd_decay_terms(b8_all[r], dt8[r], dm)
        xs_c = xs[r]
        xs_bf = xs_c.astype(BF16)
        cm_c = cm[r]
        bm_c = bm[r].astype(BF16)
        causal = _iota((ls, ls), 1) <= _iota((ls, ls), 0)
        gmats = [_dot_nt(jnp.where(_head_lane_mask(cm_c.shape, g, dm.sn), cm_c, 0.0).astype(BF16), bm_c)
                 for g in range(dm.sg)]
        st = sts[...]
        y_inter = _dot(cm_c.astype(BF16), st.astype(BF16)) * eb_exp
        upd = _dot_tn(bm_c, (xs_c * w_exp).astype(BF16))
        blk = (_iota(upd.shape, 0) >> _log2(dm.sn)) == (_iota(upd.shape, 1) >> _log2(dm.sp * hpg))
        sts[...] = st * dec_row + jnp.where(blk, upd, 0.0)
        issue(2)
        y_pairs = []
        for hh in range(dm.sh):
            seg = jnp.where(causal, bcols[hh] - b8t[hh:hh + 1, :], -jnp.inf)
            m_h = gmats[hh // hpg] * dtt[hh:hh + 1, :] * jnp.exp(seg)
            pr = hh // 2
            y_pairs.append(_dot(m_h.astype(BF16), xs_bf[:, pr * LANES:(pr + 1) * LANES]))
            if hh % 2:
                issue()
        first = _iota((ls, LANES), 1) < dm.sp
        y_intra = jnp.concatenate(
            [jnp.where(first, y_pairs[2 * j], y_pairs[2 * j + 1]) for j in range(dm.sh // 2)], axis=1)
        ossd[r, :] = y_intra + y_inter
    y = ossd[...] + dexp_ref[...] * xs
    y = y * _silu(take("mz"))
    issue()
    y_ssd = _rms(y, snorm_ref[...]).astype(BF16)

    late = [(y_gla, wbg_ref, s) for s in range(0, dm.d, MXU_COLS)]
    late += [(y_ssd, wbs_ref, s) for s in range(0, dm.d, MXU_COLS)]
    yb_parts = []

    def issue_late(n=1):
        for _ in range(n):
            if late:
                yv, w_ref, s = late.pop(0)
                yb_parts.append(_dot(yv, w_ref[:, s:s + MXU_COLS]))

    cos = cos_ref[...]
    sin = sin_ref[...]
    rq_pre = take("rq")
    rk_pre = take("rk")
    issue(len(pending))
    rq = rq_pre * cos + _swap_halves(rq_pre, dm.rn) * sin
    issue_late()
    rk = (rk_pre * cos + _swap_halves(rk_pre, dm.rn) * sin) * (dm.rn ** -0.5)
    issue_late()
    rv = take("rv")
    q_dec, k_dec, st_dec, sc_dec = ret_tabs
    for c in range(tt // lr):
        r = slice(c * lr, (c + 1) * lr)
        oret[r, :] = _linear_attention_chunk(rq[r], rk[r], sc_dec, rq[r] * q_dec, rk[r] * k_dec, rv[r],
                                             None, st_dec, strt, dm.rh, dm.rn, dm.rp, issue_late)
        issue_late()
    issue_late(len(late))
    ncol = dm.d // MXU_COLS
    yb_gla = jnp.concatenate(yb_parts[:ncol], axis=1)
    yb_ssd = jnp.concatenate(yb_parts[ncol:], axis=1)
    ga, gb, gc = take("ga"), take("gb"), take("gc")
    m2 = jax.nn.sigmoid(ga) * yb_gla + jax.nn.sigmoid(gb) * yb_ssd
    y_ret = (_group_norm_heads(oret[...], dm.rh, dm.rp) * rnorm_ref[...] * _silu(take("rg"))).astype(BF16)

    halves = 2 if tt % (2 * SUBLANES * 2) == 0 else 1
    hr = tt // halves
    for i in range(halves):
        r = slice(i * hr, (i + 1) * hr)
        m = m2[r] + jax.nn.sigmoid(gc[r]) * _dot(y_ret[r], wbr_ref[...])
        y_ref[r, :] = x_ref[r, :] + _dot(m.astype(BF16), wout_ref[...])

    @pl.when(t == nt - 1)
    def _():
        sg_ref[...] = stg[...]
        sr_ref[...] = strt[...]
        sz = sts[...]
        acc = sz[0:dm.sn]
        for g in range(1, dm.sg):
            acc = acc + sz[g * dm.sn:(g + 1) * dm.sn]
        ss_ref[...] = acc
        cv_ref[...] = cbuf[SUBLANES - ck:SUBLANES, :]


def _mixer(x, h, cos, sin, sg0, ss0, cv0, sr0, wts, dm, *, tt, lg, ls, lr):
    bsz, seq, d = x.shape
    assert seq % tt == 0 and tt % lg == 0 and tt % ls == 0 and tt % lr == 0
    kern = functools.partial(_mixer_kernel, dm=dm, tt=tt, lg=lg, ls=ls, lr=lr)
    gq = dm.gh * dm.gn
    rq = dm.rh * dm.rn

    def per_b(shape):
        nd = len(shape)
        return pl.BlockSpec((None,) + shape, lambda b, t: (b,) + (0,) * nd)

    in_specs = [
        pl.BlockSpec((None, tt, d), lambda b, t: (b, t, 0)),
        pl.BlockSpec((None, tt, d), lambda b, t: (b, t, 0)),
        pl.BlockSpec((tt, rq), lambda b, t: (t, 0)),
        pl.BlockSpec((tt, rq), lambda b, t: (t, 0)),
        per_b((gq, dm.gp)), per_b((dm.sn, dm.s_inner)), per_b((dm.conv_k - 1, dm.conv_dim)), per_b((rq, dm.rp)),
    ] + [_const_spec(w.shape) for w in wts]
    out_specs = [
        pl.BlockSpec((None, tt, d), lambda b, t: (b, t, 0)),
        per_b((gq, dm.gp)), per_b((dm.sn, dm.s_inner)), per_b((dm.conv_k - 1, dm.conv_dim)), per_b((rq, dm.rp)),
    ]
    out_shape = [
        jax.ShapeDtypeStruct((bsz, seq, d), F32),
        jax.ShapeDtypeStruct((bsz, gq, dm.gp), F32),
        jax.ShapeDtypeStruct((bsz, dm.sn, dm.s_inner), F32),
        jax.ShapeDtypeStruct((bsz, dm.conv_k - 1, dm.conv_dim), F32),
        jax.ShapeDtypeStruct((bsz, rq, dm.rp), F32),
    ]
    scratch = [
        pltpu.VMEM((gq, dm.gp), F32),
        pltpu.VMEM((dm.sg * dm.sn, dm.s_inner), F32),
        pltpu.VMEM((tt + SUBLANES, dm.conv_dim), F32),
        pltpu.VMEM((rq, dm.rp), F32),
        pltpu.VMEM((tt, dm.gh * dm.gp), F32),
        pltpu.VMEM((tt, dm.s_inner), F32),
        pltpu.VMEM((tt, dm.rh * dm.rp), F32),
    ]
    return pl.pallas_call(
        kern,
        grid=(bsz, seq // tt),
        in_specs=in_specs,
        out_specs=out_specs,
        out_shape=out_shape,
        scratch_shapes=scratch,
        compiler_params=pltpu.CompilerParams(
            dimension_semantics=("arbitrary", "arbitrary"), vmem_limit_bytes=VMEM_LIMIT_BYTES),
        name="mixer",
    )(x, h, cos, sin, sg0, ss0, cv0, sr0, *wts)


def _pad_cols(w, n):
    return jnp.pad(w, ((0, 0), (0, n - w.shape[1])))


def _row(v, n=None):
    v = v.reshape(1, -1).astype(F32)
    return v if n is None else _pad_cols(v, n)


def _rope_tables(pos, dm):
    half = dm.rn // 2
    freqs = ROPE_BASE ** (-jnp.arange(half, dtype=F32) / half)
    ang = pos.astype(F32)[:, None] * freqs[None, :]
    cos = jnp.cos(ang)
    sin = jnp.sin(ang)
    cos_h = jnp.concatenate([cos, cos], axis=1)
    sin_h = jnp.concatenate([-sin, sin], axis=1)
    return jnp.tile(cos_h, (1, dm.rh)), jnp.tile(sin_h, (1, dm.rh))


def _state_to_kernel_t(s):
    b, h, n, p = s.shape
    return s.reshape(b, h * n, p)


def _state_from_kernel_t(s, h):
    b, hn, p = s.shape
    return s.reshape(b, h, hn // h, p)


def _ssd_to_kernel(s):
    b, h, n, p = s.shape
    return jnp.transpose(s, (0, 2, 1, 3)).reshape(b, n, h * p)


def _ssd_from_kernel(s, h):
    b, n, hp = s.shape
    return jnp.transpose(s.reshape(b, n, h, hp // h), (0, 2, 1, 3))


def _tiles(seq):
    if seq % SCAN_CHUNK == 0:
        tt = 256 if seq % 256 == 0 else SCAN_CHUNK
        return dict(tt=tt, lg=SCAN_CHUNK, ls=min(tt, 128), lr=min(tt, 128))
    return dict(tt=seq, lg=seq, ls=seq, lr=seq)


def kernel(x_prompt, x_sample, state_gla, state_ssd, cache_conv, state_ret, norm_ffn1, ffn1_w_in, ffn1_w_out,
           norm_mix, w_in, gla_w_gate2, gla_b_gate, gla_norm, ssd_conv_w, ssd_conv_b, ssd_dt_bias, ssd_a_log,
           ssd_d, ssd_norm, ret_norm, w_branch_gla, w_branch_ssd, w_branch_ret, w_out, norm_ffn2, ffn2_w_in,
           ffn2_w_out, norm_final):
    depth = w_in.shape[0]
    d = x_prompt.shape[-1]
    d_ff = ffn1_w_out.shape[1]
    dm = _Dims(state_gla, state_ssd, cache_conv, state_ret, gla_w_gate2, ssd_conv_w, d)

    split_sizes = (dm.gh * dm.gn, dm.gh * dm.gn, dm.gh * dm.gp, dm.gh * dm.gp, dm.rank,
                   dm.s_inner, dm.conv_dim, dm.sh,
                   dm.rh * dm.rn, dm.rh * dm.rn, dm.rh * dm.rp, dm.rh * dm.rp, d, d, d)
    pts, acc = [], 0
    for s in split_sizes[:-1]:
        acc += s
        pts.append(acc)

    def mixer_weights(l):
        (s_gq, s_gk, s_gv, s_gr, s_glr, s_mz, s_mxbc, s_mdt,
         s_rq, s_rk, s_rv, s_rg, s_ga, s_gb, s_gc) = jnp.split(w_in[l], pts, axis=1)
        win = jnp.concatenate(
            [s_gq, s_gk, _pad_cols(s_glr, LANES), _pad_cols(s_mdt, LANES), s_gv, s_gr, s_mz, s_mxbc,
             s_rq, s_rk, s_rv, s_rg, s_ga, s_gb, s_gc], axis=1).astype(BF16)
        assert win.shape[1] == dm.in_cols
        wg2 = jnp.pad(gla_w_gate2[l], ((0, LANES - dm.rank), (0, 0))).astype(BF16)
        return [
            win, wg2, _row(gla_b_gate[l]), _row(gla_norm[l]),
            ssd_conv_w[l].astype(F32), _row(ssd_conv_b[l]), _row(ssd_dt_bias[l], LANES), _row(ssd_a_log[l], LANES),
            _row(jnp.repeat(ssd_d[l], dm.sp)), _row(ssd_norm[l]), _row(ret_norm[l]),
            w_branch_gla[l].astype(BF16), w_branch_ssd[l].astype(BF16), w_branch_ret[l].astype(BF16),
            w_out[l].astype(BF16),
        ]

    def ffn_weights(nrm, w_i, w_o, l):
        return _row(nrm[l]), w_i[l][:, :d_ff].astype(BF16), w_i[l][:, d_ff:].astype(BF16), w_o[l].astype(BF16)

    g_final = _row(norm_final)

    def run_group(x, pos, states):
        bsz, seq, _ = x.shape
        m = bsz * seq
        tm = 1024 if m % 1024 == 0 else m
        cos, sin = _rope_tables(pos, dm)
        tl = _tiles(seq)
        new_states = []
        for l in range(depth):
            sg0, ss0, cv0, sr0 = states(l)
            g1, wg1, wu1, wo1 = ffn_weights(norm_ffn1, ffn1_w_in, ffn1_w_out, l)
            x, h = _ffn(x.reshape(m, d), g1, wg1, wu1, wo1, _row(norm_mix[l]), tm=tm, fc=256, post="emit")
            x, sg, ss, cv, sr = _mixer(x.reshape(bsz, seq, d), h.reshape(bsz, seq, d), cos, sin,
                                       _state_to_kernel_t(sg0), _ssd_to_kernel(ss0), cv0,
                                       _state_to_kernel_t(sr0), mixer_weights(l), dm, **tl)
            g2, wg2_, wu2, wo2 = ffn_weights(norm_ffn2, ffn2_w_in, ffn2_w_out, l)
            last = l == depth - 1
            x = _ffn(x.reshape(m, d), g2, wg2_, wu2, wo2, g_final if last else None, tm=tm, fc=256,
                     post="norm" if last else None).reshape(bsz, seq, d)
            new_states.append((_state_from_kernel_t(sg, dm.gh), _ssd_from_kernel(ss, dm.sh), cv,
                               _state_from_kernel_t(sr, dm.rh)))
        return x, new_states

    bp, tp = x_prompt.shape[:2]
    ts = x_sample.shape[1]
    dt_ = x_prompt.dtype

    def zero_states(_):
        return (jnp.zeros((bp, dm.gh, dm.gn, dm.gp), dt_), jnp.zeros((bp, dm.sh, dm.sn, dm.sp), dt_),
                jnp.zeros((bp, dm.conv_k - 1, dm.conv_dim), dt_), jnp.zeros((bp, dm.rh, dm.rn, dm.rp), dt_))

    def carried_states(l):
        return state_gla[l], state_ssd[l], cache_conv[l], state_ret[l]

    y_p, st_p = run_group(x_prompt, jnp.arange(tp, dtype=jnp.int32), zero_states)
    y_s, st_s = run_group(x_sample, PAST_LEN + jnp.arange(ts, dtype=jnp.int32), carried_states)

    def stack(sts, i):
        return jnp.stack([s[i] for s in sts])

    return (y_p, y_s, stack(st_p, 0), stack(st_p, 1), stack(st_p, 2), stack(st_p, 3),
            stack(st_s, 0), stack(st_s, 1), stack(st_s, 2), stack(st_s, 3))
```

```python
import functools
import math

import jax
import jax.numpy as jnp
from jax import lax
from jax.experimental import pallas as pl
from jax.experimental.pallas import tpu as pltpu

F32 = jnp.float32
BF16 = jnp.bfloat16

EPS = 1e-6
PAST_LEN = 1024
GLA_TAU = 16.0
ROPE_BASE = 10000.0
SCAN_CHUNK = 64

LANES = 128
SUBLANES = 8
MXU_COLS = 256
FFN_OUT_ROWS = 256
VMEM_LIMIT_BYTES = 56 * 1024 * 1024


def _dot(a, b):
    return jnp.dot(a, b, preferred_element_type=F32)


def _dot_nt(a, b):
    return lax.dot_general(a, b, (((1,), (1,)), ((), ())), preferred_element_type=F32)


def _dot_tn(a, b):
    return lax.dot_general(a, b, (((0,), (0,)), ((), ())), preferred_element_type=F32)


def _split3(x):
    hi = x.astype(BF16)
    r1 = x - hi.astype(F32)
    mid = r1.astype(BF16)
    lo = (r1 - mid.astype(F32)).astype(BF16)
    return hi, mid, lo


def _dot_exact01(m01, x):
    hi, mid, lo = _split3(x)
    return _dot(m01, hi) + _dot(m01, mid) + _dot(m01, lo)


def _chunk_cumsum(x, L):
    T = x.shape[0]
    rows = MXU_COLS if (T % MXU_COLS == 0 and MXU_COLS % L == 0) else T
    tril = _block_tril(rows, L)
    out = [_dot_exact01(tril, x[i:i + rows]) for i in range(0, T, rows)]
    return out[0] if len(out) == 1 else jnp.concatenate(out, axis=0)


def _rms(x, g):
    return x * lax.rsqrt(jnp.mean(x * x, axis=-1, keepdims=True) + EPS) * g


def _silu(x):
    return x * jax.nn.sigmoid(x)


def _softplus(x):
    return jnp.maximum(x, 0.0) + jnp.log(1.0 + jnp.exp(-jnp.abs(x)))


def _log_sigmoid(x):
    return jnp.minimum(x, 0.0) - jnp.log(1.0 + jnp.exp(-jnp.abs(x)))


def _iota(shape, axis):
    return lax.broadcasted_iota(jnp.int32, shape, axis)


def _log2(n):
    k = int(math.log2(n))
    assert (1 << k) == n, f"{n} must be a power of two"
    return k


def _ffn_kernel(*refs, fc, post):
    if post is None:
        x_ref, g_ref, wg_ref, wu_ref, wo_ref, o_ref, h_ref, a_ref = refs
    elif post == "norm":
        x_ref, g_ref, wg_ref, wu_ref, wo_ref, g2_ref, o_ref, h_ref, a_ref = refs
    else:
        x_ref, g_ref, wg_ref, wu_ref, wo_ref, g2_ref, o_ref, hn_ref, h_ref, a_ref = refs
    x = x_ref[...]
    h_ref[...] = _rms(x, g_ref[...]).astype(BF16)
    d_ff = wg_ref.shape[1]
    for c in range(d_ff // fc):
        sl = slice(c * fc, (c + 1) * fc)
        gate = _dot(h_ref[...], wg_ref[:, sl])
        up = _dot(h_ref[...], wu_ref[:, sl])
        a_ref[:, sl] = (_silu(gate) * up).astype(BF16)
    tm = x_ref.shape[0]
    rb = FFN_OUT_ROWS if tm % FFN_OUT_ROWS == 0 else tm
    for i in range(tm // rb):
        r = slice(i * rb, (i + 1) * rb)
        y = x_ref[r, :] + 0.5 * _dot(a_ref[r, :], wo_ref[...])
        if post == "norm":
            y = _rms(y, g2_ref[...])
        o_ref[r, :] = y
        if post == "emit":
            hn_ref[r, :] = _rms(y, g2_ref[...]).astype(BF16)


def _const_spec(shape):
    nd = len(shape)
    return pl.BlockSpec(shape, lambda *_: (0,) * nd, pipeline_mode=pl.Buffered(1))


def _ffn(x2d, g, w_gate, w_up, w_out, g2, *, tm, fc, post):
    m, d = x2d.shape
    d_ff = w_gate.shape[1]
    assert m % tm == 0 and d_ff % fc == 0 and (g2 is None) == (post is None)
    kern = functools.partial(_ffn_kernel, fc=fc, post=post)
    row_spec = pl.BlockSpec((tm, d), lambda i: (i, 0))
    in_specs = [row_spec, _const_spec((1, d)), _const_spec((d, d_ff)), _const_spec((d, d_ff)),
                _const_spec((d_ff, d))]
    args = [x2d, g, w_gate, w_up, w_out]
    if post is not None:
        in_specs.append(_const_spec((1, d)))
        args.append(g2)
    out_specs, out_shape = row_spec, jax.ShapeDtypeStruct((m, d), F32)
    if post == "emit":
        out_specs, out_shape = [row_spec, row_spec], [out_shape, jax.ShapeDtypeStruct((m, d), BF16)]
    return pl.pallas_call(
        kern,
        grid=(m // tm,),
        in_specs=in_specs,
        out_specs=out_specs,
        out_shape=out_shape,
        scratch_shapes=[pltpu.VMEM((tm, d), BF16), pltpu.VMEM((tm, d_ff), BF16)],
        compiler_params=pltpu.CompilerParams(
            dimension_semantics=("arbitrary",), vmem_limit_bytes=VMEM_LIMIT_BYTES),
        name="ffn",
    )(*args)


class _Dims:
    def __init__(self, state_gla, state_ssd, cache_conv, state_ret, gla_w_gate2, ssd_conv_w, d_model):
        _, _, self.gh, self.gn, self.gp = state_gla.shape
        _, _, self.sh, self.sn, self.sp = state_ssd.shape
        _, _, self.rh, self.rn, self.rp = state_ret.shape
        self.conv_k = ssd_conv_w.shape[1]
        self.conv_dim = cache_conv.shape[-1]
        self.rank = gla_w_gate2.shape[1]
        self.d = d_model
        self.s_inner = self.sh * self.sp
        self.sg = (self.conv_dim - self.s_inner) // (2 * self.sn)
        assert self.gh * self.gn == 2 * LANES and self.gp == LANES
        assert self.rh * self.rn == 2 * LANES and self.rp == LANES
        assert self.sp * 2 == LANES and self.sg * self.sn == LANES and self.sh <= LANES
        assert self.rank <= LANES and self.conv_k - 1 <= SUBLANES
        w = [self.gh * self.gn, self.gh * self.gn, 2 * LANES, self.gh * self.gp, self.gh * self.gp,
             self.s_inner, self.conv_dim,
             self.rh * self.rn, self.rh * self.rn, self.rh * self.rp, self.rh * self.rp,
             d_model, d_model, d_model]
        self.seg_names = ["gq", "gk", "lrdt", "gv", "gr", "mz", "mxbc", "rq", "rk", "rv", "rg", "ga", "gb", "gc"]
        self.seg = {}
        off = 0
        for n, wd in zip(self.seg_names, w):
            assert wd % MXU_COLS == 0
            self.seg[n] = (off, off + wd)
            off += wd
        self.in_cols = off


def _head_lane_mask(shape, h, width):
    lane = _iota(shape, 1)
    return (lane >> _log2(width)) == h


def _stack_heads(x, nh, width):
    parts = [jnp.where(_head_lane_mask(x.shape, h, width), x, 0.0).astype(BF16) for h in range(nh)]
    return jnp.concatenate(parts, axis=0)


def _linear_attention_chunk(q_sc, k_sc, sc_mul, q_in, k_up, v, dec_row, dec_col, st_ref, nh, n, p, issue):
    L = q_sc.shape[0]
    k_big = _stack_heads(k_sc, nh, n)
    sc = _dot_nt(q_sc.astype(BF16), k_big)
    lane = _iota(sc.shape, 1)
    row = _iota(sc.shape, 0)
    causal = (lane & (L - 1)) <= row
    sc = jnp.where(causal, sc if sc_mul is None else sc * sc_mul, 0.0)
    v_big = _stack_heads(v, nh, p)
    issue()
    st = st_ref[...]
    rowhead = _iota(st.shape, 0) >> _log2(n)
    st_bd = jnp.concatenate([jnp.where(rowhead == h, st, 0.0).astype(BF16) for h in range(nh)], axis=1)
    o = _dot(sc.astype(BF16), v_big) + _dot(q_in.astype(BF16), st_bd)
    rows = [k_up]
    if dec_col is None:
        rows.append(jnp.broadcast_to(dec_row, (SUBLANES, nh * n)))
    pad = LANES - L - (len(rows) - 1) * SUBLANES
    assert pad >= 0
    if pad:
        rows.append(jnp.zeros((pad, nh * n), F32))
    kt = jnp.concatenate(rows, axis=0).T if len(rows) > 1 else k_up.T
    if dec_col is None:
        dec_col = jnp.sum(jnp.where(_iota(kt.shape, 1) == L, kt, 0.0), axis=-1, keepdims=True)
    kt = kt.astype(BF16)
    upd = []
    for h in range(nh):
        v_h = v[:, h * p:(h + 1) * p].astype(BF16)
        if L < LANES:
            v_h = jnp.concatenate([v_h, jnp.zeros((LANES - L, p), BF16)], axis=0)
        upd.append(_dot(kt[h * n:(h + 1) * n, :], v_h))
    st_ref[...] = st * dec_col + jnp.concatenate(upd, axis=0)
    return o


def _group_norm_heads(o, nh, p):
    outs = []
    for h in range(nh):
        oh = o[:, h * p:(h + 1) * p]
        mu = jnp.mean(oh, axis=-1, keepdims=True)
        dlt = oh - mu
        var = jnp.mean(dlt * dlt, axis=-1, keepdims=True)
        outs.append(dlt * lax.rsqrt(var + EPS))
    return jnp.concatenate(outs, axis=1)


def _block_tril(tt, L):
    row = _iota((tt, tt), 0)
    col = _iota((tt, tt), 1)
    k = _log2(L)
    return jnp.where((col <= row) & ((row >> k) == (col >> k)), 1.0, 0.0).astype(BF16)


def _ret_log_gamma(head, nh):
    out = jnp.full(head.shape, math.log1p(-2.0 ** (-5.0 - (nh - 1))), F32)
    for h in range(nh - 1):
        out = jnp.where(head == h, math.log1p(-2.0 ** (-5.0 - h)), out)
    return out


def _ssd_decay_terms(b8, dtc, dm):
    L = b8.shape[0]
    bl8 = b8[L - 1:L, :]
    w8 = dtc * jnp.exp(bl8 - b8)
    b8t = _transpose_rows(b8)
    dtt = _transpose_rows(dtc)
    lane8 = _iota(b8.shape, 1)
    ebl8 = jnp.exp(bl8)
    bcols, eb_cols, w_cols, dec_cols = [], [], [], []
    for hh in range(dm.sh):
        bcol = jnp.sum(jnp.where(lane8 == hh, b8, 0.0), axis=-1, keepdims=True)
        bcols.append(bcol)
        eb_cols.append(jnp.broadcast_to(jnp.exp(bcol), (L, LANES)))
        w_cols.append(jnp.broadcast_to(
            jnp.sum(jnp.where(lane8 == hh, w8, 0.0), axis=-1, keepdims=True), (L, LANES)))
        dec_cols.append(jnp.broadcast_to(
            jnp.sum(jnp.where(_iota(bl8.shape, 1) == hh, ebl8, 0.0), axis=-1, keepdims=True), (1, LANES)))

    def pair_lanes(cols):
        first = _iota(cols[0].shape, 1) < dm.sp
        return jnp.concatenate(
            [jnp.where(first, cols[2 * j], cols[2 * j + 1]) for j in range(dm.sh // 2)], axis=1)

    return bcols, b8t, dtt, pair_lanes(eb_cols), pair_lanes(w_cols), pair_lanes(dec_cols)


def _retention_tables(dm, L):
    nlane = dm.rh * dm.rn
    lgam_q = _ret_log_gamma(_iota((L, nlane), 1) >> _log2(dm.rn), dm.rh)
    pos_q = _iota((L, nlane), 0).astype(F32)
    q_dec = jnp.exp((pos_q + 1.0) * lgam_q)
    k_dec = jnp.exp((L - 1.0 - pos_q) * lgam_q)
    lgam_r = _ret_log_gamma(_iota((nlane, dm.rp), 0) >> _log2(dm.rn), dm.rh)
    st_dec = jnp.exp(L * lgam_r)
    lgam_s = _ret_log_gamma(_iota((L, dm.rh * L), 1) >> _log2(L), dm.rh)
    dpos = (_iota((L, dm.rh * L), 0) - (_iota((L, dm.rh * L), 1) & (L - 1))).astype(F32)
    sc_dec = jnp.exp(jnp.maximum(dpos, 0.0) * lgam_s)
    return q_dec, k_dec, st_dec, sc_dec


def _transpose_rows(x):
    L = x.shape[0]
    if L % LANES:
        pad = LANES - L % LANES
        x = jnp.concatenate([x, jnp.zeros((pad, x.shape[1]), x.dtype)], axis=0)
    return x.T[:, :L]


def _swap_halves(x, n):
    width = x.shape[1]
    half = n // 2
    first = (_iota(x.shape, 1) & (n - 1)) < half
    return jnp.where(first, pltpu.roll(x, width - half, 1), pltpu.roll(x, half, 1))


def _mixer_kernel(x_ref, h_ref, cos_ref, sin_ref, sg0_ref, ss0_ref, cv0_ref, sr0_ref,
                  win_ref, wg2_ref, bg_ref, gnorm_ref, cw_ref, cb_ref, dtb_ref, alog_ref,
                  dexp_ref, snorm_ref, rnorm_ref, wbg_ref, wbs_ref, wbr_ref, wout_ref,
                  y_ref, sg_ref, ss_ref, cv_ref, sr_ref,
                  stg, sts, cbuf, strt, ogla, ossd, oret,
                  *, dm, tt, lg, ls, lr):
    t = pl.program_id(1)
    nt = pl.num_programs(1)
    ck = dm.conv_k - 1

    @pl.when(t == 0)
    def _():
        stg[...] = sg0_ref[...]
        strt[...] = sr0_ref[...]
        z0 = ss0_ref[...]
        hpg = dm.sh // dm.sg
        sts[...] = jnp.concatenate(
            [jnp.where(_head_lane_mask(z0.shape, g, dm.sp * hpg), z0, 0.0) for g in range(dm.sg)], axis=0)
        cbuf[SUBLANES - ck:SUBLANES, :] = cv0_ref[...]

    pending = []
    for name in ["gq", "gk", "lrdt", "gv", "gr", "mxbc", "mz", "rq", "rk", "rv", "rg", "ga", "gb", "gc"]:
        a, b = dm.seg[name]
        pending += [(name, s) for s in range(a, b, MXU_COLS)]
    parts = {name: [] for name in dm.seg_names}

    def issue(n=1):
        for _ in range(n):
            if pending:
                name, s = pending.pop(0)
                parts[name].append(_dot(h_ref[...], win_ref[:, s:s + MXU_COLS]))

    every = max(1, tt // MXU_COLS)
    calls = [0]

    def issue_thin(n=1):
        for _ in range(n):
            calls[0] += 1
            if calls[0] % every == 0:
                issue()

    def take(name):
        a, b = dm.seg[name]
        while len(parts[name]) * MXU_COLS < b - a:
            issue()
        return parts[name][0] if len(parts[name]) == 1 else jnp.concatenate(parts[name], axis=1)

    q = take("gq") * (dm.gn ** -0.5)
    k = take("gk")
    lrdt = take("lrdt")
    ret_tabs = _retention_tables(dm, lr)
    gate = _dot(lrdt[:, :LANES].astype(BF16), wg2_ref[...]) + bg_ref[...]
    dt8 = _softplus(lrdt[:, LANES:] + dtb_ref[...])
    a8 = dt8 * (-jnp.exp(alog_ref[...]))
    b8_all = _chunk_cumsum(a8, ls)
    issue(3)
    log_a = _log_sigmoid(gate) * (1.0 / GLA_TAU)
    b_all = _chunk_cumsum(log_a, lg)
    issue(3)
    qd_all = q * jnp.exp(b_all)
    kd_all = k * jnp.exp(-b_all)
    v = take("gv")
    for c in range(tt // lg):
        r = slice(c * lg, (c + 1) * lg)
        b = b_all[r]
        bl = b[lg - 1:lg, :]
        ks = k[r] * jnp.exp(bl - b)
        ogla[r, :] = _linear_attention_chunk(qd_all[r], kd_all[r], None, qd_all[r], ks, v[r],
                                             jnp.exp(bl), None, stg, dm.gh, dm.gn, dm.gp,
                                             issue_thin if c else (lambda: None))
    y_gla = (_group_norm_heads(ogla[...], dm.gh, dm.gp) * gnorm_ref[...] * _silu(take("gr"))).astype(BF16)
    issue(2)

    xbc_pre = take("mxbc")
    cbuf[SUBLANES:SUBLANES + tt, :] = xbc_pre
    conv = cb_ref[...] + cw_ref[ck:ck + 1, :] * xbc_pre
    for j in range(ck):
        conv = conv + cw_ref[j:j + 1, :] * cbuf[SUBLANES - ck + j:SUBLANES - ck + j + tt, :]
    new_carry = cbuf[SUBLANES + tt - ck:SUBLANES + tt, :]
    cbuf[SUBLANES - ck:SUBLANES, :] = new_carry
    xbc = _silu(conv)
    xs = xbc[:, :dm.s_inner]
    bm = xbc[:, dm.s_inner:dm.s_inner + LANES]
    cm = xbc[:, dm.s_inner + LANES:dm.s_inner + 2 * LANES]
    hpg = dm.sh // dm.sg
    for c in range(tt // ls):
        r = slice(c * ls, (c + 1) * ls)
        bcols, b8t, dtt, eb_exp, w_exp, dec_row = _ssd_decay_terms(b8_all[r], dt8[r], dm)
        xs_c = xs[r]
        xs_bf = xs_c.astype(BF16)
        cm_c = cm[r]
        bm_c = bm[r].astype(BF16)
        causal = _iota((ls, ls), 1) <= _iota((ls, ls), 0)
        gmats = [_dot_nt(jnp.where(_head_lane_mask(cm_c.shape, g, dm.sn), cm_c, 0.0).astype(BF16), bm_c)
                 for g in range(dm.sg)]
        st = sts[...]
        y_inter = _dot(cm_c.astype(BF16), st.astype(BF16)) * eb_exp
        upd = _dot_tn(bm_c, (xs_c * w_exp).astype(BF16))
        blk = (_iota(upd.shape, 0) >> _log2(dm.sn)) == (_iota(upd.shape, 1) >> _log2(dm.sp * hpg))
        sts[...] = st * dec_row + jnp.where(blk, upd, 0.0)
        issue_thin(2)
        y_pairs = []
        for hh in range(dm.sh):
            seg = jnp.where(causal, bcols[hh] - b8t[hh:hh + 1, :], -jnp.inf)
            m_h = gmats[hh // hpg] * dtt[hh:hh + 1, :] * jnp.exp(seg)
            pr = hh // 2
            y_pairs.append(_dot(m_h.astype(BF16), xs_bf[:, pr * LANES:(pr + 1) * LANES]))
            if hh % 2:
                issue_thin()
        first = _iota((ls, LANES), 1) < dm.sp
        y_intra = jnp.concatenate(
            [jnp.where(first, y_pairs[2 * j], y_pairs[2 * j + 1]) for j in range(dm.sh // 2)], axis=1)
        ossd[r, :] = y_intra + y_inter
    y = ossd[...] + dexp_ref[...] * xs
    y = y * _silu(take("mz"))
    issue()
    y_ssd = _rms(y, snorm_ref[...]).astype(BF16)

    late = [(y_gla, wbg_ref, s) for s in range(0, dm.d, MXU_COLS)]
    late += [(y_ssd, wbs_ref, s) for s in range(0, dm.d, MXU_COLS)]
    yb_parts = []

    def issue_late(n=1):
        for _ in range(n):
            if late:
                yv, w_ref, s = late.pop(0)
                yb_parts.append(_dot(yv, w_ref[:, s:s + MXU_COLS]))

    cos = cos_ref[...]
    sin = sin_ref[...]
    rq_pre = take("rq")
    rk_pre = take("rk")
    issue(len(pending))
    rq = rq_pre * cos + _swap_halves(rq_pre, dm.rn) * sin
    issue_late()
    rk = (rk_pre * cos + _swap_halves(rk_pre, dm.rn) * sin) * (dm.rn ** -0.5)
    issue_late()
    rv = take("rv")
    q_dec, k_dec, st_dec, sc_dec = ret_tabs
    for c in range(tt // lr):
        r = slice(c * lr, (c + 1) * lr)
        oret[r, :] = _linear_attention_chunk(rq[r], rk[r], sc_dec, rq[r] * q_dec, rk[r] * k_dec, rv[r],
                                             None, st_dec, strt, dm.rh, dm.rn, dm.rp, issue_late)
        issue_late()
    issue_late(len(late))
    ncol = dm.d // MXU_COLS
    yb_gla = jnp.concatenate(yb_parts[:ncol], axis=1)
    yb_ssd = jnp.concatenate(yb_parts[ncol:], axis=1)
    ga, gb, gc = take("ga"), take("gb"), take("gc")
    m2 = jax.nn.sigmoid(ga) * yb_gla + jax.nn.sigmoid(gb) * yb_ssd
    y_ret = (_group_norm_heads(oret[...], dm.rh, dm.rp) * rnorm_ref[...] * _silu(take("rg"))).astype(BF16)

    halves = 2 if tt % (2 * SUBLANES * 2) == 0 else 1
    hr = tt // halves
    for i in range(halves):
        r = slice(i * hr, (i + 1) * hr)
        m = m2[r] + jax.nn.sigmoid(gc[r]) * _dot(y_ret[r], wbr_ref[...])
        y_ref[r, :] = x_ref[r, :] + _dot(m.astype(BF16), wout_ref[...])

    @pl.when(t == nt - 1)
    def _():
        sg_ref[...] = stg[...]
        sr_ref[...] = strt[...]
        sz = sts[...]
        acc = sz[0:dm.sn]
        for g in range(1, dm.sg):
            acc = acc + sz[g * dm.sn:(g + 1) * dm.sn]
        ss_ref[...] = acc
        cv_ref[...] = cbuf[SUBLANES - ck:SUBLANES, :]


def _mixer(x, h, cos, sin, sg0, ss0, cv0, sr0, wts, dm, *, tt, lg, ls, lr):
    bsz, seq, d = x.shape
    assert seq % tt == 0 and tt % lg == 0 and tt % ls == 0 and tt % lr == 0
    kern = functools.partial(_mixer_kernel, dm=dm, tt=tt, lg=lg, ls=ls, lr=lr)
    gq = dm.gh * dm.gn
    rq = dm.rh * dm.rn

    def per_b(shape):
        nd = len(shape)
        return pl.BlockSpec((None,) + shape, lambda b, t: (b,) + (0,) * nd)

    in_specs = [
        pl.BlockSpec((None, tt, d), lambda b, t: (b, t, 0)),
        pl.BlockSpec((None, tt, d), lambda b, t: (b, t, 0)),
        pl.BlockSpec((tt, rq), lambda b, t: (t, 0)),
        pl.BlockSpec((tt, rq), lambda b, t: (t, 0)),
        per_b((gq, dm.gp)), per_b((dm.sn, dm.s_inner)), per_b((dm.conv_k - 1, dm.conv_dim)), per_b((rq, dm.rp)),
    ] + [_const_spec(w.shape) for w in wts]
    out_specs = [
        pl.BlockSpec((None, tt, d), lambda b, t: (b, t, 0)),
        per_b((gq, dm.gp)), per_b((dm.sn, dm.s_inner)), per_b((dm.conv_k - 1, dm.conv_dim)), per_b((rq, dm.rp)),
    ]
    out_shape = [
        jax.ShapeDtypeStruct((bsz, seq, d), F32),
        jax.ShapeDtypeStruct((bsz, gq, dm.gp), F32),
        jax.ShapeDtypeStruct((bsz, dm.sn, dm.s_inner), F32),
        jax.ShapeDtypeStruct((bsz, dm.conv_k - 1, dm.conv_dim), F32),
        jax.ShapeDtypeStruct((bsz, rq, dm.rp), F32),
    ]
    scratch = [
        pltpu.VMEM((gq, dm.gp), F32),
        pltpu.VMEM((dm.sg * dm.sn, dm.s_inner), F32),
        pltpu.VMEM((tt + SUBLANES, dm.conv_dim), F32),
        pltpu.VMEM((rq, dm.rp), F32),
        pltpu.VMEM((tt, dm.gh * dm.gp), F32),
        pltpu.VMEM((tt, dm.s_inner), F32),
        pltpu.VMEM((tt, dm.rh * dm.rp), F32),
    ]
    return pl.pallas_call(
        kern,
        grid=(bsz, seq // tt),
        in_specs=in_specs,
        out_specs=out_specs,
        out_shape=out_shape,
        scratch_shapes=scratch,
        compiler_params=pltpu.CompilerParams(
            dimension_semantics=("arbitrary", "arbitrary"), vmem_limit_bytes=VMEM_LIMIT_BYTES),
        name="mixer",
    )(x, h, cos, sin, sg0, ss0, cv0, sr0, *wts)


def _pad_cols(w, n):
    return jnp.pad(w, ((0, 0), (0, n - w.shape[1])))


def _row(v, n=None):
    v = v.reshape(1, -1).astype(F32)
    return v if n is None else _pad_cols(v, n)


def _rope_tables(pos, dm):
    half = dm.rn // 2
    freqs = ROPE_BASE ** (-jnp.arange(half, dtype=F32) / half)
    ang = pos.astype(F32)[:, None] * freqs[None, :]
    cos = jnp.cos(ang)
    sin = jnp.sin(ang)
    cos_h = jnp.concatenate([cos, cos], axis=1)
    sin_h = jnp.concatenate([-sin, sin], axis=1)
    return jnp.tile(cos_h, (1, dm.rh)), jnp.tile(sin_h, (1, dm.rh))


def _state_to_kernel_t(s):
    b, h, n, p = s.shape
    return s.reshape(b, h * n, p)


def _state_from_kernel_t(s, h):
    b, hn, p = s.shape
    return s.reshape(b, h, hn // h, p)


def _ssd_to_kernel(s):
    b, h, n, p = s.shape
    return jnp.transpose(s, (0, 2, 1, 3)).reshape(b, n, h * p)


def _ssd_from_kernel(s, h):
    b, n, hp = s.shape
    return jnp.transpose(s.reshape(b, n, h, hp // h), (0, 2, 1, 3))


def _tiles(seq):
    if seq % SCAN_CHUNK == 0:
        tt = 512 if seq % 512 == 0 else SCAN_CHUNK
        return dict(tt=tt, lg=SCAN_CHUNK, ls=min(tt, 128), lr=min(tt, 128))
    return dict(tt=seq, lg=seq, ls=seq, lr=seq)


def kernel(x_prompt, x_sample, state_gla, state_ssd, cache_conv, state_ret, norm_ffn1, ffn1_w_in, ffn1_w_out,
           norm_mix, w_in, gla_w_gate2, gla_b_gate, gla_norm, ssd_conv_w, ssd_conv_b, ssd_dt_bias, ssd_a_log,
           ssd_d, ssd_norm, ret_norm, w_branch_gla, w_branch_ssd, w_branch_ret, w_out, norm_ffn2, ffn2_w_in,
           ffn2_w_out, norm_final):
    depth = w_in.shape[0]
    d = x_prompt.shape[-1]
    d_ff = ffn1_w_out.shape[1]
    dm = _Dims(state_gla, state_ssd, cache_conv, state_ret, gla_w_gate2, ssd_conv_w, d)

    split_sizes = (dm.gh * dm.gn, dm.gh * dm.gn, dm.gh * dm.gp, dm.gh * dm.gp, dm.rank,
                   dm.s_inner, dm.conv_dim, dm.sh,
                   dm.rh * dm.rn, dm.rh * dm.rn, dm.rh * dm.rp, dm.rh * dm.rp, d, d, d)
    pts, acc = [], 0
    for s in split_sizes[:-1]:
        acc += s
        pts.append(acc)

    def mixer_weights(l):
        (s_gq, s_gk, s_gv, s_gr, s_glr, s_mz, s_mxbc, s_mdt,
         s_rq, s_rk, s_rv, s_rg, s_ga, s_gb, s_gc) = jnp.split(w_in[l], pts, axis=1)
        win = jnp.concatenate(
            [s_gq, s_gk, _pad_cols(s_glr, LANES), _pad_cols(s_mdt, LANES), s_gv, s_gr, s_mz, s_mxbc,
             s_rq, s_rk, s_rv, s_rg, s_ga, s_gb, s_gc], axis=1).astype(BF16)
        assert win.shape[1] == dm.in_cols
        wg2 = jnp.pad(gla_w_gate2[l], ((0, LANES - dm.rank), (0, 0))).astype(BF16)
        return [
            win, wg2, _row(gla_b_gate[l]), _row(gla_norm[l]),
            ssd_conv_w[l].astype(F32), _row(ssd_conv_b[l]), _row(ssd_dt_bias[l], LANES), _row(ssd_a_log[l], LANES),
            _row(jnp.repeat(ssd_d[l], dm.sp)), _row(ssd_norm[l]), _row(ret_norm[l]),
            w_branch_gla[l].astype(BF16), w_branch_ssd[l].astype(BF16), w_branch_ret[l].astype(BF16),
            w_out[l].astype(BF16),
        ]

    def ffn_weights(nrm, w_i, w_o, l):
        return _row(nrm[l]), w_i[l][:, :d_ff].astype(BF16), w_i[l][:, d_ff:].astype(BF16), w_o[l].astype(BF16)

    g_final = _row(norm_final)

    def run_group(x, pos, states):
        bsz, seq, _ = x.shape
        m = bsz * seq
        tm = 1024 if m % 1024 == 0 else m
        cos, sin = _rope_tables(pos, dm)
        tl = _tiles(seq)
        new_states = []
        for l in range(depth):
            sg0, ss0, cv0, sr0 = states(l)
            g1, wg1, wu1, wo1 = ffn_weights(norm_ffn1, ffn1_w_in, ffn1_w_out, l)
            x, h = _ffn(x.reshape(m, d), g1, wg1, wu1, wo1, _row(norm_mix[l]), tm=tm, fc=256, post="emit")
            x, sg, ss, cv, sr = _mixer(x.reshape(bsz, seq, d), h.reshape(bsz, seq, d), cos, sin,
                                       _state_to_kernel_t(sg0), _ssd_to_kernel(ss0), cv0,
                                       _state_to_kernel_t(sr0), mixer_weights(l), dm, **tl)
            g2, wg2_, wu2, wo2 = ffn_weights(norm_ffn2, ffn2_w_in, ffn2_w_out, l)
            last = l == depth - 1
            x = _ffn(x.reshape(m, d), g2, wg2_, wu2, wo2, g_final if last else None, tm=tm, fc=256,
                     post="norm" if last else None).reshape(bsz, seq, d)
            new_states.append((_state_from_kernel_t(sg, dm.gh), _ssd_from_kernel(ss, dm.sh), cv,
                               _state_from_kernel_t(sr, dm.rh)))
        return x, new_states

    bp, tp = x_prompt.shape[:2]
    ts = x_sample.shape[1]
    dt_ = x_prompt.dtype

    def zero_states(_):
        return (jnp.zeros((bp, dm.gh, dm.gn, dm.gp), dt_), jnp.zeros((bp, dm.sh, dm.sn, dm.sp), dt_),
                jnp.zeros((bp, dm.conv_k - 1, dm.conv_dim), dt_), jnp.zeros((bp, dm.rh, dm.rn, dm.rp), dt_))

    def carried_states(l):
        return state_gla[l], state_ssd[l], cache_conv[l], state_ret[l]

    y_p, st_p = run_group(x_prompt, jnp.arange(tp, dtype=jnp.int32), zero_states)
    y_s, st_s = run_group(x_sample, PAST_LEN + jnp.arange(ts, dtype=jnp.int32), carried_states)

    def stack(sts, i):
        return jnp.stack([s[i] for s in sts])

    return (y_p, y_s, stack(st_p, 0), stack(st_p, 1), stack(st_p, 2), stack(st_p, 3),
            stack(st_s, 0), stack(st_s, 1), stack(st_s, 2), stack(st_s, 3))
```

```python
import functools
import math

import jax
import jax.numpy as jnp
from jax import lax
from jax.experimental import pallas as pl
from jax.experimental.pallas import tpu as pltpu

F32 = jnp.float32
BF16 = jnp.bfloat16

EPS = 1e-6
PAST_LEN = 1024
GLA_TAU = 16.0
ROPE_BASE = 10000.0
SCAN_CHUNK = 64

LANES = 128
SUBLANES = 8
MXU_COLS = 256
FFN_OUT_ROWS = 256
VMEM_LIMIT_BYTES = 56 * 1024 * 1024


def _dot(a, b):
    return jnp.dot(a, b, preferred_element_type=F32)


def _dot_nt(a, b):
    return lax.dot_general(a, b, (((1,), (1,)), ((), ())), preferred_element_type=F32)


def _dot_tn(a, b):
    return lax.dot_general(a, b, (((0,), (0,)), ((), ())), preferred_element_type=F32)


def _split3(x):
    hi = x.astype(BF16)
    r1 = x - hi.astype(F32)
    mid = r1.astype(BF16)
    lo = (r1 - mid.astype(F32)).astype(BF16)
    return hi, mid, lo


def _dot_exact01(m01, x):
    hi, mid, lo = _split3(x)
    return _dot(m01, hi) + _dot(m01, mid) + _dot(m01, lo)


def _chunk_cumsum(x, L):
    T = x.shape[0]
    rows = MXU_COLS if (T % MXU_COLS == 0 and MXU_COLS % L == 0) else T
    tril = _block_tril(rows, L)
    out = [_dot_exact01(tril, x[i:i + rows]) for i in range(0, T, rows)]
    return out[0] if len(out) == 1 else jnp.concatenate(out, axis=0)


def _rms(x, g):
    return x * lax.rsqrt(jnp.mean(x * x, axis=-1, keepdims=True) + EPS) * g


def _silu(x):
    return x * jax.nn.sigmoid(x)


def _softplus(x):
    return jnp.maximum(x, 0.0) + jnp.log(1.0 + jnp.exp(-jnp.abs(x)))


def _log_sigmoid(x):
    return jnp.minimum(x, 0.0) - jnp.log(1.0 + jnp.exp(-jnp.abs(x)))


def _iota(shape, axis):
    return lax.broadcasted_iota(jnp.int32, shape, axis)


def _log2(n):
    k = int(math.log2(n))
    assert (1 << k) == n, f"{n} must be a power of two"
    return k


def _ffn_kernel(*refs, fc, post):
    if post is None:
        x_ref, g_ref, wg_ref, wu_ref, wo_ref, o_ref, h_ref, a_ref = refs
    elif post == "norm":
        x_ref, g_ref, wg_ref, wu_ref, wo_ref, g2_ref, o_ref, h_ref, a_ref = refs
    else:
        x_ref, g_ref, wg_ref, wu_ref, wo_ref, g2_ref, o_ref, hn_ref, h_ref, a_ref = refs
    x = x_ref[...]
    h_ref[...] = _rms(x, g_ref[...]).astype(BF16)
    d_ff = wg_ref.shape[1]
    for c in range(d_ff // fc):
        sl = slice(c * fc, (c + 1) * fc)
        gate = _dot(h_ref[...], wg_ref[:, sl])
        up = _dot(h_ref[...], wu_ref[:, sl])
        a_ref[:, sl] = (_silu(gate) * up).astype(BF16)
    tm = x_ref.shape[0]
    rb = FFN_OUT_ROWS if tm % FFN_OUT_ROWS == 0 else tm
    for i in range(tm // rb):
        r = slice(i * rb, (i + 1) * rb)
        y = x_ref[r, :] + 0.5 * _dot(a_ref[r, :], wo_ref[...])
        if post == "norm":
            y = _rms(y, g2_ref[...])
        o_ref[r, :] = y
        if post == "emit":
            hn_ref[r, :] = _rms(y, g2_ref[...]).astype(BF16)


def _const_spec(shape):
    nd = len(shape)
    return pl.BlockSpec(shape, lambda *_: (0,) * nd, pipeline_mode=pl.Buffered(1))


def _ffn(x2d, g, w_gate, w_up, w_out, g2, *, tm, fc, post):
    m, d = x2d.shape
    d_ff = w_gate.shape[1]
    assert m % tm == 0 and d_ff % fc == 0 and (g2 is None) == (post is None)
    kern = functools.partial(_ffn_kernel, fc=fc, post=post)
    row_spec = pl.BlockSpec((tm, d), lambda i: (i, 0))
    in_specs = [row_spec, _const_spec((1, d)), _const_spec((d, d_ff)), _const_spec((d, d_ff)),
                _const_spec((d_ff, d))]
    args = [x2d, g, w_gate, w_up, w_out]
    if post is not None:
        in_specs.append(_const_spec((1, d)))
        args.append(g2)
    out_specs, out_shape = row_spec, jax.ShapeDtypeStruct((m, d), F32)
    if post == "emit":
        out_specs, out_shape = [row_spec, row_spec], [out_shape, jax.ShapeDtypeStruct((m, d), BF16)]
    return pl.pallas_call(
        kern,
        grid=(m // tm,),
        in_specs=in_specs,
        out_specs=out_specs,
        out_shape=out_shape,
        scratch_shapes=[pltpu.VMEM((tm, d), BF16), pltpu.VMEM((tm, d_ff), BF16)],
        compiler_params=pltpu.CompilerParams(
            dimension_semantics=("arbitrary",), vmem_limit_bytes=VMEM_LIMIT_BYTES),
        name="ffn",
    )(*args)


class _Dims:
    def __init__(self, state_gla, state_ssd, cache_conv, state_ret, gla_w_gate2, ssd_conv_w, d_model):
        _, _, self.gh, self.gn, self.gp = state_gla.shape
        _, _, self.sh, self.sn, self.sp = state_ssd.shape
        _, _, self.rh, self.rn, self.rp = state_ret.shape
        self.conv_k = ssd_conv_w.shape[1]
        self.conv_dim = cache_conv.shape[-1]
        self.rank = gla_w_gate2.shape[1]
        self.d = d_model
        self.s_inner = self.sh * self.sp
        self.sg = (self.conv_dim - self.s_inner) // (2 * self.sn)
        assert self.gh * self.gn == 2 * LANES and self.gp == LANES
        assert self.rh * self.rn == 2 * LANES and self.rp == LANES
        assert self.sp * 2 == LANES and self.sg * self.sn == LANES and self.sh <= LANES
        assert self.rank <= LANES and self.conv_k - 1 <= SUBLANES
        w = [self.gh * self.gn, self.gh * self.gn, 2 * LANES, self.gh * self.gp, self.gh * self.gp,
             self.s_inner, self.conv_dim,
             self.rh * self.rn, self.rh * self.rn, self.rh * self.rp, self.rh * self.rp,
             d_model, d_model, d_model]
        self.seg_names = ["gq", "gk", "lrdt", "gv", "gr", "mz", "mxbc", "rq", "rk", "rv", "rg", "ga", "gb", "gc"]
        self.seg = {}
        off = 0
        for n, wd in zip(self.seg_names, w):
            assert wd % MXU_COLS == 0
            self.seg[n] = (off, off + wd)
            off += wd
        self.in_cols = off


def _head_lane_mask(shape, h, width):
    lane = _iota(shape, 1)
    return (lane >> _log2(width)) == h


def _stack_heads(x, nh, width):
    parts = [jnp.where(_head_lane_mask(x.shape, h, width), x, 0.0).astype(BF16) for h in range(nh)]
    return jnp.concatenate(parts, axis=0)


def _linear_attention_chunk(q_sc, k_sc, sc_mul, q_in, k_up, v, dec_row, dec_col, st_ref, nh, n, p, issue):
    L = q_sc.shape[0]
    k_big = _stack_heads(k_sc, nh, n)
    sc = _dot_nt(q_sc.astype(BF16), k_big)
    lane = _iota(sc.shape, 1)
    row = _iota(sc.shape, 0)
    causal = (lane & (L - 1)) <= row
    sc = jnp.where(causal, sc if sc_mul is None else sc * sc_mul, 0.0)
    sc = sc.astype(BF16)
    if L % LANES == 0:
        intra = jnp.concatenate(
            [_dot(sc[:, h * L:(h + 1) * L], v[:, h * p:(h + 1) * p].astype(BF16)) for h in range(nh)], axis=1)
    else:
        intra = _dot(sc, _stack_heads(v, nh, p))
    issue()
    st = st_ref[...]
    rowhead = _iota(st.shape, 0) >> _log2(n)
    st_bd = jnp.concatenate([jnp.where(rowhead == h, st, 0.0).astype(BF16) for h in range(nh)], axis=1)
    o = intra + _dot(q_in.astype(BF16), st_bd)
    rows = [k_up]
    if dec_col is None:
        rows.append(jnp.broadcast_to(dec_row, (SUBLANES, nh * n)))
    pad = LANES - L - (len(rows) - 1) * SUBLANES
    assert pad >= 0
    if pad:
        rows.append(jnp.zeros((pad, nh * n), F32))
    kt = jnp.concatenate(rows, axis=0).T if len(rows) > 1 else k_up.T
    if dec_col is None:
        dec_col = kt[:, L:L + 1]
    kt = kt.astype(BF16)
    upd = []
    for h in range(nh):
        v_h = v[:, h * p:(h + 1) * p].astype(BF16)
        if L < LANES:
            v_h = jnp.concatenate([v_h, jnp.zeros((LANES - L, p), BF16)], axis=0)
        upd.append(_dot(kt[h * n:(h + 1) * n, :], v_h))
    st_ref[...] = st * dec_col + jnp.concatenate(upd, axis=0)
    return o


def _group_norm_heads(o, nh, p):
    outs = []
    for h in range(nh):
        oh = o[:, h * p:(h + 1) * p]
        mu = jnp.mean(oh, axis=-1, keepdims=True)
        dlt = oh - mu
        var = jnp.mean(dlt * dlt, axis=-1, keepdims=True)
        outs.append(dlt * lax.rsqrt(var + EPS))
    return jnp.concatenate(outs, axis=1)


def _block_tril(tt, L):
    row = _iota((tt, tt), 0)
    col = _iota((tt, tt), 1)
    k = _log2(L)
    return jnp.where((col <= row) & ((row >> k) == (col >> k)), 1.0, 0.0).astype(BF16)


def _ret_log_gamma(head, nh):
    out = jnp.full(head.shape, math.log1p(-2.0 ** (-5.0 - (nh - 1))), F32)
    for h in range(nh - 1):
        out = jnp.where(head == h, math.log1p(-2.0 ** (-5.0 - h)), out)
    return out


def _ssd_decay_terms(b8, dtc, dm):
    L = b8.shape[0]
    bl8 = b8[L - 1:L, :]
    w8 = dtc * jnp.exp(bl8 - b8)
    b8t = _transpose_rows(b8)
    dtt = _transpose_rows(dtc)
    lane8 = _iota(b8.shape, 1)
    ebl8 = jnp.exp(bl8)
    bcols, eb_cols, w_cols, dec_cols = [], [], [], []
    for hh in range(dm.sh):
        bcol = jnp.sum(jnp.where(lane8 == hh, b8, 0.0), axis=-1, keepdims=True)
        bcols.append(bcol)
        eb_cols.append(jnp.broadcast_to(jnp.exp(bcol), (L, LANES)))
        w_cols.append(jnp.broadcast_to(
            jnp.sum(jnp.where(lane8 == hh, w8, 0.0), axis=-1, keepdims=True), (L, LANES)))
        dec_cols.append(jnp.broadcast_to(
            jnp.sum(jnp.where(_iota(bl8.shape, 1) == hh, ebl8, 0.0), axis=-1, keepdims=True), (1, LANES)))

    def pair_lanes(cols):
        first = _iota(cols[0].shape, 1) < dm.sp
        return jnp.concatenate(
            [jnp.where(first, cols[2 * j], cols[2 * j + 1]) for j in range(dm.sh // 2)], axis=1)

    return bcols, b8t, dtt, pair_lanes(eb_cols), pair_lanes(w_cols), pair_lanes(dec_cols)


def _retention_tables(dm, L):
    nlane = dm.rh * dm.rn
    lgam_q = _ret_log_gamma(_iota((L, nlane), 1) >> _log2(dm.rn), dm.rh)
    pos_q = _iota((L, nlane), 0).astype(F32)
    q_dec = jnp.exp((pos_q + 1.0) * lgam_q)
    k_dec = jnp.exp((L - 1.0 - pos_q) * lgam_q)
    lgam_r = _ret_log_gamma(_iota((nlane, dm.rp), 0) >> _log2(dm.rn), dm.rh)
    st_dec = jnp.exp(L * lgam_r)
    lgam_s = _ret_log_gamma(_iota((L, dm.rh * L), 1) >> _log2(L), dm.rh)
    dpos = (_iota((L, dm.rh * L), 0) - (_iota((L, dm.rh * L), 1) & (L - 1))).astype(F32)
    sc_dec = jnp.exp(jnp.maximum(dpos, 0.0) * lgam_s)
    return q_dec, k_dec, st_dec, sc_dec


def _transpose_rows(x):
    L = x.shape[0]
    if L % LANES:
        pad = LANES - L % LANES
        x = jnp.concatenate([x, jnp.zeros((pad, x.shape[1]), x.dtype)], axis=0)
    return x.T[:, :L]


def _swap_halves(x, n):
    width = x.shape[1]
    half = n // 2
    first = (_iota(x.shape, 1) & (n - 1)) < half
    return jnp.where(first, pltpu.roll(x, width - half, 1), pltpu.roll(x, half, 1))


def _mixer_kernel(x_ref, h_ref, cos_ref, sin_ref, sg0_ref, ss0_ref, cv0_ref, sr0_ref,
                  win_ref, wg2_ref, bg_ref, gnorm_ref, cw_ref, cb_ref, dtb_ref, alog_ref,
                  dexp_ref, snorm_ref, rnorm_ref, wbg_ref, wbs_ref, wbr_ref, wout_ref,
                  y_ref, sg_ref, ss_ref, cv_ref, sr_ref,
                  stg, sts, cbuf, strt, ogla, ossd, oret,
                  *, dm, tt, lg, ls, lr):
    t = pl.program_id(1)
    nt = pl.num_programs(1)
    ck = dm.conv_k - 1

    @pl.when(t == 0)
    def _():
        stg[...] = sg0_ref[...]
        strt[...] = sr0_ref[...]
        z0 = ss0_ref[...]
        hpg = dm.sh // dm.sg
        sts[...] = jnp.concatenate(
            [jnp.where(_head_lane_mask(z0.shape, g, dm.sp * hpg), z0, 0.0) for g in range(dm.sg)], axis=0)
        cbuf[SUBLANES - ck:SUBLANES, :] = cv0_ref[...]

    pending = []
    for name in ["lrdt", "gq", "gk", "gv", "gr", "mxbc", "mz", "rq", "rk", "rv", "rg", "ga", "gb", "gc"]:
        a, b = dm.seg[name]
        pending += [(name, s) for s in range(a, b, MXU_COLS)]
    parts = {name: [] for name in dm.seg_names}

    def issue(n=1):
        for _ in range(n):
            if pending:
                name, s = pending.pop(0)
                parts[name].append(_dot(h_ref[...], win_ref[:, s:s + MXU_COLS]))

    every = max(1, tt // MXU_COLS)
    calls = [0]

    def issue_thin(n=1):
        for _ in range(n):
            calls[0] += 1
            if calls[0] % every == 0:
                issue()

    def take(name):
        a, b = dm.seg[name]
        while len(parts[name]) * MXU_COLS < b - a:
            issue()
        return parts[name][0] if len(parts[name]) == 1 else jnp.concatenate(parts[name], axis=1)

    ret_tabs = _retention_tables(dm, lr)
    lrdt = take("lrdt")
    gate = _dot(lrdt[:, :LANES].astype(BF16), wg2_ref[...]) + bg_ref[...]
    q = take("gq") * (dm.gn ** -0.5)
    k = take("gk")
    dt8 = _softplus(lrdt[:, LANES:] + dtb_ref[...])
    a8 = dt8 * (-jnp.exp(alog_ref[...]))
    b8_all = _chunk_cumsum(a8, ls)
    issue(3)
    log_a = _log_sigmoid(gate) * (1.0 / GLA_TAU)
    b_all = _chunk_cumsum(log_a, lg)
    issue(3)
    qd_all = q * jnp.exp(b_all)
    kd_all = k * jnp.exp(-b_all)
    v = take("gv").astype(BF16)
    for c in range(tt // lg):
        r = slice(c * lg, (c + 1) * lg)
        b = b_all[r]
        bl = b[lg - 1:lg, :]
        ks = k[r] * jnp.exp(bl - b)
        ogla[r, :] = _linear_attention_chunk(qd_all[r], kd_all[r], None, qd_all[r], ks, v[r],
                                             jnp.exp(bl), None, stg, dm.gh, dm.gn, dm.gp,
                                             issue_thin if c else (lambda: None))
    y_gla = (_group_norm_heads(ogla[...], dm.gh, dm.gp) * gnorm_ref[...] * _silu(take("gr"))).astype(BF16)
    issue(2)

    xbc_pre = take("mxbc")
    cbuf[SUBLANES:SUBLANES + tt, :] = xbc_pre
    conv = cb_ref[...] + cw_ref[ck:ck + 1, :] * xbc_pre
    for j in range(ck):
        conv = conv + cw_ref[j:j + 1, :] * cbuf[SUBLANES - ck + j:SUBLANES - ck + j + tt, :]
    new_carry = cbuf[SUBLANES + tt - ck:SUBLANES + tt, :]
    cbuf[SUBLANES - ck:SUBLANES, :] = new_carry
    xbc = _silu(conv)
    xs = xbc[:, :dm.s_inner]
    bm = xbc[:, dm.s_inner:dm.s_inner + LANES]
    cm = xbc[:, dm.s_inner + LANES:dm.s_inner + 2 * LANES]
    hpg = dm.sh // dm.sg
    for c in range(tt // ls):
        r = slice(c * ls, (c + 1) * ls)
        bcols, b8t, dtt, eb_exp, w_exp, dec_row = _ssd_decay_terms(b8_all[r], dt8[r], dm)
        xs_c = xs[r]
        xs_bf = xs_c.astype(BF16)
        cm_c = cm[r]
        bm_c = bm[r].astype(BF16)
        causal = _iota((ls, ls), 1) <= _iota((ls, ls), 0)
        gmats = [_dot_nt(jnp.where(_head_lane_mask(cm_c.shape, g, dm.sn), cm_c, 0.0).astype(BF16), bm_c)
                 for g in range(dm.sg)]
        st = sts[...]
        y_inter = _dot(cm_c.astype(BF16), st.astype(BF16)) * eb_exp
        upd = _dot_tn(bm_c, (xs_c * w_exp).astype(BF16))
        blk = (_iota(upd.shape, 0) >> _log2(dm.sn)) == (_iota(upd.shape, 1) >> _log2(dm.sp * hpg))
        sts[...] = st * dec_row + jnp.where(blk, upd, 0.0)
        issue_thin(2)
        y_pairs = []
        for hh in range(dm.sh):
            seg = jnp.where(causal, bcols[hh] - b8t[hh:hh + 1, :], -jnp.inf)
            m_h = gmats[hh // hpg] * dtt[hh:hh + 1, :] * jnp.exp(seg)
            pr = hh // 2
            y_pairs.append(_dot(m_h.astype(BF16), xs_bf[:, pr * LANES:(pr + 1) * LANES]))
            if hh % 2:
                issue_thin()
        first = _iota((ls, LANES), 1) < dm.sp
        y_intra = jnp.concatenate(
            [jnp.where(first, y_pairs[2 * j], y_pairs[2 * j + 1]) for j in range(dm.sh // 2)], axis=1)
        ossd[r, :] = y_intra + y_inter
    y = ossd[...] + dexp_ref[...] * xs
    y = y * _silu(take("mz"))
    issue()
    y_ssd = _rms(y, snorm_ref[...]).astype(BF16)

    late = [(y_gla, wbg_ref, s) for s in range(0, dm.d, MXU_COLS)]
    late += [(y_ssd, wbs_ref, s) for s in range(0, dm.d, MXU_COLS)]
    yb_parts = []

    def issue_late(n=1):
        for _ in range(n):
            if late:
                yv, w_ref, s = late.pop(0)
                yb_parts.append(_dot(yv, w_ref[:, s:s + MXU_COLS]))

    cos = cos_ref[...]
    sin = sin_ref[...]
    rq_pre = take("rq")
    rk_pre = take("rk")
    issue(len(pending))
    rq = rq_pre * cos + _swap_halves(rq_pre, dm.rn) * sin
    issue_late()
    rk = (rk_pre * cos + _swap_halves(rk_pre, dm.rn) * sin) * (dm.rn ** -0.5)
    issue_late()
    rv = take("rv").astype(BF16)
    q_dec, k_dec, st_dec, sc_dec = ret_tabs
    for c in range(tt // lr):
        r = slice(c * lr, (c + 1) * lr)
        oret[r, :] = _linear_attention_chunk(rq[r], rk[r], sc_dec, rq[r] * q_dec, rk[r] * k_dec, rv[r],
                                             None, st_dec, strt, dm.rh, dm.rn, dm.rp, issue_late)
        issue_late()
    issue_late(len(late))
    ncol = dm.d // MXU_COLS
    yb_gla = jnp.concatenate(yb_parts[:ncol], axis=1)
    yb_ssd = jnp.concatenate(yb_parts[ncol:], axis=1)
    ga, gb, gc = take("ga"), take("gb"), take("gc")
    m2 = jax.nn.sigmoid(ga) * yb_gla + jax.nn.sigmoid(gb) * yb_ssd
    y_ret = (_group_norm_heads(oret[...], dm.rh, dm.rp) * rnorm_ref[...] * _silu(take("rg"))).astype(BF16)

    halves = 2 if tt % (2 * SUBLANES * 2) == 0 else 1
    hr = tt // halves
    for i in range(halves):
        r = slice(i * hr, (i + 1) * hr)
        m = m2[r] + jax.nn.sigmoid(gc[r]) * _dot(y_ret[r], wbr_ref[...])
        y_ref[r, :] = x_ref[r, :] + _dot(m.astype(BF16), wout_ref[...])

    @pl.when(t == nt - 1)
    def _():
        sg_ref[...] = stg[...]
        sr_ref[...] = strt[...]
        sz = sts[...]
        acc = sz[0:dm.sn]
        for g in range(1, dm.sg):
            acc = acc + sz[g * dm.sn:(g + 1) * dm.sn]
        ss_ref[...] = acc
        cv_ref[...] = cbuf[SUBLANES - ck:SUBLANES, :]


def _mixer(x, h, cos, sin, sg0, ss0, cv0, sr0, wts, dm, *, tt, lg, ls, lr):
    bsz, seq, d = x.shape
    assert seq % tt == 0 and tt % lg == 0 and tt % ls == 0 and tt % lr == 0
    kern = functools.partial(_mixer_kernel, dm=dm, tt=tt, lg=lg, ls=ls, lr=lr)
    gq = dm.gh * dm.gn
    rq = dm.rh * dm.rn

    def per_b(shape):
        nd = len(shape)
        return pl.BlockSpec((None,) + shape, lambda b, t: (b,) + (0,) * nd)

    in_specs = [
        pl.BlockSpec((None, tt, d), lambda b, t: (b, t, 0)),
        pl.BlockSpec((None, tt, d), lambda b, t: (b, t, 0)),
        pl.BlockSpec((tt, rq), lambda b, t: (t, 0)),
        pl.BlockSpec((tt, rq), lambda b, t: (t, 0)),
        per_b((gq, dm.gp)), per_b((dm.sn, dm.s_inner)), per_b((dm.conv_k - 1, dm.conv_dim)), per_b((rq, dm.rp)),
    ] + [_const_spec(w.shape) for w in wts]
    out_specs = [
        pl.BlockSpec((None, tt, d), lambda b, t: (b, t, 0)),
        per_b((gq, dm.gp)), per_b((dm.sn, dm.s_inner)), per_b((dm.conv_k - 1, dm.conv_dim)), per_b((rq, dm.rp)),
    ]
    out_shape = [
        jax.ShapeDtypeStruct((bsz, seq, d), F32),
        jax.ShapeDtypeStruct((bsz, gq, dm.gp), F32),
        jax.ShapeDtypeStruct((bsz, dm.sn, dm.s_inner), F32),
        jax.ShapeDtypeStruct((bsz, dm.conv_k - 1, dm.conv_dim), F32),
        jax.ShapeDtypeStruct((bsz, rq, dm.rp), F32),
    ]
    scratch = [
        pltpu.VMEM((gq, dm.gp), F32),
        pltpu.VMEM((dm.sg * dm.sn, dm.s_inner), F32),
        pltpu.VMEM((tt + SUBLANES, dm.conv_dim), F32),
        pltpu.VMEM((rq, dm.rp), F32),
        pltpu.VMEM((tt, dm.gh * dm.gp), F32),
        pltpu.VMEM((tt, dm.s_inner), F32),
        pltpu.VMEM((tt, dm.rh * dm.rp), F32),
    ]
    return pl.pallas_call(
        kern,
        grid=(bsz, seq // tt),
        in_specs=in_specs,
        out_specs=out_specs,
        out_shape=out_shape,
        scratch_shapes=scratch,
        compiler_params=pltpu.CompilerParams(
            dimension_semantics=("arbitrary", "arbitrary"), vmem_limit_bytes=VMEM_LIMIT_BYTES),
        name="mixer",
    )(x, h, cos, sin, sg0, ss0, cv0, sr0, *wts)


def _pad_cols(w, n):
    return jnp.pad(w, ((0, 0), (0, n - w.shape[1])))


def _row(v, n=None):
    v = v.reshape(1, -1).astype(F32)
    return v if n is None else _pad_cols(v, n)


def _rope_tables(pos, dm):
    half = dm.rn // 2
    freqs = ROPE_BASE ** (-jnp.arange(half, dtype=F32) / half)
    ang = pos.astype(F32)[:, None] * freqs[None, :]
    cos = jnp.cos(ang)
    sin = jnp.sin(ang)
    cos_h = jnp.concatenate([cos, cos], axis=1)
    sin_h = jnp.concatenate([-sin, sin], axis=1)
    return jnp.tile(cos_h, (1, dm.rh)), jnp.tile(sin_h, (1, dm.rh))


def _state_to_kernel_t(s):
    b, h, n, p = s.shape
    return s.reshape(b, h * n, p)


def _state_from_kernel_t(s, h):
    b, hn, p = s.shape
    return s.reshape(b, h, hn // h, p)


def _ssd_to_kernel(s):
    b, h, n, p = s.shape
    return jnp.transpose(s, (0, 2, 1, 3)).reshape(b, n, h * p)


def _ssd_from_kernel(s, h):
    b, n, hp = s.shape
    return jnp.transpose(s.reshape(b, n, h, hp // h), (0, 2, 1, 3))


def _tiles(seq):
    if seq % SCAN_CHUNK == 0:
        tt = 512 if seq % 512 == 0 else SCAN_CHUNK
        return dict(tt=tt, lg=SCAN_CHUNK, ls=min(tt, 128), lr=min(tt, 128))
    return dict(tt=seq, lg=seq, ls=seq, lr=seq)


def kernel(x_prompt, x_sample, state_gla, state_ssd, cache_conv, state_ret, norm_ffn1, ffn1_w_in, ffn1_w_out,
           norm_mix, w_in, gla_w_gate2, gla_b_gate, gla_norm, ssd_conv_w, ssd_conv_b, ssd_dt_bias, ssd_a_log,
           ssd_d, ssd_norm, ret_norm, w_branch_gla, w_branch_ssd, w_branch_ret, w_out, norm_ffn2, ffn2_w_in,
           ffn2_w_out, norm_final):
    depth = w_in.shape[0]
    d = x_prompt.shape[-1]
    d_ff = ffn1_w_out.shape[1]
    dm = _Dims(state_gla, state_ssd, cache_conv, state_ret, gla_w_gate2, ssd_conv_w, d)

    split_sizes = (dm.gh * dm.gn, dm.gh * dm.gn, dm.gh * dm.gp, dm.gh * dm.gp, dm.rank,
                   dm.s_inner, dm.conv_dim, dm.sh,
                   dm.rh * dm.rn, dm.rh * dm.rn, dm.rh * dm.rp, dm.rh * dm.rp, d, d, d)
    pts, acc = [], 0
    for s in split_sizes[:-1]:
        acc += s
        pts.append(acc)

    def mixer_weights(l):
        (s_gq, s_gk, s_gv, s_gr, s_glr, s_mz, s_mxbc, s_mdt,
         s_rq, s_rk, s_rv, s_rg, s_ga, s_gb, s_gc) = jnp.split(w_in[l], pts, axis=1)
        win = jnp.concatenate(
            [s_gq, s_gk, _pad_cols(s_glr, LANES), _pad_cols(s_mdt, LANES), s_gv, s_gr, s_mz, s_mxbc,
             s_rq, s_rk, s_rv, s_rg, s_ga, s_gb, s_gc], axis=1).astype(BF16)
        assert win.shape[1] == dm.in_cols
        wg2 = jnp.pad(gla_w_gate2[l], ((0, LANES - dm.rank), (0, 0))).astype(BF16)
        return [
            win, wg2, _row(gla_b_gate[l]), _row(gla_norm[l]),
            ssd_conv_w[l].astype(F32), _row(ssd_conv_b[l]), _row(ssd_dt_bias[l], LANES), _row(ssd_a_log[l], LANES),
            _row(jnp.repeat(ssd_d[l], dm.sp)), _row(ssd_norm[l]), _row(ret_norm[l]),
            w_branch_gla[l].astype(BF16), w_branch_ssd[l].astype(BF16), w_branch_ret[l].astype(BF16),
            w_out[l].astype(BF16),
        ]

    def ffn_weights(nrm, w_i, w_o, l):
        return _row(nrm[l]), w_i[l][:, :d_ff].astype(BF16), w_i[l][:, d_ff:].astype(BF16), w_o[l].astype(BF16)

    g_final = _row(norm_final)

    def run_group(x, pos, states):
        bsz, seq, _ = x.shape
        m = bsz * seq
        tm = 1024 if m % 1024 == 0 else m
        cos, sin = _rope_tables(pos, dm)
        tl = _tiles(seq)
        new_states = []
        for l in range(depth):
            sg0, ss0, cv0, sr0 = states(l)
            g1, wg1, wu1, wo1 = ffn_weights(norm_ffn1, ffn1_w_in, ffn1_w_out, l)
            x, h = _ffn(x.reshape(m, d), g1, wg1, wu1, wo1, _row(norm_mix[l]), tm=tm, fc=256, post="emit")
            x, sg, ss, cv, sr = _mixer(x.reshape(bsz, seq, d), h.reshape(bsz, seq, d), cos, sin,
                                       _state_to_kernel_t(sg0), _ssd_to_kernel(ss0), cv0,
                                       _state_to_kernel_t(sr0), mixer_weights(l), dm, **tl)
            g2, wg2_, wu2, wo2 = ffn_weights(norm_ffn2, ffn2_w_in, ffn2_w_out, l)
            last = l == depth - 1
            x = _ffn(x.reshape(m, d), g2, wg2_, wu2, wo2, g_final if last else None, tm=tm, fc=256,
                     post="norm" if last else None).reshape(bsz, seq, d)
            new_states.append((_state_from_kernel_t(sg, dm.gh), _ssd_from_kernel(ss, dm.sh), cv,
                               _state_from_kernel_t(sr, dm.rh)))
        return x, new_states

    bp, tp = x_prompt.shape[:2]
    ts = x_sample.shape[1]
    dt_ = x_prompt.dtype

    def zero_states(_):
        return (jnp.zeros((bp, dm.gh, dm.gn, dm.gp), dt_), jnp.zeros((bp, dm.sh, dm.sn, dm.sp), dt_),
                jnp.zeros((bp, dm.conv_k - 1, dm.conv_dim), dt_), jnp.zeros((bp, dm.rh, dm.rn, dm.rp), dt_))

    def carried_states(l):
        return state_gla[l], state_ssd[l], cache_conv[l], state_ret[l]

    y_p, st_p = run_group(x_prompt, jnp.arange(tp, dtype=jnp.int32), zero_states)
    y_s, st_s = run_group(x_sample, PAST_LEN + jnp.arange(ts, dtype=jnp.int32), carried_states)

    def stack(sts, i):
        return jnp.stack([s[i] for s in sts])

    return (y_p, y_s, stack(st_p, 0), stack(st_p, 1), stack(st_p, 2), stack(st_p, 3),
            stack(st_s, 0), stack(st_s, 1), stack(st_s, 2), stack(st_s, 3))
```

```python
import functools
import math

import jax
import jax.numpy as jnp
from jax import lax
from jax.experimental import pallas as pl
from jax.experimental.pallas import tpu as pltpu

F32 = jnp.float32
BF16 = jnp.bfloat16

EPS = 1e-6
PAST_LEN = 1024
GLA_TAU = 16.0
ROPE_BASE = 10000.0
SCAN_CHUNK = 64

LANES = 128
SUBLANES = 8
MXU_COLS = 256
FFN_OUT_ROWS = 256
VMEM_LIMIT_BYTES = 56 * 1024 * 1024


def _dot(a, b):
    return jnp.dot(a, b, preferred_element_type=F32)


def _dot_nt(a, b):
    return lax.dot_general(a, b, (((1,), (1,)), ((), ())), preferred_element_type=F32)


def _dot_tn(a, b):
    return lax.dot_general(a, b, (((0,), (0,)), ((), ())), preferred_element_type=F32)


def _split3(x):
    hi = x.astype(BF16)
    r1 = x - hi.astype(F32)
    mid = r1.astype(BF16)
    lo = (r1 - mid.astype(F32)).astype(BF16)
    return hi, mid, lo


def _dot_exact01(m01, x):
    hi, mid, lo = _split3(x)
    return _dot(m01, hi) + _dot(m01, mid) + _dot(m01, lo)


def _chunk_cumsum(x, L):
    T = x.shape[0]
    rows = MXU_COLS if (T % MXU_COLS == 0 and MXU_COLS % L == 0) else T
    tril = _block_tril(rows, L)
    out = [_dot_exact01(tril, x[i:i + rows]) for i in range(0, T, rows)]
    return out[0] if len(out) == 1 else jnp.concatenate(out, axis=0)


def _rms(x, g):
    return x * lax.rsqrt(jnp.mean(x * x, axis=-1, keepdims=True) + EPS) * g


def _silu(x):
    return x * jax.nn.sigmoid(x)


def _softplus(x):
    return jnp.maximum(x, 0.0) + jnp.log(1.0 + jnp.exp(-jnp.abs(x)))


def _log_sigmoid(x):
    return jnp.minimum(x, 0.0) - jnp.log(1.0 + jnp.exp(-jnp.abs(x)))


def _iota(shape, axis):
    return lax.broadcasted_iota(jnp.int32, shape, axis)


def _log2(n):
    k = int(math.log2(n))
    assert (1 << k) == n, f"{n} must be a power of two"
    return k


def _ffn_kernel(*refs, fc, post):
    if post is None:
        x_ref, g_ref, wg_ref, wu_ref, wo_ref, o_ref, h_ref, a_ref = refs
    elif post == "norm":
        x_ref, g_ref, wg_ref, wu_ref, wo_ref, g2_ref, o_ref, h_ref, a_ref = refs
    else:
        x_ref, g_ref, wg_ref, wu_ref, wo_ref, g2_ref, o_ref, hn_ref, h_ref, a_ref = refs
    x = x_ref[...]
    h_ref[...] = _rms(x, g_ref[...]).astype(BF16)
    d_ff = wg_ref.shape[1]
    for c in range(d_ff // fc):
        sl = slice(c * fc, (c + 1) * fc)
        gate = _dot(h_ref[...], wg_ref[:, sl])
        up = _dot(h_ref[...], wu_ref[:, sl])
        a_ref[:, sl] = (_silu(gate) * up).astype(BF16)
    tm = x_ref.shape[0]
    rb = FFN_OUT_ROWS if tm % FFN_OUT_ROWS == 0 else tm
    for i in range(tm // rb):
        r = slice(i * rb, (i + 1) * rb)
        y = x_ref[r, :] + 0.5 * _dot(a_ref[r, :], wo_ref[...])
        if post == "norm":
            y = _rms(y, g2_ref[...])
        o_ref[r, :] = y
        if post == "emit":
            hn_ref[r, :] = _rms(y, g2_ref[...]).astype(BF16)


def _const_spec(shape):
    nd = len(shape)
    return pl.BlockSpec(shape, lambda *_: (0,) * nd, pipeline_mode=pl.Buffered(1))


def _ffn(x2d, g, w_gate, w_up, w_out, g2, *, tm, fc, post):
    m, d = x2d.shape
    d_ff = w_gate.shape[1]
    assert m % tm == 0 and d_ff % fc == 0 and (g2 is None) == (post is None)
    kern = functools.partial(_ffn_kernel, fc=fc, post=post)
    row_spec = pl.BlockSpec((tm, d), lambda i: (i, 0))
    in_specs = [row_spec, _const_spec((1, d)), _const_spec((d, d_ff)), _const_spec((d, d_ff)),
                _const_spec((d_ff, d))]
    args = [x2d, g, w_gate, w_up, w_out]
    if post is not None:
        in_specs.append(_const_spec((1, d)))
        args.append(g2)
    out_specs, out_shape = row_spec, jax.ShapeDtypeStruct((m, d), F32)
    if post == "emit":
        out_specs, out_shape = [row_spec, row_spec], [out_shape, jax.ShapeDtypeStruct((m, d), BF16)]
    return pl.pallas_call(
        kern,
        grid=(m // tm,),
        in_specs=in_specs,
        out_specs=out_specs,
        out_shape=out_shape,
        scratch_shapes=[pltpu.VMEM((tm, d), BF16), pltpu.VMEM((tm, d_ff), BF16)],
        compiler_params=pltpu.CompilerParams(
            dimension_semantics=("arbitrary",), vmem_limit_bytes=VMEM_LIMIT_BYTES),
        name="ffn",
    )(*args)


class _Dims:
    def __init__(self, state_gla, state_ssd, cache_conv, state_ret, gla_w_gate2, ssd_conv_w, d_model):
        _, _, self.gh, self.gn, self.gp = state_gla.shape
        _, _, self.sh, self.sn, self.sp = state_ssd.shape
        _, _, self.rh, self.rn, self.rp = state_ret.shape
        self.conv_k = ssd_conv_w.shape[1]
        self.conv_dim = cache_conv.shape[-1]
        self.rank = gla_w_gate2.shape[1]
        self.d = d_model
        self.s_inner = self.sh * self.sp
        self.sg = (self.conv_dim - self.s_inner) // (2 * self.sn)
        assert self.gh * self.gn == 2 * LANES and self.gp == LANES
        assert self.rh * self.rn == 2 * LANES and self.rp == LANES
        assert self.sp * 2 == LANES and self.sg * self.sn == LANES and self.sh <= LANES
        assert self.rank <= LANES and self.conv_k - 1 <= SUBLANES
        w = [self.gh * self.gn, self.gh * self.gn, 2 * LANES, self.gh * self.gp, self.gh * self.gp,
             self.s_inner, self.conv_dim,
             self.rh * self.rn, self.rh * self.rn, self.rh * self.rp, self.rh * self.rp,
             d_model, d_model, d_model]
        self.seg_names = ["gq", "gk", "lrdt", "gv", "gr", "mz", "mxbc", "rq", "rk", "rv", "rg", "ga", "gb", "gc"]
        self.seg = {}
        off = 0
        for n, wd in zip(self.seg_names, w):
            assert wd % MXU_COLS == 0
            self.seg[n] = (off, off + wd)
            off += wd
        self.in_cols = off


def _head_lane_mask(shape, h, width):
    lane = _iota(shape, 1)
    return (lane >> _log2(width)) == h


def _stack_heads(x, nh, width):
    parts = [jnp.where(_head_lane_mask(x.shape, h, width), x, 0.0).astype(BF16) for h in range(nh)]
    return jnp.concatenate(parts, axis=0)


def _linear_attention_chunk(q_sc, k_sc, sc_mul, q_in, k_up, v, dec_row, dec_col, st_ref, nh, n, p, issue):
    L = q_sc.shape[0]
    k_big = _stack_heads(k_sc, nh, n)
    sc = _dot_nt(q_sc.astype(BF16), k_big)
    lane = _iota(sc.shape, 1)
    row = _iota(sc.shape, 0)
    causal = (lane & (L - 1)) <= row
    sc = jnp.where(causal, sc if sc_mul is None else sc * sc_mul, 0.0)
    sc = sc.astype(BF16)
    if L % LANES == 0:
        intra = jnp.concatenate(
            [_dot(sc[:, h * L:(h + 1) * L], v[:, h * p:(h + 1) * p].astype(BF16)) for h in range(nh)], axis=1)
    else:
        intra = _dot(sc, _stack_heads(v, nh, p))
    issue()
    st = st_ref[...]
    rowhead = _iota(st.shape, 0) >> _log2(n)
    st_bd = jnp.concatenate([jnp.where(rowhead == h, st, 0.0).astype(BF16) for h in range(nh)], axis=1)
    o = intra + _dot(q_in.astype(BF16), st_bd)
    rows = [k_up]
    if dec_col is None:
        rows.append(jnp.broadcast_to(dec_row, (SUBLANES, nh * n)))
    pad = LANES - L - (len(rows) - 1) * SUBLANES
    assert pad >= 0
    if pad:
        rows.append(jnp.zeros((pad, nh * n), F32))
    kt = jnp.concatenate(rows, axis=0).T if len(rows) > 1 else k_up.T
    if dec_col is None:
        dec_col = kt[:, L:L + 1]
    kt = kt.astype(BF16)
    upd = []
    for h in range(nh):
        v_h = v[:, h * p:(h + 1) * p].astype(BF16)
        if L < LANES:
            v_h = jnp.concatenate([v_h, jnp.zeros((LANES - L, p), BF16)], axis=0)
        upd.append(_dot(kt[h * n:(h + 1) * n, :], v_h))
    st_ref[...] = st * dec_col + jnp.concatenate(upd, axis=0)
    return o


def _group_norm_heads(o, nh, p):
    outs = []
    for h in range(nh):
        oh = o[:, h * p:(h + 1) * p]
        mu = jnp.mean(oh, axis=-1, keepdims=True)
        dlt = oh - mu
        var = jnp.mean(dlt * dlt, axis=-1, keepdims=True)
        outs.append(dlt * lax.rsqrt(var + EPS))
    return jnp.concatenate(outs, axis=1)


def _block_tril(tt, L):
    row = _iota((tt, tt), 0)
    col = _iota((tt, tt), 1)
    k = _log2(L)
    return jnp.where((col <= row) & ((row >> k) == (col >> k)), 1.0, 0.0).astype(BF16)


def _ret_log_gamma(head, nh):
    out = jnp.full(head.shape, math.log1p(-2.0 ** (-5.0 - (nh - 1))), F32)
    for h in range(nh - 1):
        out = jnp.where(head == h, math.log1p(-2.0 ** (-5.0 - h)), out)
    return out


def _ssd_decay_terms(b8, dtc, dm):
    L = b8.shape[0]
    bl8 = b8[L - 1:L, :]
    w8 = dtc * jnp.exp(bl8 - b8)
    b8t = _transpose_rows(b8)
    dtt = _transpose_rows(dtc)
    lane8 = _iota(b8.shape, 1)
    ebl8 = jnp.exp(bl8)
    bcols, eb_cols, w_cols, dec_cols = [], [], [], []
    for hh in range(dm.sh):
        bcol = jnp.sum(jnp.where(lane8 == hh, b8, 0.0), axis=-1, keepdims=True)
        bcols.append(bcol)
        eb_cols.append(jnp.broadcast_to(jnp.exp(bcol), (L, LANES)))
        w_cols.append(jnp.broadcast_to(
            jnp.sum(jnp.where(lane8 == hh, w8, 0.0), axis=-1, keepdims=True), (L, LANES)))
        dec_cols.append(jnp.broadcast_to(
            jnp.sum(jnp.where(_iota(bl8.shape, 1) == hh, ebl8, 0.0), axis=-1, keepdims=True), (1, LANES)))

    def pair_lanes(cols):
        first = _iota(cols[0].shape, 1) < dm.sp
        return jnp.concatenate(
            [jnp.where(first, cols[2 * j], cols[2 * j + 1]) for j in range(dm.sh // 2)], axis=1)

    return bcols, b8t, dtt, pair_lanes(eb_cols), pair_lanes(w_cols), pair_lanes(dec_cols)


def _retention_tables(dm, L):
    nlane = dm.rh * dm.rn
    lgam_q = _ret_log_gamma(_iota((L, nlane), 1) >> _log2(dm.rn), dm.rh)
    pos_q = _iota((L, nlane), 0).astype(F32)
    q_dec = jnp.exp((pos_q + 1.0) * lgam_q)
    k_dec = jnp.exp((L - 1.0 - pos_q) * lgam_q)
    lgam_r = _ret_log_gamma(_iota((nlane, dm.rp), 0) >> _log2(dm.rn), dm.rh)
    st_dec = jnp.exp(L * lgam_r)
    lgam_s = _ret_log_gamma(_iota((L, dm.rh * L), 1) >> _log2(L), dm.rh)
    dpos = (_iota((L, dm.rh * L), 0) - (_iota((L, dm.rh * L), 1) & (L - 1))).astype(F32)
    sc_dec = jnp.exp(jnp.maximum(dpos, 0.0) * lgam_s)
    return q_dec, k_dec, st_dec, sc_dec


def _transpose_rows(x):
    L = x.shape[0]
    if L % LANES:
        pad = LANES - L % LANES
        x = jnp.concatenate([x, jnp.zeros((pad, x.shape[1]), x.dtype)], axis=0)
    return x.T[:, :L]


def _swap_halves(x, n):
    width = x.shape[1]
    half = n // 2
    first = (_iota(x.shape, 1) & (n - 1)) < half
    return jnp.where(first, pltpu.roll(x, width - half, 1), pltpu.roll(x, half, 1))


def _mixer_kernel(x_ref, h_ref, cos_ref, sin_ref, sg0_ref, ss0_ref, cv0_ref, sr0_ref,
                  win_ref, wg2_ref, bg_ref, gnorm_ref, cw_ref, cb_ref, dtb_ref, alog_ref,
                  dexp_ref, snorm_ref, rnorm_ref, wbg_ref, wbs_ref, wbr_ref, wout_ref,
                  y_ref, sg_ref, ss_ref, cv_ref, sr_ref,
                  stg, sts, cbuf, strt, ogla, ossd, oret,
                  *, dm, tt, lg, ls, lr):
    t = pl.program_id(1)
    nt = pl.num_programs(1)
    ck = dm.conv_k - 1

    @pl.when(t == 0)
    def _():
        stg[...] = sg0_ref[...]
        strt[...] = sr0_ref[...]
        z0 = ss0_ref[...]
        hpg = dm.sh // dm.sg
        sts[...] = jnp.concatenate(
            [jnp.where(_head_lane_mask(z0.shape, g, dm.sp * hpg), z0, 0.0) for g in range(dm.sg)], axis=0)
        cbuf[SUBLANES - ck:SUBLANES, :] = cv0_ref[...]

    pending = []
    for name in ["lrdt", "mxbc", "gq", "gk", "gv", "gr", "mz", "rq", "rk", "rv", "rg", "ga", "gb", "gc"]:
        a, b = dm.seg[name]
        pending += [(name, s) for s in range(a, b, MXU_COLS)]
    parts = {name: [] for name in dm.seg_names}

    def issue(n=1):
        for _ in range(n):
            if pending:
                name, s = pending.pop(0)
                parts[name].append(_dot(h_ref[...], win_ref[:, s:s + MXU_COLS]))

    every = max(1, tt // MXU_COLS)
    calls = [0]

    def issue_thin(n=1):
        for _ in range(n):
            calls[0] += 1
            if calls[0] % every == 0:
                issue()

    def take(name):
        a, b = dm.seg[name]
        while len(parts[name]) * MXU_COLS < b - a:
            issue()
        return parts[name][0] if len(parts[name]) == 1 else jnp.concatenate(parts[name], axis=1)

    ret_tabs = _retention_tables(dm, lr)
    lrdt = take("lrdt")
    gate = _dot(lrdt[:, :LANES].astype(BF16), wg2_ref[...]) + bg_ref[...]
    xbc_pre = take("mxbc")
    cbuf[SUBLANES:SUBLANES + tt, :] = xbc_pre
    conv = cb_ref[...] + cw_ref[ck:ck + 1, :] * xbc_pre
    for j in range(ck):
        conv = conv + cw_ref[j:j + 1, :] * cbuf[SUBLANES - ck + j:SUBLANES - ck + j + tt, :]
    new_carry = cbuf[SUBLANES + tt - ck:SUBLANES + tt, :]
    cbuf[SUBLANES - ck:SUBLANES, :] = new_carry
    q = take("gq") * (dm.gn ** -0.5)
    k = take("gk")
    xbc = _silu(conv)
    xs = xbc[:, :dm.s_inner]
    bm = xbc[:, dm.s_inner:dm.s_inner + LANES]
    cm = xbc[:, dm.s_inner + LANES:dm.s_inner + 2 * LANES]
    dt8 = _softplus(lrdt[:, LANES:] + dtb_ref[...])
    a8 = dt8 * (-jnp.exp(alog_ref[...]))
    b8_all = _chunk_cumsum(a8, ls)
    issue(3)
    log_a = _log_sigmoid(gate) * (1.0 / GLA_TAU)
    b_all = _chunk_cumsum(log_a, lg)
    issue(3)
    qd_all = q * jnp.exp(b_all)
    kd_all = k * jnp.exp(-b_all)
    v = take("gv").astype(BF16)
    for c in range(tt // lg):
        r = slice(c * lg, (c + 1) * lg)
        b = b_all[r]
        bl = b[lg - 1:lg, :]
        ks = k[r] * jnp.exp(bl - b)
        ogla[r, :] = _linear_attention_chunk(qd_all[r], kd_all[r], None, qd_all[r], ks, v[r],
                                             jnp.exp(bl), None, stg, dm.gh, dm.gn, dm.gp,
                                             issue_thin if c else (lambda: None))
    y_gla = (_group_norm_heads(ogla[...], dm.gh, dm.gp) * gnorm_ref[...] * _silu(take("gr"))).astype(BF16)
    issue(2)

    hpg = dm.sh // dm.sg
    for c in range(tt // ls):
        r = slice(c * ls, (c + 1) * ls)
        bcols, b8t, dtt, eb_exp, w_exp, dec_row = _ssd_decay_terms(b8_all[r], dt8[r], dm)
        xs_c = xs[r]
        xs_bf = xs_c.astype(BF16)
        cm_c = cm[r]
        bm_c = bm[r].astype(BF16)
        causal = _iota((ls, ls), 1) <= _iota((ls, ls), 0)
        gmats = [_dot_nt(jnp.where(_head_lane_mask(cm_c.shape, g, dm.sn), cm_c, 0.0).astype(BF16), bm_c)
                 for g in range(dm.sg)]
        st = sts[...]
        y_inter = _dot(cm_c.astype(BF16), st.astype(BF16)) * eb_exp
        upd = _dot_tn(bm_c, (xs_c * w_exp).astype(BF16))
        blk = (_iota(upd.shape, 0) >> _log2(dm.sn)) == (_iota(upd.shape, 1) >> _log2(dm.sp * hpg))
        sts[...] = st * dec_row + jnp.where(blk, upd, 0.0)
        issue_thin(2)
        y_pairs = []
        for hh in range(dm.sh):
            seg = jnp.where(causal, bcols[hh] - b8t[hh:hh + 1, :], -jnp.inf)
            m_h = gmats[hh // hpg] * dtt[hh:hh + 1, :] * jnp.exp(seg)
            pr = hh // 2
            y_pairs.append(_dot(m_h.astype(BF16), xs_bf[:, pr * LANES:(pr + 1) * LANES]))
            if hh % 2:
                issue_thin()
        first = _iota((ls, LANES), 1) < dm.sp
        y_intra = jnp.concatenate(
            [jnp.where(first, y_pairs[2 * j], y_pairs[2 * j + 1]) for j in range(dm.sh // 2)], axis=1)
        ossd[r, :] = y_intra + y_inter
    y = ossd[...] + dexp_ref[...] * xs
    y = y * _silu(take("mz"))
    issue()
    y_ssd = _rms(y, snorm_ref[...]).astype(BF16)

    late = [(y_gla, wbg_ref, s) for s in range(0, dm.d, MXU_COLS)]
    late += [(y_ssd, wbs_ref, s) for s in range(0, dm.d, MXU_COLS)]
    yb_parts = []

    def issue_late(n=1):
        for _ in range(n):
            if late:
                yv, w_ref, s = late.pop(0)
                yb_parts.append(_dot(yv, w_ref[:, s:s + MXU_COLS]))

    cos = cos_ref[...]
    sin = sin_ref[...]
    rq_pre = take("rq")
    rk_pre = take("rk")
    issue(len(pending))
    rq = rq_pre * cos + _swap_halves(rq_pre, dm.rn) * sin
    issue_late()
    rk = (rk_pre * cos + _swap_halves(rk_pre, dm.rn) * sin) * (dm.rn ** -0.5)
    issue_late()
    rv = take("rv").astype(BF16)
    q_dec, k_dec, st_dec, sc_dec = ret_tabs
    for c in range(tt // lr):
        r = slice(c * lr, (c + 1) * lr)
        oret[r, :] = _linear_attention_chunk(rq[r], rk[r], sc_dec, rq[r] * q_dec, rk[r] * k_dec, rv[r],
                                             None, st_dec, strt, dm.rh, dm.rn, dm.rp, issue_late)
        issue_late()
    issue_late(len(late))
    ncol = dm.d // MXU_COLS
    yb_gla = jnp.concatenate(yb_parts[:ncol], axis=1)
    yb_ssd = jnp.concatenate(yb_parts[ncol:], axis=1)
    ga, gb, gc = take("ga"), take("gb"), take("gc")
    m2 = jax.nn.sigmoid(ga) * yb_gla + jax.nn.sigmoid(gb) * yb_ssd
    y_ret = (_group_norm_heads(oret[...], dm.rh, dm.rp) * rnorm_ref[...] * _silu(take("rg"))).astype(BF16)

    halves = 2 if tt % (2 * SUBLANES * 2) == 0 else 1
    hr = tt // halves
    for i in range(halves):
        r = slice(i * hr, (i + 1) * hr)
        m = m2[r] + jax.nn.sigmoid(gc[r]) * _dot(y_ret[r], wbr_ref[...])
        y_ref[r, :] = x_ref[r, :] + _dot(m.astype(BF16), wout_ref[...])

    @pl.when(t == nt - 1)
    def _():
        sg_ref[...] = stg[...]
        sr_ref[...] = strt[...]
        sz = sts[...]
        acc = sz[0:dm.sn]
        for g in range(1, dm.sg):
            acc = acc + sz[g * dm.sn:(g + 1) * dm.sn]
        ss_ref[...] = acc
        cv_ref[...] = cbuf[SUBLANES - ck:SUBLANES, :]


def _mixer(x, h, cos, sin, sg0, ss0, cv0, sr0, wts, dm, *, tt, lg, ls, lr):
    bsz, seq, d = x.shape
    assert seq % tt == 0 and tt % lg == 0 and tt % ls == 0 and tt % lr == 0
    kern = functools.partial(_mixer_kernel, dm=dm, tt=tt, lg=lg, ls=ls, lr=lr)
    gq = dm.gh * dm.gn
    rq = dm.rh * dm.rn

    def per_b(shape):
        nd = len(shape)
        return pl.BlockSpec((None,) + shape, lambda b, t: (b,) + (0,) * nd)

    in_specs = [
        pl.BlockSpec((None, tt, d), lambda b, t: (b, t, 0)),
        pl.BlockSpec((None, tt, d), lambda b, t: (b, t, 0)),
        pl.BlockSpec((tt, rq), lambda b, t: (t, 0)),
        pl.BlockSpec((tt, rq), lambda b, t: (t, 0)),
        per_b((gq, dm.gp)), per_b((dm.sn, dm.s_inner)), per_b((dm.conv_k - 1, dm.conv_dim)), per_b((rq, dm.rp)),
    ] + [_const_spec(w.shape) for w in wts]
    out_specs = [
        pl.BlockSpec((None, tt, d), lambda b, t: (b, t, 0)),
        per_b((gq, dm.gp)), per_b((dm.sn, dm.s_inner)), per_b((dm.conv_k - 1, dm.conv_dim)), per_b((rq, dm.rp)),
    ]
    out_shape = [
        jax.ShapeDtypeStruct((bsz, seq, d), F32),
        jax.ShapeDtypeStruct((bsz, gq, dm.gp), F32),
        jax.ShapeDtypeStruct((bsz, dm.sn, dm.s_inner), F32),
        jax.ShapeDtypeStruct((bsz, dm.conv_k - 1, dm.conv_dim), F32),
        jax.ShapeDtypeStruct((bsz, rq, dm.rp), F32),
    ]
    scratch = [
        pltpu.VMEM((gq, dm.gp), F32),
        pltpu.VMEM((dm.sg * dm.sn, dm.s_inner), F32),
        pltpu.VMEM((tt + SUBLANES, dm.conv_dim), F32),
        pltpu.VMEM((rq, dm.rp), F32),
        pltpu.VMEM((tt, dm.gh * dm.gp), F32),
        pltpu.VMEM((tt, dm.s_inner), F32),
        pltpu.VMEM((tt, dm.rh * dm.rp), F32),
    ]
    return pl.pallas_call(
        kern,
        grid=(bsz, seq // tt),
        in_specs=in_specs,
        out_specs=out_specs,
        out_shape=out_shape,
        scratch_shapes=scratch,
        compiler_params=pltpu.CompilerParams(
            dimension_semantics=("arbitrary", "arbitrary"), vmem_limit_bytes=VMEM_LIMIT_BYTES),
        name="mixer",
    )(x, h, cos, sin, sg0, ss0, cv0, sr0, *wts)


def _pad_cols(w, n):
    return jnp.pad(w, ((0, 0), (0, n - w.shape[1])))


def _row(v, n=None):
    v = v.reshape(1, -1).astype(F32)
    return v if n is None else _pad_cols(v, n)


def _rope_tables(pos, dm):
    half = dm.rn // 2
    freqs = ROPE_BASE ** (-jnp.arange(half, dtype=F32) / half)
    ang = pos.astype(F32)[:, None] * freqs[None, :]
    cos = jnp.cos(ang)
    sin = jnp.sin(ang)
    cos_h = jnp.concatenate([cos, cos], axis=1)
    sin_h = jnp.concatenate([-sin, sin], axis=1)
    return jnp.tile(cos_h, (1, dm.rh)), jnp.tile(sin_h, (1, dm.rh))


def _state_to_kernel_t(s):
    b, h, n, p = s.shape
    return s.reshape(b, h * n, p)


def _state_from_kernel_t(s, h):
    b, hn, p = s.shape
    return s.reshape(b, h, hn // h, p)


def _ssd_to_kernel(s):
    b, h, n, p = s.shape
    return jnp.transpose(s, (0, 2, 1, 3)).reshape(b, n, h * p)


def _ssd_from_kernel(s, h):
    b, n, hp = s.shape
    return jnp.transpose(s.reshape(b, n, h, hp // h), (0, 2, 1, 3))


def _tiles(seq):
    if seq % SCAN_CHUNK == 0:
        tt = 512 if seq % 512 == 0 else SCAN_CHUNK
        return dict(tt=tt, lg=SCAN_CHUNK, ls=min(tt, 128), lr=min(tt, 128))
    return dict(tt=seq, lg=seq, ls=seq, lr=seq)


def kernel(x_prompt, x_sample, state_gla, state_ssd, cache_conv, state_ret, norm_ffn1, ffn1_w_in, ffn1_w_out,
           norm_mix, w_in, gla_w_gate2, gla_b_gate, gla_norm, ssd_conv_w, ssd_conv_b, ssd_dt_bias, ssd_a_log,
           ssd_d, ssd_norm, ret_norm, w_branch_gla, w_branch_ssd, w_branch_ret, w_out, norm_ffn2, ffn2_w_in,
           ffn2_w_out, norm_final):
    depth = w_in.shape[0]
    d = x_prompt.shape[-1]
    d_ff = ffn1_w_out.shape[1]
    dm = _Dims(state_gla, state_ssd, cache_conv, state_ret, gla_w_gate2, ssd_conv_w, d)

    split_sizes = (dm.gh * dm.gn, dm.gh * dm.gn, dm.gh * dm.gp, dm.gh * dm.gp, dm.rank,
                   dm.s_inner, dm.conv_dim, dm.sh,
                   dm.rh * dm.rn, dm.rh * dm.rn, dm.rh * dm.rp, dm.rh * dm.rp, d, d, d)
    pts, acc = [], 0
    for s in split_sizes[:-1]:
        acc += s
        pts.append(acc)

    def mixer_weights(l):
        (s_gq, s_gk, s_gv, s_gr, s_glr, s_mz, s_mxbc, s_mdt,
         s_rq, s_rk, s_rv, s_rg, s_ga, s_gb, s_gc) = jnp.split(w_in[l], pts, axis=1)
        win = jnp.concatenate(
            [s_gq, s_gk, _pad_cols(s_glr, LANES), _pad_cols(s_mdt, LANES), s_gv, s_gr, s_mz, s_mxbc,
             s_rq, s_rk, s_rv, s_rg, s_ga, s_gb, s_gc], axis=1).astype(BF16)
        assert win.shape[1] == dm.in_cols
        wg2 = jnp.pad(gla_w_gate2[l], ((0, LANES - dm.rank), (0, 0))).astype(BF16)
        return [
            win, wg2, _row(gla_b_gate[l]), _row(gla_norm[l]),
            ssd_conv_w[l].astype(F32), _row(ssd_conv_b[l]), _row(ssd_dt_bias[l], LANES), _row(ssd_a_log[l], LANES),
            _row(jnp.repeat(ssd_d[l], dm.sp)), _row(ssd_norm[l]), _row(ret_norm[l]),
            w_branch_gla[l].astype(BF16), w_branch_ssd[l].astype(BF16), w_branch_ret[l].astype(BF16),
            w_out[l].astype(BF16),
        ]

    def ffn_weights(nrm, w_i, w_o, l):
        return _row(nrm[l]), w_i[l][:, :d_ff].astype(BF16), w_i[l][:, d_ff:].astype(BF16), w_o[l].astype(BF16)

    g_final = _row(norm_final)

    def run_group(x, pos, states):
        bsz, seq, _ = x.shape
        m = bsz * seq
        tm = 1024 if m % 1024 == 0 else m
        cos, sin = _rope_tables(pos, dm)
        tl = _tiles(seq)
        new_states = []
        for l in range(depth):
            sg0, ss0, cv0, sr0 = states(l)
            g1, wg1, wu1, wo1 = ffn_weights(norm_ffn1, ffn1_w_in, ffn1_w_out, l)
            x, h = _ffn(x.reshape(m, d), g1, wg1, wu1, wo1, _row(norm_mix[l]), tm=tm, fc=256, post="emit")
            x, sg, ss, cv, sr = _mixer(x.reshape(bsz, seq, d), h.reshape(bsz, seq, d), cos, sin,
                                       _state_to_kernel_t(sg0), _ssd_to_kernel(ss0), cv0,
                                       _state_to_kernel_t(sr0), mixer_weights(l), dm, **tl)
            g2, wg2_, wu2, wo2 = ffn_weights(norm_ffn2, ffn2_w_in, ffn2_w_out, l)
            last = l == depth - 1
            x = _ffn(x.reshape(m, d), g2, wg2_, wu2, wo2, g_final if last else None, tm=tm, fc=256,
                     post="norm" if last else None).reshape(bsz, seq, d)
            new_states.append((_state_from_kernel_t(sg, dm.gh), _ssd_from_kernel(ss, dm.sh), cv,
                               _state_from_kernel_t(sr, dm.rh)))
        return x, new_states

    bp, tp = x_prompt.shape[:2]
    ts = x_sample.shape[1]
    dt_ = x_prompt.dtype

    def zero_states(_):
        return (jnp.zeros((bp, dm.gh, dm.gn, dm.gp), dt_), jnp.zeros((bp, dm.sh, dm.sn, dm.sp), dt_),
                jnp.zeros((bp, dm.conv_k - 1, dm.conv_dim), dt_), jnp.zeros((bp, dm.rh, dm.rn, dm.rp), dt_))

    def carried_states(l):
        return state_gla[l], state_ssd[l], cache_conv[l], state_ret[l]

    y_p, st_p = run_group(x_prompt, jnp.arange(tp, dtype=jnp.int32), zero_states)
    y_s, st_s = run_group(x_sample, PAST_LEN + jnp.arange(ts, dtype=jnp.int32), carried_states)

    def stack(sts, i):
        return jnp.stack([s[i] for s in sts])

    return (y_p, y_s, stack(st_p, 0), stack(st_p, 1), stack(st_p, 2), stack(st_p, 3),
            stack(st_s, 0), stack(st_s, 1), stack(st_s, 2), stack(st_s, 3))
```

```python
import functools
import math

import jax
import jax.numpy as jnp
from jax import lax
from jax.experimental import pallas as pl
from jax.experimental.pallas import tpu as pltpu

F32 = jnp.float32
BF16 = jnp.bfloat16

EPS = 1e-6
PAST_LEN = 1024
GLA_TAU = 16.0
ROPE_BASE = 10000.0
SCAN_CHUNK = 64

LANES = 128
SUBLANES = 8
MXU_COLS = 256
FFN_OUT_ROWS = 256
VMEM_LIMIT_BYTES = 56 * 1024 * 1024


def _dot(a, b):
    return jnp.dot(a, b, preferred_element_type=F32)


def _dot_nt(a, b):
    return lax.dot_general(a, b, (((1,), (1,)), ((), ())), preferred_element_type=F32)


def _dot_tn(a, b):
    return lax.dot_general(a, b, (((0,), (0,)), ((), ())), preferred_element_type=F32)


def _split3(x):
    hi = x.astype(BF16)
    r1 = x - hi.astype(F32)
    mid = r1.astype(BF16)
    lo = (r1 - mid.astype(F32)).astype(BF16)
    return hi, mid, lo


def _dot_exact01(m01, x):
    hi, mid, lo = _split3(x)
    return _dot(m01, hi) + _dot(m01, mid) + _dot(m01, lo)


def _chunk_cumsum(x, L):
    T = x.shape[0]
    rows = MXU_COLS if (T % MXU_COLS == 0 and MXU_COLS % L == 0) else T
    tril = _block_tril(rows, L)
    out = [_dot_exact01(tril, x[i:i + rows]) for i in range(0, T, rows)]
    return out[0] if len(out) == 1 else jnp.concatenate(out, axis=0)


def _rms(x, g):
    return x * lax.rsqrt(jnp.mean(x * x, axis=-1, keepdims=True) + EPS) * g


def _silu(x):
    return x * jax.nn.sigmoid(x)


def _softplus(x):
    return jnp.maximum(x, 0.0) + jnp.log(1.0 + jnp.exp(-jnp.abs(x)))


def _log_sigmoid(x):
    return jnp.minimum(x, 0.0) - jnp.log(1.0 + jnp.exp(-jnp.abs(x)))


def _iota(shape, axis):
    return lax.broadcasted_iota(jnp.int32, shape, axis)


def _log2(n):
    k = int(math.log2(n))
    assert (1 << k) == n, f"{n} must be a power of two"
    return k


def _ffn_kernel(*refs, fc, post):
    if post is None:
        x_ref, g_ref, wg_ref, wu_ref, wo_ref, o_ref, h_ref, a_ref = refs
    elif post == "norm":
        x_ref, g_ref, wg_ref, wu_ref, wo_ref, g2_ref, o_ref, h_ref, a_ref = refs
    else:
        x_ref, g_ref, wg_ref, wu_ref, wo_ref, g2_ref, o_ref, hn_ref, h_ref, a_ref = refs
    x = x_ref[...]
    h_ref[...] = _rms(x, g_ref[...]).astype(BF16)
    d_ff = wg_ref.shape[1]
    for c in range(d_ff // fc):
        sl = slice(c * fc, (c + 1) * fc)
        gate = _dot(h_ref[...], wg_ref[:, sl])
        up = _dot(h_ref[...], wu_ref[:, sl])
        a_ref[:, sl] = (_silu(gate) * up).astype(BF16)
    tm = x_ref.shape[0]
    rb = FFN_OUT_ROWS if tm % FFN_OUT_ROWS == 0 else tm
    for i in range(tm // rb):
        r = slice(i * rb, (i + 1) * rb)
        y = x_ref[r, :] + 0.5 * _dot(a_ref[r, :], wo_ref[...])
        if post == "norm":
            y = _rms(y, g2_ref[...])
        o_ref[r, :] = y
        if post == "emit":
            hn_ref[r, :] = _rms(y, g2_ref[...]).astype(BF16)


def _const_spec(shape):
    nd = len(shape)
    return pl.BlockSpec(shape, lambda *_: (0,) * nd, pipeline_mode=pl.Buffered(1))


def _ffn(x2d, g, w_gate, w_up, w_out, g2, *, tm, fc, post):
    m, d = x2d.shape
    d_ff = w_gate.shape[1]
    assert m % tm == 0 and d_ff % fc == 0 and (g2 is None) == (post is None)
    kern = functools.partial(_ffn_kernel, fc=fc, post=post)
    row_spec = pl.BlockSpec((tm, d), lambda i: (i, 0))
    in_specs = [row_spec, _const_spec((1, d)), _const_spec((d, d_ff)), _const_spec((d, d_ff)),
                _const_spec((d_ff, d))]
    args = [x2d, g, w_gate, w_up, w_out]
    if post is not None:
        in_specs.append(_const_spec((1, d)))
        args.append(g2)
    out_specs, out_shape = row_spec, jax.ShapeDtypeStruct((m, d), F32)
    if post == "emit":
        out_specs, out_shape = [row_spec, row_spec], [out_shape, jax.ShapeDtypeStruct((m, d), BF16)]
    return pl.pallas_call(
        kern,
        grid=(m // tm,),
        in_specs=in_specs,
        out_specs=out_specs,
        out_shape=out_shape,
        scratch_shapes=[pltpu.VMEM((tm, d), BF16), pltpu.VMEM((tm, d_ff), BF16)],
        compiler_params=pltpu.CompilerParams(
            dimension_semantics=("arbitrary",), vmem_limit_bytes=VMEM_LIMIT_BYTES),
        name="ffn",
    )(*args)


class _Dims:
    def __init__(self, state_gla, state_ssd, cache_conv, state_ret, gla_w_gate2, ssd_conv_w, d_model):
        _, _, self.gh, self.gn, self.gp = state_gla.shape
        _, _, self.sh, self.sn, self.sp = state_ssd.shape
        _, _, self.rh, self.rn, self.rp = state_ret.shape
        self.conv_k = ssd_conv_w.shape[1]
        self.conv_dim = cache_conv.shape[-1]
        self.rank = gla_w_gate2.shape[1]
        self.d = d_model
        self.s_inner = self.sh * self.sp
        self.sg = (self.conv_dim - self.s_inner) // (2 * self.sn)
        assert self.gh * self.gn == 2 * LANES and self.gp == LANES
        assert self.rh * self.rn == 2 * LANES and self.rp == LANES
        assert self.sp * 2 == LANES and self.sg * self.sn == LANES and self.sh <= LANES
        assert self.rank <= LANES and self.conv_k - 1 <= SUBLANES
        w = [self.gh * self.gn, self.gh * self.gn, 2 * LANES, self.gh * self.gp, self.gh * self.gp,
             self.s_inner, self.conv_dim,
             self.rh * self.rn, self.rh * self.rn, self.rh * self.rp, self.rh * self.rp,
             d_model, d_model, d_model]
        self.seg_names = ["gq", "gk", "lrdt", "gv", "gr", "mz", "mxbc", "rq", "rk", "rv", "rg", "ga", "gb", "gc"]
        self.seg = {}
        off = 0
        for n, wd in zip(self.seg_names, w):
            assert wd % MXU_COLS == 0
            self.seg[n] = (off, off + wd)
            off += wd
        self.in_cols = off


def _head_lane_mask(shape, h, width):
    lane = _iota(shape, 1)
    return (lane >> _log2(width)) == h


def _stack_heads(x, nh, width):
    parts = [jnp.where(_head_lane_mask(x.shape, h, width), x, 0.0).astype(BF16) for h in range(nh)]
    return jnp.concatenate(parts, axis=0)


def _linear_attention_chunk(q_sc, k_sc, sc_mul, q_in, k_up, v, dec_row, dec_col, st_ref, nh, n, p, issue):
    L = q_sc.shape[0]
    k_big = _stack_heads(k_sc, nh, n)
    sc = _dot_nt(q_sc.astype(BF16), k_big)
    st = st_ref[...]
    rowhead = _iota(st.shape, 0) >> _log2(n)
    st_bd = jnp.concatenate([jnp.where(rowhead == h, st, 0.0).astype(BF16) for h in range(nh)], axis=1)
    inter = _dot(q_in.astype(BF16), st_bd)
    rows = [k_up]
    if dec_col is None:
        rows.append(jnp.broadcast_to(dec_row, (SUBLANES, nh * n)))
    pad = LANES - L - (len(rows) - 1) * SUBLANES
    assert pad >= 0
    if pad:
        rows.append(jnp.zeros((pad, nh * n), F32))
    kt = jnp.concatenate(rows, axis=0).T if len(rows) > 1 else k_up.T
    issue()
    lane = _iota(sc.shape, 1)
    row = _iota(sc.shape, 0)
    causal = (lane & (L - 1)) <= row
    sc = jnp.where(causal, sc if sc_mul is None else sc * sc_mul, 0.0)
    sc = sc.astype(BF16)
    if L % LANES == 0:
        intra = jnp.concatenate(
            [_dot(sc[:, h * L:(h + 1) * L], v[:, h * p:(h + 1) * p].astype(BF16)) for h in range(nh)], axis=1)
    else:
        intra = _dot(sc, _stack_heads(v, nh, p))
    o = intra + inter
    if dec_col is None:
        dec_col = kt[:, L:L + 1]
    kt = kt.astype(BF16)
    upd = []
    for h in range(nh):
        v_h = v[:, h * p:(h + 1) * p].astype(BF16)
        if L < LANES:
            v_h = jnp.concatenate([v_h, jnp.zeros((LANES - L, p), BF16)], axis=0)
        upd.append(_dot(kt[h * n:(h + 1) * n, :], v_h))
    st_ref[...] = st * dec_col + jnp.concatenate(upd, axis=0)
    return o


def _group_norm_heads(o, nh, p):
    outs = []
    for h in range(nh):
        oh = o[:, h * p:(h + 1) * p]
        mu = jnp.mean(oh, axis=-1, keepdims=True)
        dlt = oh - mu
        var = jnp.mean(dlt * dlt, axis=-1, keepdims=True)
        outs.append(dlt * lax.rsqrt(var + EPS))
    return jnp.concatenate(outs, axis=1)


def _block_tril(tt, L):
    row = _iota((tt, tt), 0)
    col = _iota((tt, tt), 1)
    k = _log2(L)
    return jnp.where((col <= row) & ((row >> k) == (col >> k)), 1.0, 0.0).astype(BF16)


def _ret_log_gamma(head, nh):
    out = jnp.full(head.shape, math.log1p(-2.0 ** (-5.0 - (nh - 1))), F32)
    for h in range(nh - 1):
        out = jnp.where(head == h, math.log1p(-2.0 ** (-5.0 - h)), out)
    return out


def _ssd_decay_terms(b8, dtc, dm):
    L = b8.shape[0]
    bl8 = b8[L - 1:L, :]
    w8 = dtc * jnp.exp(bl8 - b8)
    b8t = _transpose_rows(b8)
    dtt = _transpose_rows(dtc)
    lane8 = _iota(b8.shape, 1)
    ebl8 = jnp.exp(bl8)
    bcols, eb_cols, w_cols, dec_cols = [], [], [], []
    for hh in range(dm.sh):
        bcol = jnp.sum(jnp.where(lane8 == hh, b8, 0.0), axis=-1, keepdims=True)
        bcols.append(bcol)
        eb_cols.append(jnp.broadcast_to(jnp.exp(bcol), (L, LANES)))
        w_cols.append(jnp.broadcast_to(
            jnp.sum(jnp.where(lane8 == hh, w8, 0.0), axis=-1, keepdims=True), (L, LANES)))
        dec_cols.append(jnp.broadcast_to(
            jnp.sum(jnp.where(_iota(bl8.shape, 1) == hh, ebl8, 0.0), axis=-1, keepdims=True), (1, LANES)))

    def pair_lanes(cols):
        first = _iota(cols[0].shape, 1) < dm.sp
        return jnp.concatenate(
            [jnp.where(first, cols[2 * j], cols[2 * j + 1]) for j in range(dm.sh // 2)], axis=1)

    return bcols, b8t, dtt, pair_lanes(eb_cols), pair_lanes(w_cols), pair_lanes(dec_cols)


def _retention_tables(dm, L):
    nlane = dm.rh * dm.rn
    lgam_q = _ret_log_gamma(_iota((L, nlane), 1) >> _log2(dm.rn), dm.rh)
    pos_q = _iota((L, nlane), 0).astype(F32)
    q_dec = jnp.exp((pos_q + 1.0) * lgam_q)
    k_dec = jnp.exp((L - 1.0 - pos_q) * lgam_q)
    lgam_r = _ret_log_gamma(_iota((nlane, dm.rp), 0) >> _log2(dm.rn), dm.rh)
    st_dec = jnp.exp(L * lgam_r)
    lgam_s = _ret_log_gamma(_iota((L, dm.rh * L), 1) >> _log2(L), dm.rh)
    dpos = (_iota((L, dm.rh * L), 0) - (_iota((L, dm.rh * L), 1) & (L - 1))).astype(F32)
    sc_dec = jnp.exp(jnp.maximum(dpos, 0.0) * lgam_s)
    return q_dec, k_dec, st_dec, sc_dec


def _transpose_rows(x):
    L = x.shape[0]
    if L % LANES:
        pad = LANES - L % LANES
        x = jnp.concatenate([x, jnp.zeros((pad, x.shape[1]), x.dtype)], axis=0)
    return x.T[:, :L]


def _swap_halves(x, n):
    width = x.shape[1]
    half = n // 2
    first = (_iota(x.shape, 1) & (n - 1)) < half
    return jnp.where(first, pltpu.roll(x, width - half, 1), pltpu.roll(x, half, 1))


def _mixer_kernel(x_ref, h_ref, cos_ref, sin_ref, sg0_ref, ss0_ref, cv0_ref, sr0_ref,
                  win_ref, wg2_ref, bg_ref, gnorm_ref, cw_ref, cb_ref, dtb_ref, alog_ref,
                  dexp_ref, snorm_ref, rnorm_ref, wbg_ref, wbs_ref, wbr_ref, wout_ref,
                  y_ref, sg_ref, ss_ref, cv_ref, sr_ref,
                  stg, sts, cbuf, strt, ogla, ossd, oret,
                  *, dm, tt, lg, ls, lr):
    t = pl.program_id(1)
    nt = pl.num_programs(1)
    ck = dm.conv_k - 1

    @pl.when(t == 0)
    def _():
        stg[...] = sg0_ref[...]
        strt[...] = sr0_ref[...]
        z0 = ss0_ref[...]
        hpg = dm.sh // dm.sg
        sts[...] = jnp.concatenate(
            [jnp.where(_head_lane_mask(z0.shape, g, dm.sp * hpg), z0, 0.0) for g in range(dm.sg)], axis=0)
        cbuf[SUBLANES - ck:SUBLANES, :] = cv0_ref[...]

    pending = []
    for name in ["lrdt", "mxbc", "gq", "gk", "gv", "gr", "mz", "rq", "rk", "rv", "rg", "ga", "gb", "gc"]:
        a, b = dm.seg[name]
        pending += [(name, s) for s in range(a, b, MXU_COLS)]
    parts = {name: [] for name in dm.seg_names}

    def issue(n=1):
        for _ in range(n):
            if pending:
                name, s = pending.pop(0)
                parts[name].append(_dot(h_ref[...], win_ref[:, s:s + MXU_COLS]))

    every = max(1, tt // MXU_COLS)
    calls = [0]

    def issue_thin(n=1):
        for _ in range(n):
            calls[0] += 1
            if calls[0] % every == 0:
                issue()

    def take(name):
        a, b = dm.seg[name]
        while len(parts[name]) * MXU_COLS < b - a:
            issue()
        return parts[name][0] if len(parts[name]) == 1 else jnp.concatenate(parts[name], axis=1)

    ret_tabs = _retention_tables(dm, lr)
    lrdt = take("lrdt")
    xbc_pre = take("mxbc")
    gate = _dot(lrdt[:, :LANES].astype(BF16), wg2_ref[...]) + bg_ref[...]
    cbuf[SUBLANES:SUBLANES + tt, :] = xbc_pre
    conv = cb_ref[...] + cw_ref[ck:ck + 1, :] * xbc_pre
    for j in range(ck):
        conv = conv + cw_ref[j:j + 1, :] * cbuf[SUBLANES - ck + j:SUBLANES - ck + j + tt, :]
    new_carry = cbuf[SUBLANES + tt - ck:SUBLANES + tt, :]
    cbuf[SUBLANES - ck:SUBLANES, :] = new_carry
    q = take("gq") * (dm.gn ** -0.5)
    k = take("gk")
    xbc = _silu(conv)
    xs = xbc[:, :dm.s_inner]
    bm = xbc[:, dm.s_inner:dm.s_inner + LANES]
    cm = xbc[:, dm.s_inner + LANES:dm.s_inner + 2 * LANES]
    dt8 = _softplus(lrdt[:, LANES:] + dtb_ref[...])
    a8 = dt8 * (-jnp.exp(alog_ref[...]))
    b8_all = _chunk_cumsum(a8, ls)
    issue(3)
    log_a = _log_sigmoid(gate) * (1.0 / GLA_TAU)
    b_all = _chunk_cumsum(log_a, lg)
    issue(3)
    qd_all = q * jnp.exp(b_all)
    kd_all = k * jnp.exp(-b_all)
    v = take("gv").astype(BF16)
    for c in range(tt // lg):
        r = slice(c * lg, (c + 1) * lg)
        b = b_all[r]
        bl = b[lg - 1:lg, :]
        ks = k[r] * jnp.exp(bl - b)
        ogla[r, :] = _linear_attention_chunk(qd_all[r], kd_all[r], None, qd_all[r], ks, v[r],
                                             jnp.exp(bl), None, stg, dm.gh, dm.gn, dm.gp,
                                             issue_thin if c else (lambda: None))
    y_gla = (_group_norm_heads(ogla[...], dm.gh, dm.gp) * gnorm_ref[...] * _silu(take("gr"))).astype(BF16)
    issue(2)

    hpg = dm.sh // dm.sg
    for c in range(tt // ls):
        r = slice(c * ls, (c + 1) * ls)
        bcols, b8t, dtt, eb_exp, w_exp, dec_row = _ssd_decay_terms(b8_all[r], dt8[r], dm)
        xs_c = xs[r]
        xs_bf = xs_c.astype(BF16)
        cm_c = cm[r]
        bm_c = bm[r].astype(BF16)
        causal = _iota((ls, ls), 1) <= _iota((ls, ls), 0)
        gmats = [_dot_nt(jnp.where(_head_lane_mask(cm_c.shape, g, dm.sn), cm_c, 0.0).astype(BF16), bm_c)
                 for g in range(dm.sg)]
        st = sts[...]
        y_inter = _dot(cm_c.astype(BF16), st.astype(BF16)) * eb_exp
        upd = _dot_tn(bm_c, (xs_c * w_exp).astype(BF16))
        blk = (_iota(upd.shape, 0) >> _log2(dm.sn)) == (_iota(upd.shape, 1) >> _log2(dm.sp * hpg))
        sts[...] = st * dec_row + jnp.where(blk, upd, 0.0)
        issue_thin(2)
        y_pairs = []
        for hh in range(dm.sh):
            seg = jnp.where(causal, bcols[hh] - b8t[hh:hh + 1, :], -jnp.inf)
            m_h = gmats[hh // hpg] * dtt[hh:hh + 1, :] * jnp.exp(seg)
            pr = hh // 2
            y_pairs.append(_dot(m_h.astype(BF16), xs_bf[:, pr * LANES:(pr + 1) * LANES]))
            if hh % 2:
                issue_thin()
        first = _iota((ls, LANES), 1) < dm.sp
        y_intra = jnp.concatenate(
            [jnp.where(first, y_pairs[2 * j], y_pairs[2 * j + 1]) for j in range(dm.sh // 2)], axis=1)
        ossd[r, :] = y_intra + y_inter
    y = ossd[...] + dexp_ref[...] * xs
    y = y * _silu(take("mz"))
    issue()
    y_ssd = _rms(y, snorm_ref[...]).astype(BF16)

    late = [(y_gla, wbg_ref, s) for s in range(0, dm.d, MXU_COLS)]
    late += [(y_ssd, wbs_ref, s) for s in range(0, dm.d, MXU_COLS)]
    yb_parts = []

    def issue_late(n=1):
        for _ in range(n):
            if late:
                yv, w_ref, s = late.pop(0)
                yb_parts.append(_dot(yv, w_ref[:, s:s + MXU_COLS]))

    cos = cos_ref[...]
    sin = sin_ref[...]
    rq_pre = take("rq")
    rk_pre = take("rk")
    issue(len(pending))
    rq = rq_pre * cos + _swap_halves(rq_pre, dm.rn) * sin
    issue_late()
    rk = (rk_pre * cos + _swap_halves(rk_pre, dm.rn) * sin) * (dm.rn ** -0.5)
    issue_late()
    rv = take("rv").astype(BF16)
    q_dec, k_dec, st_dec, sc_dec = ret_tabs
    for c in range(tt // lr):
        r = slice(c * lr, (c + 1) * lr)
        oret[r, :] = _linear_attention_chunk(rq[r], rk[r], sc_dec, rq[r] * q_dec, rk[r] * k_dec, rv[r],
                                             None, st_dec, strt, dm.rh, dm.rn, dm.rp, issue_late)
        issue_late()
    issue_late(len(late))
    ncol = dm.d // MXU_COLS
    yb_gla = jnp.concatenate(yb_parts[:ncol], axis=1)
    yb_ssd = jnp.concatenate(yb_parts[ncol:], axis=1)
    ga, gb, gc = take("ga"), take("gb"), take("gc")
    m2 = jax.nn.sigmoid(ga) * yb_gla + jax.nn.sigmoid(gb) * yb_ssd
    y_ret = (_group_norm_heads(oret[...], dm.rh, dm.rp) * rnorm_ref[...] * _silu(take("rg"))).astype(BF16)

    halves = 2 if tt % (2 * SUBLANES * 2) == 0 else 1
    hr = tt // halves
    for i in range(halves):
        r = slice(i * hr, (i + 1) * hr)
        m = m2[r] + jax.nn.sigmoid(gc[r]) * _dot(y_ret[r], wbr_ref[...])
        y_ref[r, :] = x_ref[r, :] + _dot(m.astype(BF16), wout_ref[...])

    @pl.when(t == nt - 1)
    def _():
        sg_ref[...] = stg[...]
        sr_ref[...] = strt[...]
        sz = sts[...]
        acc = sz[0:dm.sn]
        for g in range(1, dm.sg):
            acc = acc + sz[g * dm.sn:(g + 1) * dm.sn]
        ss_ref[...] = acc
        cv_ref[...] = cbuf[SUBLANES - ck:SUBLANES, :]


def _mixer(x, h, cos, sin, sg0, ss0, cv0, sr0, wts, dm, *, tt, lg, ls, lr):
    bsz, seq, d = x.shape
    assert seq % tt == 0 and tt % lg == 0 and tt % ls == 0 and tt % lr == 0
    kern = functools.partial(_mixer_kernel, dm=dm, tt=tt, lg=lg, ls=ls, lr=lr)
    gq = dm.gh * dm.gn
    rq = dm.rh * dm.rn

    def per_b(shape):
        nd = len(shape)
        return pl.BlockSpec((None,) + shape, lambda b, t: (b,) + (0,) * nd)

    in_specs = [
        pl.BlockSpec((None, tt, d), lambda b, t: (b, t, 0)),
        pl.BlockSpec((None, tt, d), lambda b, t: (b, t, 0)),
        pl.BlockSpec((tt, rq), lambda b, t: (t, 0)),
        pl.BlockSpec((tt, rq), lambda b, t: (t, 0)),
        per_b((gq, dm.gp)), per_b((dm.sn, dm.s_inner)), per_b((dm.conv_k - 1, dm.conv_dim)), per_b((rq, dm.rp)),
    ] + [_const_spec(w.shape) for w in wts]
    out_specs = [
        pl.BlockSpec((None, tt, d), lambda b, t: (b, t, 0)),
        per_b((gq, dm.gp)), per_b((dm.sn, dm.s_inner)), per_b((dm.conv_k - 1, dm.conv_dim)), per_b((rq, dm.rp)),
    ]
    out_shape = [
        jax.ShapeDtypeStruct((bsz, seq, d), F32),
        jax.ShapeDtypeStruct((bsz, gq, dm.gp), F32),
        jax.ShapeDtypeStruct((bsz, dm.sn, dm.s_inner), F32),
        jax.ShapeDtypeStruct((bsz, dm.conv_k - 1, dm.conv_dim), F32),
        jax.ShapeDtypeStruct((bsz, rq, dm.rp), F32),
    ]
    scratch = [
        pltpu.VMEM((gq, dm.gp), F32),
        pltpu.VMEM((dm.sg * dm.sn, dm.s_inner), F32),
        pltpu.VMEM((tt + SUBLANES, dm.conv_dim), F32),
        pltpu.VMEM((rq, dm.rp), F32),
        pltpu.VMEM((tt, dm.gh * dm.gp), F32),
        pltpu.VMEM((tt, dm.s_inner), F32),
        pltpu.VMEM((tt, dm.rh * dm.rp), F32),
    ]
    return pl.pallas_call(
        kern,
        grid=(bsz, seq // tt),
        in_specs=in_specs,
        out_specs=out_specs,
        out_shape=out_shape,
        scratch_shapes=scratch,
        compiler_params=pltpu.CompilerParams(
            dimension_semantics=("arbitrary", "arbitrary"), vmem_limit_bytes=VMEM_LIMIT_BYTES),
        name="mixer",
    )(x, h, cos, sin, sg0, ss0, cv0, sr0, *wts)


def _pad_cols(w, n):
    return jnp.pad(w, ((0, 0), (0, n - w.shape[1])))


def _row(v, n=None):
    v = v.reshape(1, -1).astype(F32)
    return v if n is None else _pad_cols(v, n)


def _rope_tables(pos, dm):
    half = dm.rn // 2
    freqs = ROPE_BASE ** (-jnp.arange(half, dtype=F32) / half)
    ang = pos.astype(F32)[:, None] * freqs[None, :]
    cos = jnp.cos(ang)
    sin = jnp.sin(ang)
    cos_h = jnp.concatenate([cos, cos], axis=1)
    sin_h = jnp.concatenate([-sin, sin], axis=1)
    return jnp.tile(cos_h, (1, dm.rh)), jnp.tile(sin_h, (1, dm.rh))


def _state_to_kernel_t(s):
    b, h, n, p = s.shape
    return s.reshape(b, h * n, p)


def _state_from_kernel_t(s, h):
    b, hn, p = s.shape
    return s.reshape(b, h, hn // h, p)


def _ssd_to_kernel(s):
    b, h, n, p = s.shape
    return jnp.transpose(s, (0, 2, 1, 3)).reshape(b, n, h * p)


def _ssd_from_kernel(s, h):
    b, n, hp = s.shape
    return jnp.transpose(s.reshape(b, n, h, hp // h), (0, 2, 1, 3))


def _tiles(seq):
    if seq % SCAN_CHUNK == 0:
        tt = 512 if seq % 512 == 0 else SCAN_CHUNK
        return dict(tt=tt, lg=SCAN_CHUNK, ls=min(tt, 128), lr=min(tt, 128))
    return dict(tt=seq, lg=seq, ls=seq, lr=seq)


def kernel(x_prompt, x_sample, state_gla, state_ssd, cache_conv, state_ret, norm_ffn1, ffn1_w_in, ffn1_w_out,
           norm_mix, w_in, gla_w_gate2, gla_b_gate, gla_norm, ssd_conv_w, ssd_conv_b, ssd_dt_bias, ssd_a_log,
           ssd_d, ssd_norm, ret_norm, w_branch_gla, w_branch_ssd, w_branch_ret, w_out, norm_ffn2, ffn2_w_in,
           ffn2_w_out, norm_final):
    depth = w_in.shape[0]
    d = x_prompt.shape[-1]
    d_ff = ffn1_w_out.shape[1]
    dm = _Dims(state_gla, state_ssd, cache_conv, state_ret, gla_w_gate2, ssd_conv_w, d)

    split_sizes = (dm.gh * dm.gn, dm.gh * dm.gn, dm.gh * dm.gp, dm.gh * dm.gp, dm.rank,
                   dm.s_inner, dm.conv_dim, dm.sh,
                   dm.rh * dm.rn, dm.rh * dm.rn, dm.rh * dm.rp, dm.rh * dm.rp, d, d, d)
    pts, acc = [], 0
    for s in split_sizes[:-1]:
        acc += s
        pts.append(acc)

    def mixer_weights(l):
        (s_gq, s_gk, s_gv, s_gr, s_glr, s_mz, s_mxbc, s_mdt,
         s_rq, s_rk, s_rv, s_rg, s_ga, s_gb, s_gc) = jnp.split(w_in[l], pts, axis=1)
        win = jnp.concatenate(
            [s_gq, s_gk, _pad_cols(s_glr, LANES), _pad_cols(s_mdt, LANES), s_gv, s_gr, s_mz, s_mxbc,
             s_rq, s_rk, s_rv, s_rg, s_ga, s_gb, s_gc], axis=1).astype(BF16)
        assert win.shape[1] == dm.in_cols
        wg2 = jnp.pad(gla_w_gate2[l], ((0, LANES - dm.rank), (0, 0))).astype(BF16)
        return [
            win, wg2, _row(gla_b_gate[l]), _row(gla_norm[l]),
            ssd_conv_w[l].astype(F32), _row(ssd_conv_b[l]), _row(ssd_dt_bias[l], LANES), _row(ssd_a_log[l], LANES),
            _row(jnp.repeat(ssd_d[l], dm.sp)), _row(ssd_norm[l]), _row(ret_norm[l]),
            w_branch_gla[l].astype(BF16), w_branch_ssd[l].astype(BF16), w_branch_ret[l].astype(BF16),
            w_out[l].astype(BF16),
        ]

    def ffn_weights(nrm, w_i, w_o, l):
        return _row(nrm[l]), w_i[l][:, :d_ff].astype(BF16), w_i[l][:, d_ff:].astype(BF16), w_o[l].astype(BF16)

    g_final = _row(norm_final)

    def run_group(x, pos, states):
        bsz, seq, _ = x.shape
        m = bsz * seq
        tm = 1024 if m % 1024 == 0 else m
        cos, sin = _rope_tables(pos, dm)
        tl = _tiles(seq)
        new_states = []
        for l in range(depth):
            sg0, ss0, cv0, sr0 = states(l)
            g1, wg1, wu1, wo1 = ffn_weights(norm_ffn1, ffn1_w_in, ffn1_w_out, l)
            x, h = _ffn(x.reshape(m, d), g1, wg1, wu1, wo1, _row(norm_mix[l]), tm=tm, fc=256, post="emit")
            x, sg, ss, cv, sr = _mixer(x.reshape(bsz, seq, d), h.reshape(bsz, seq, d), cos, sin,
                                       _state_to_kernel_t(sg0), _ssd_to_kernel(ss0), cv0,
                                       _state_to_kernel_t(sr0), mixer_weights(l), dm, **tl)
            g2, wg2_, wu2, wo2 = ffn_weights(norm_ffn2, ffn2_w_in, ffn2_w_out, l)
            last = l == depth - 1
            x = _ffn(x.reshape(m, d), g2, wg2_, wu2, wo2, g_final if last else None, tm=tm, fc=256,
                     post="norm" if last else None).reshape(bsz, seq, d)
            new_states.append((_state_from_kernel_t(sg, dm.gh), _ssd_from_kernel(ss, dm.sh), cv,
                               _state_from_kernel_t(sr, dm.rh)))
        return x, new_states

    bp, tp = x_prompt.shape[:2]
    ts = x_sample.shape[1]
    dt_ = x_prompt.dtype

    def zero_states(_):
        return (jnp.zeros((bp, dm.gh, dm.gn, dm.gp), dt_), jnp.zeros((bp, dm.sh, dm.sn, dm.sp), dt_),
                jnp.zeros((bp, dm.conv_k - 1, dm.conv_dim), dt_), jnp.zeros((bp, dm.rh, dm.rn, dm.rp), dt_))

    def carried_states(l):
        return state_gla[l], state_ssd[l], cache_conv[l], state_ret[l]

    y_p, st_p = run_group(x_prompt, jnp.arange(tp, dtype=jnp.int32), zero_states)
    y_s, st_s = run_group(x_sample, PAST_LEN + jnp.arange(ts, dtype=jnp.int32), carried_states)

    def stack(sts, i):
        return jnp.stack([s[i] for s in sts])

    return (y_p, y_s, stack(st_p, 0), stack(st_p, 1), stack(st_p, 2), stack(st_p, 3),
            stack(st_s, 0), stack(st_s, 1), stack(st_s, 2), stack(st_s, 3))
```

```python
import functools
import math

import jax
import jax.numpy as jnp
from jax import lax
from jax.experimental import pallas as pl
from jax.experimental.pallas import tpu as pltpu

F32 = jnp.float32
BF16 = jnp.bfloat16

EPS = 1e-6
PAST_LEN = 1024
GLA_TAU = 16.0
ROPE_BASE = 10000.0
SCAN_CHUNK = 64

LANES = 128
SUBLANES = 8
MXU_COLS = 256
FFN_OUT_ROWS = 256
VMEM_LIMIT_BYTES = 56 * 1024 * 1024


def _dot(a, b):
    return jnp.dot(a, b, preferred_element_type=F32)


def _dot_nt(a, b):
    return lax.dot_general(a, b, (((1,), (1,)), ((), ())), preferred_element_type=F32)


def _dot_tn(a, b):
    return lax.dot_general(a, b, (((0,), (0,)), ((), ())), preferred_element_type=F32)


def _split3(x):
    hi = x.astype(BF16)
    r1 = x - hi.astype(F32)
    mid = r1.astype(BF16)
    lo = (r1 - mid.astype(F32)).astype(BF16)
    return hi, mid, lo


def _dot_exact01(m01, x):
    hi, mid, lo = _split3(x)
    return _dot(m01, hi) + _dot(m01, mid) + _dot(m01, lo)


def _chunk_cumsum(x, L):
    T = x.shape[0]
    rows = MXU_COLS if (T % MXU_COLS == 0 and MXU_COLS % L == 0) else T
    tril = _block_tril(rows, L)
    out = [_dot_exact01(tril, x[i:i + rows]) for i in range(0, T, rows)]
    return out[0] if len(out) == 1 else jnp.concatenate(out, axis=0)


def _rms(x, g):
    return x * lax.rsqrt(jnp.mean(x * x, axis=-1, keepdims=True) + EPS) * g


def _silu(x):
    return x * jax.nn.sigmoid(x)


def _softplus(x):
    return jnp.maximum(x, 0.0) + jnp.log(1.0 + jnp.exp(-jnp.abs(x)))


def _log_sigmoid(x):
    return jnp.minimum(x, 0.0) - jnp.log(1.0 + jnp.exp(-jnp.abs(x)))


def _iota(shape, axis):
    return lax.broadcasted_iota(jnp.int32, shape, axis)


def _log2(n):
    k = int(math.log2(n))
    assert (1 << k) == n, f"{n} must be a power of two"
    return k


def _ffn_kernel(*refs, fc, post):
    if post is None:
        x_ref, g_ref, wg_ref, wu_ref, wo_ref, o_ref, h_ref, a_ref = refs
    elif post == "norm":
        x_ref, g_ref, wg_ref, wu_ref, wo_ref, g2_ref, o_ref, h_ref, a_ref = refs
    else:
        x_ref, g_ref, wg_ref, wu_ref, wo_ref, g2_ref, o_ref, hn_ref, h_ref, a_ref = refs
    x = x_ref[...]
    h_ref[...] = _rms(x, g_ref[...]).astype(BF16)
    d_ff = wg_ref.shape[1]
    for c in range(d_ff // fc):
        sl = slice(c * fc, (c + 1) * fc)
        gate = _dot(h_ref[...], wg_ref[:, sl])
        up = _dot(h_ref[...], wu_ref[:, sl])
        a_ref[:, sl] = (_silu(gate) * up).astype(BF16)
    tm = x_ref.shape[0]
    rb = FFN_OUT_ROWS if tm % FFN_OUT_ROWS == 0 else tm
    for i in range(tm // rb):
        r = slice(i * rb, (i + 1) * rb)
        y = x_ref[r, :] + 0.5 * _dot(a_ref[r, :], wo_ref[...])
        if post == "norm":
            y = _rms(y, g2_ref[...])
        o_ref[r, :] = y
        if post == "emit":
            hn_ref[r, :] = _rms(y, g2_ref[...]).astype(BF16)


def _const_spec(shape):
    nd = len(shape)
    return pl.BlockSpec(shape, lambda *_: (0,) * nd, pipeline_mode=pl.Buffered(1))


def _ffn(x2d, g, w_gate, w_up, w_out, g2, *, tm, fc, post):
    m, d = x2d.shape
    d_ff = w_gate.shape[1]
    assert m % tm == 0 and d_ff % fc == 0 and (g2 is None) == (post is None)
    kern = functools.partial(_ffn_kernel, fc=fc, post=post)
    row_spec = pl.BlockSpec((tm, d), lambda i: (i, 0))
    in_specs = [row_spec, _const_spec((1, d)), _const_spec((d, d_ff)), _const_spec((d, d_ff)),
                _const_spec((d_ff, d))]
    args = [x2d, g, w_gate, w_up, w_out]
    if post is not None:
        in_specs.append(_const_spec((1, d)))
        args.append(g2)
    out_specs, out_shape = row_spec, jax.ShapeDtypeStruct((m, d), F32)
    if post == "emit":
        out_specs, out_shape = [row_spec, row_spec], [out_shape, jax.ShapeDtypeStruct((m, d), BF16)]
    return pl.pallas_call(
        kern,
        grid=(m // tm,),
        in_specs=in_specs,
        out_specs=out_specs,
        out_shape=out_shape,
        scratch_shapes=[pltpu.VMEM((tm, d), BF16), pltpu.VMEM((tm, d_ff), BF16)],
        compiler_params=pltpu.CompilerParams(
            dimension_semantics=("arbitrary",), vmem_limit_bytes=VMEM_LIMIT_BYTES),
        name="ffn",
    )(*args)


class _Dims:
    def __init__(self, state_gla, state_ssd, cache_conv, state_ret, gla_w_gate2, ssd_conv_w, d_model):
        _, _, self.gh, self.gn, self.gp = state_gla.shape
        _, _, self.sh, self.sn, self.sp = state_ssd.shape
        _, _, self.rh, self.rn, self.rp = state_ret.shape
        self.conv_k = ssd_conv_w.shape[1]
        self.conv_dim = cache_conv.shape[-1]
        self.rank = gla_w_gate2.shape[1]
        self.d = d_model
        self.s_inner = self.sh * self.sp
        self.sg = (self.conv_dim - self.s_inner) // (2 * self.sn)
        assert self.gh * self.gn == 2 * LANES and self.gp == LANES
        assert self.rh * self.rn == 2 * LANES and self.rp == LANES
        assert self.sp * 2 == LANES and self.sg * self.sn == LANES and self.sh <= LANES
        assert self.rank <= LANES and self.conv_k - 1 <= SUBLANES
        w = [self.gh * self.gn, self.gh * self.gn, 2 * LANES, self.gh * self.gp, self.gh * self.gp,
             self.s_inner, self.conv_dim,
             self.rh * self.rn, self.rh * self.rn, self.rh * self.rp, self.rh * self.rp,
             d_model, d_model, d_model]
        self.seg_names = ["gq", "gk", "lrdt", "gv", "gr", "mz", "mxbc", "rq", "rk", "rv", "rg", "ga", "gb", "gc"]
        self.seg = {}
        off = 0
        for n, wd in zip(self.seg_names, w):
            assert wd % MXU_COLS == 0
            self.seg[n] = (off, off + wd)
            off += wd
        self.in_cols = off


def _head_lane_mask(shape, h, width):
    lane = _iota(shape, 1)
    return (lane >> _log2(width)) == h


def _stack_heads(x, nh, width):
    parts = [jnp.where(_head_lane_mask(x.shape, h, width), x, 0.0).astype(BF16) for h in range(nh)]
    return jnp.concatenate(parts, axis=0)


def _linear_attention_chunk(q_sc, k_sc, sc_mul, q_in, k_up, v, dec_row, dec_col, st_ref, nh, n, p, issue):
    L = q_sc.shape[0]
    k_big = _stack_heads(k_sc, nh, n)
    sc = _dot_nt(q_sc.astype(BF16), k_big)
    st = st_ref[...]
    rowhead = _iota(st.shape, 0) >> _log2(n)
    st_bd = jnp.concatenate([jnp.where(rowhead == h, st, 0.0).astype(BF16) for h in range(nh)], axis=1)
    inter = _dot(q_in.astype(BF16), st_bd)
    rows = [k_up]
    if dec_col is None:
        rows.append(jnp.broadcast_to(dec_row, (SUBLANES, nh * n)))
    pad = LANES - L - (len(rows) - 1) * SUBLANES
    assert pad >= 0
    if pad:
        rows.append(jnp.zeros((pad, nh * n), F32))
    kt = jnp.concatenate(rows, axis=0).T if len(rows) > 1 else k_up.T
    issue()
    lane = _iota(sc.shape, 1)
    row = _iota(sc.shape, 0)
    causal = (lane & (L - 1)) <= row
    sc = jnp.where(causal, sc if sc_mul is None else sc * sc_mul, 0.0)
    sc = sc.astype(BF16)
    if L % LANES == 0:
        intra = jnp.concatenate(
            [_dot(sc[:, h * L:(h + 1) * L], v[:, h * p:(h + 1) * p].astype(BF16)) for h in range(nh)], axis=1)
    else:
        intra = _dot(sc, _stack_heads(v, nh, p))
    o = intra + inter
    if dec_col is None:
        dec_col = kt[:, L:L + 1]
    kt = kt.astype(BF16)
    upd = []
    for h in range(nh):
        v_h = v[:, h * p:(h + 1) * p].astype(BF16)
        if L < LANES:
            v_h = jnp.concatenate([v_h, jnp.zeros((LANES - L, p), BF16)], axis=0)
        upd.append(_dot(kt[h * n:(h + 1) * n, :], v_h))
    st_ref[...] = st * dec_col + jnp.concatenate(upd, axis=0)
    return o


def _group_norm_heads(o, nh, p):
    outs = []
    for h in range(nh):
        oh = o[:, h * p:(h + 1) * p]
        mu = jnp.mean(oh, axis=-1, keepdims=True)
        dlt = oh - mu
        var = jnp.mean(dlt * dlt, axis=-1, keepdims=True)
        outs.append(dlt * lax.rsqrt(var + EPS))
    return jnp.concatenate(outs, axis=1)


def _block_tril(tt, L):
    row = _iota((tt, tt), 0)
    col = _iota((tt, tt), 1)
    k = _log2(L)
    return jnp.where((col <= row) & ((row >> k) == (col >> k)), 1.0, 0.0).astype(BF16)


def _ret_log_gamma(head, nh):
    out = jnp.full(head.shape, math.log1p(-2.0 ** (-5.0 - (nh - 1))), F32)
    for h in range(nh - 1):
        out = jnp.where(head == h, math.log1p(-2.0 ** (-5.0 - h)), out)
    return out


def _ssd_decay_terms(b8, dtc, dm):
    L = b8.shape[0]
    bl8 = b8[L - 1:L, :]
    w8 = dtc * jnp.exp(bl8 - b8)
    b8t = _transpose_rows(b8)
    dtt = _transpose_rows(dtc)
    lane8 = _iota(b8.shape, 1)
    ebl8 = jnp.exp(bl8)
    bcols, eb_cols, w_cols, dec_cols = [], [], [], []
    for hh in range(dm.sh):
        bcol = jnp.sum(jnp.where(lane8 == hh, b8, 0.0), axis=-1, keepdims=True)
        bcols.append(bcol)
        eb_cols.append(jnp.broadcast_to(jnp.exp(bcol), (L, LANES)))
        w_cols.append(jnp.broadcast_to(
            jnp.sum(jnp.where(lane8 == hh, w8, 0.0), axis=-1, keepdims=True), (L, LANES)))
        dec_cols.append(jnp.broadcast_to(
            jnp.sum(jnp.where(_iota(bl8.shape, 1) == hh, ebl8, 0.0), axis=-1, keepdims=True), (1, LANES)))

    def pair_lanes(cols):
        first = _iota(cols[0].shape, 1) < dm.sp
        return jnp.concatenate(
            [jnp.where(first, cols[2 * j], cols[2 * j + 1]) for j in range(dm.sh // 2)], axis=1)

    return bcols, b8t, dtt, pair_lanes(eb_cols), pair_lanes(w_cols), pair_lanes(dec_cols)


def _retention_tables(dm, L):
    nlane = dm.rh * dm.rn
    lgam_q = _ret_log_gamma(_iota((L, nlane), 1) >> _log2(dm.rn), dm.rh)
    pos_q = _iota((L, nlane), 0).astype(F32)
    q_dec = jnp.exp((pos_q + 1.0) * lgam_q)
    k_dec = jnp.exp((L - 1.0 - pos_q) * lgam_q)
    lgam_r = _ret_log_gamma(_iota((nlane, dm.rp), 0) >> _log2(dm.rn), dm.rh)
    st_dec = jnp.exp(L * lgam_r)
    lgam_s = _ret_log_gamma(_iota((L, dm.rh * L), 1) >> _log2(L), dm.rh)
    dpos = (_iota((L, dm.rh * L), 0) - (_iota((L, dm.rh * L), 1) & (L - 1))).astype(F32)
    sc_dec = jnp.exp(jnp.maximum(dpos, 0.0) * lgam_s)
    return q_dec, k_dec, st_dec, sc_dec


def _transpose_rows(x):
    L = x.shape[0]
    if L % LANES:
        pad = LANES - L % LANES
        x = jnp.concatenate([x, jnp.zeros((pad, x.shape[1]), x.dtype)], axis=0)
    return x.T[:, :L]


def _swap_halves(x, n):
    width = x.shape[1]
    half = n // 2
    first = (_iota(x.shape, 1) & (n - 1)) < half
    return jnp.where(first, pltpu.roll(x, width - half, 1), pltpu.roll(x, half, 1))


def _mixer_kernel(x_ref, h_ref, cos_ref, sin_ref, sg0_ref, ss0_ref, cv0_ref, sr0_ref,
                  win_ref, wg2_ref, bg_ref, gnorm_ref, cw_ref, cb_ref, dtb_ref, alog_ref,
                  dexp_ref, snorm_ref, rnorm_ref, wbg_ref, wbs_ref, wbr_ref, wout_ref,
                  y_ref, sg_ref, ss_ref, cv_ref, sr_ref,
                  stg, sts, cbuf, strt, ogla, ossd, oret,
                  *, dm, tt, lg, ls, lr):
    t = pl.program_id(1)
    nt = pl.num_programs(1)
    ck = dm.conv_k - 1

    @pl.when(t == 0)
    def _():
        stg[...] = sg0_ref[...]
        strt[...] = sr0_ref[...]
        z0 = ss0_ref[...]
        hpg = dm.sh // dm.sg
        sts[...] = jnp.concatenate(
            [jnp.where(_head_lane_mask(z0.shape, g, dm.sp * hpg), z0, 0.0) for g in range(dm.sg)], axis=0)
        cbuf[SUBLANES - ck:SUBLANES, :] = cv0_ref[...]

    pending = []
    for name in ["lrdt", "mxbc", "gq", "gk", "gv", "gr", "mz", "rq", "rk", "rv", "rg", "ga", "gb", "gc"]:
        a, b = dm.seg[name]
        pending += [(name, s) for s in range(a, b, MXU_COLS)]
    parts = {name: [] for name in dm.seg_names}

    def issue(n=1):
        for _ in range(n):
            if pending:
                name, s = pending.pop(0)
                parts[name].append(_dot(h_ref[...], win_ref[:, s:s + MXU_COLS]))

    every = max(1, tt // MXU_COLS)
    calls = [0]

    def issue_thin(n=1):
        for _ in range(n):
            calls[0] += 1
            if calls[0] % every == 0:
                issue()

    def take(name):
        a, b = dm.seg[name]
        while len(parts[name]) * MXU_COLS < b - a:
            issue()
        return parts[name][0] if len(parts[name]) == 1 else jnp.concatenate(parts[name], axis=1)

    ret_tabs = _retention_tables(dm, lr)
    lrdt = take("lrdt")
    xbc_pre = take("mxbc")
    gate = _dot(lrdt[:, :LANES].astype(BF16), wg2_ref[...]) + bg_ref[...]
    cbuf[SUBLANES:SUBLANES + tt, :] = xbc_pre
    conv = cb_ref[...] + cw_ref[ck:ck + 1, :] * xbc_pre
    for j in range(ck):
        conv = conv + cw_ref[j:j + 1, :] * cbuf[SUBLANES - ck + j:SUBLANES - ck + j + tt, :]
    new_carry = cbuf[SUBLANES + tt - ck:SUBLANES + tt, :]
    cbuf[SUBLANES - ck:SUBLANES, :] = new_carry
    q = take("gq") * (dm.gn ** -0.5)
    k = take("gk")
    xbc = _silu(conv)
    xs = xbc[:, :dm.s_inner]
    bm = xbc[:, dm.s_inner:dm.s_inner + LANES]
    cm = xbc[:, dm.s_inner + LANES:dm.s_inner + 2 * LANES]
    dt8 = _softplus(lrdt[:, LANES:] + dtb_ref[...])
    a8 = dt8 * (-jnp.exp(alog_ref[...]))
    b8_all = _chunk_cumsum(a8, ls)
    issue(3)
    log_a = _log_sigmoid(gate) * (1.0 / GLA_TAU)
    b_all = _chunk_cumsum(log_a, lg)
    issue(3)
    qd_all = q * jnp.exp(b_all)
    kd_all = k * jnp.exp(-b_all)
    v = take("gv").astype(BF16)
    for c in range(tt // lg):
        r = slice(c * lg, (c + 1) * lg)
        b = b_all[r]
        bl = b[lg - 1:lg, :]
        ks = k[r] * jnp.exp(bl - b)
        ogla[r, :] = _linear_attention_chunk(qd_all[r], kd_all[r], None, qd_all[r], ks, v[r],
                                             jnp.exp(bl), None, stg, dm.gh, dm.gn, dm.gp,
                                             issue_thin if c else (lambda: None))
    y_gla = (_group_norm_heads(ogla[...], dm.gh, dm.gp) * gnorm_ref[...] * _silu(take("gr"))).astype(BF16)
    issue(2)

    hpg = dm.sh // dm.sg
    for c in range(tt // ls):
        r = slice(c * ls, (c + 1) * ls)
        bcols, b8t, dtt, eb_exp, w_exp, dec_row = _ssd_decay_terms(b8_all[r], dt8[r], dm)
        xs_c = xs[r]
        xs_bf = xs_c.astype(BF16)
        cm_c = cm[r]
        bm_c = bm[r].astype(BF16)
        causal = _iota((ls, ls), 1) <= _iota((ls, ls), 0)
        gmats = [_dot_nt(jnp.where(_head_lane_mask(cm_c.shape, g, dm.sn), cm_c, 0.0).astype(BF16), bm_c)
                 for g in range(dm.sg)]
        st = sts[...]
        y_inter = _dot(cm_c.astype(BF16), st.astype(BF16)) * eb_exp
        issue_thin(2)
        y_pairs = []
        for hh in range(dm.sh):
            seg = jnp.where(causal, bcols[hh] - b8t[hh:hh + 1, :], -jnp.inf)
            m_h = gmats[hh // hpg] * dtt[hh:hh + 1, :] * jnp.exp(seg)
            pr = hh // 2
            y_pairs.append(_dot(m_h.astype(BF16), xs_bf[:, pr * LANES:(pr + 1) * LANES]))
            if hh % 2:
                issue_thin()
        first = _iota((ls, LANES), 1) < dm.sp
        y_intra = jnp.concatenate(
            [jnp.where(first, y_pairs[2 * j], y_pairs[2 * j + 1]) for j in range(dm.sh // 2)], axis=1)
        ossd[r, :] = y_intra + y_inter
        upd = _dot_tn(bm_c, (xs_c * w_exp).astype(BF16))
        blk = (_iota(upd.shape, 0) >> _log2(dm.sn)) == (_iota(upd.shape, 1) >> _log2(dm.sp * hpg))
        sts[...] = st * dec_row + jnp.where(blk, upd, 0.0)
    y = ossd[...] + dexp_ref[...] * xs
    y = y * _silu(take("mz"))
    issue()
    y_ssd = _rms(y, snorm_ref[...]).astype(BF16)

    late = [(y_gla, wbg_ref, s) for s in range(0, dm.d, MXU_COLS)]
    late += [(y_ssd, wbs_ref, s) for s in range(0, dm.d, MXU_COLS)]
    yb_parts = []

    def issue_late(n=1):
        for _ in range(n):
            if late:
                yv, w_ref, s = late.pop(0)
                yb_parts.append(_dot(yv, w_ref[:, s:s + MXU_COLS]))

    cos = cos_ref[...]
    sin = sin_ref[...]
    rq_pre = take("rq")
    rk_pre = take("rk")
    issue(len(pending))
    rq = rq_pre * cos + _swap_halves(rq_pre, dm.rn) * sin
    issue_late()
    rk = (rk_pre * cos + _swap_halves(rk_pre, dm.rn) * sin) * (dm.rn ** -0.5)
    issue_late()
    rv = take("rv").astype(BF16)
    q_dec, k_dec, st_dec, sc_dec = ret_tabs
    for c in range(tt // lr):
        r = slice(c * lr, (c + 1) * lr)
        oret[r, :] = _linear_attention_chunk(rq[r], rk[r], sc_dec, rq[r] * q_dec, rk[r] * k_dec, rv[r],
                                             None, st_dec, strt, dm.rh, dm.rn, dm.rp, issue_late)
        issue_late()
    issue_late(len(late))
    ncol = dm.d // MXU_COLS
    yb_gla = jnp.concatenate(yb_parts[:ncol], axis=1)
    yb_ssd = jnp.concatenate(yb_parts[ncol:], axis=1)
    ga, gb, gc = take("ga"), take("gb"), take("gc")
    m2 = jax.nn.sigmoid(ga) * yb_gla + jax.nn.sigmoid(gb) * yb_ssd
    y_ret = (_group_norm_heads(oret[...], dm.rh, dm.rp) * rnorm_ref[...] * _silu(take("rg"))).astype(BF16)

    halves = 2 if tt % (2 * SUBLANES * 2) == 0 else 1
    hr = tt // halves
    rs = [slice(i * hr, (i + 1) * hr) for i in range(halves)]
    yb_ret = [_dot(y_ret[r], wbr_ref[...]) for r in rs]
    for r, yb in zip(rs, yb_ret):
        m = m2[r] + jax.nn.sigmoid(gc[r]) * yb
        y_ref[r, :] = x_ref[r, :] + _dot(m.astype(BF16), wout_ref[...])

    @pl.when(t == nt - 1)
    def _():
        sg_ref[...] = stg[...]
        sr_ref[...] = strt[...]
        sz = sts[...]
        acc = sz[0:dm.sn]
        for g in range(1, dm.sg):
            acc = acc + sz[g * dm.sn:(g + 1) * dm.sn]
        ss_ref[...] = acc
        cv_ref[...] = cbuf[SUBLANES - ck:SUBLANES, :]


def _mixer(x, h, cos, sin, sg0, ss0, cv0, sr0, wts, dm, *, tt, lg, ls, lr):
    bsz, seq, d = x.shape
    assert seq % tt == 0 and tt % lg == 0 and tt % ls == 0 and tt % lr == 0
    kern = functools.partial(_mixer_kernel, dm=dm, tt=tt, lg=lg, ls=ls, lr=lr)
    gq = dm.gh * dm.gn
    rq = dm.rh * dm.rn

    def per_b(shape):
        nd = len(shape)
        return pl.BlockSpec((None,) + shape, lambda b, t: (b,) + (0,) * nd)

    in_specs = [
        pl.BlockSpec((None, tt, d), lambda b, t: (b, t, 0)),
        pl.BlockSpec((None, tt, d), lambda b, t: (b, t, 0)),
        pl.BlockSpec((tt, rq), lambda b, t: (t, 0)),
        pl.BlockSpec((tt, rq), lambda b, t: (t, 0)),
        per_b((gq, dm.gp)), per_b((dm.sn, dm.s_inner)), per_b((dm.conv_k - 1, dm.conv_dim)), per_b((rq, dm.rp)),
    ] + [_const_spec(w.shape) for w in wts]
    out_specs = [
        pl.BlockSpec((None, tt, d), lambda b, t: (b, t, 0)),
        per_b((gq, dm.gp)), per_b((dm.sn, dm.s_inner)), per_b((dm.conv_k - 1, dm.conv_dim)), per_b((rq, dm.rp)),
    ]
    out_shape = [
        jax.ShapeDtypeStruct((bsz, seq, d), F32),
        jax.ShapeDtypeStruct((bsz, gq, dm.gp), F32),
        jax.ShapeDtypeStruct((bsz, dm.sn, dm.s_inner), F32),
        jax.ShapeDtypeStruct((bsz, dm.conv_k - 1, dm.conv_dim), F32),
        jax.ShapeDtypeStruct((bsz, rq, dm.rp), F32),
    ]
    scratch = [
        pltpu.VMEM((gq, dm.gp), F32),
        pltpu.VMEM((dm.sg * dm.sn, dm.s_inner), F32),
        pltpu.VMEM((tt + SUBLANES, dm.conv_dim), F32),
        pltpu.VMEM((rq, dm.rp), F32),
        pltpu.VMEM((tt, dm.gh * dm.gp), F32),
        pltpu.VMEM((tt, dm.s_inner), F32),
        pltpu.VMEM((tt, dm.rh * dm.rp), F32),
    ]
    return pl.pallas_call(
        kern,
        grid=(bsz, seq // tt),
        in_specs=in_specs,
        out_specs=out_specs,
        out_shape=out_shape,
        scratch_shapes=scratch,
        compiler_params=pltpu.CompilerParams(
            dimension_semantics=("arbitrary", "arbitrary"), vmem_limit_bytes=VMEM_LIMIT_BYTES),
        name="mixer",
    )(x, h, cos, sin, sg0, ss0, cv0, sr0, *wts)


def _pad_cols(w, n):
    return jnp.pad(w, ((0, 0), (0, n - w.shape[1])))


def _row(v, n=None):
    v = v.reshape(1, -1).astype(F32)
    return v if n is None else _pad_cols(v, n)


def _rope_tables(pos, dm):
    half = dm.rn // 2
    freqs = ROPE_BASE ** (-jnp.arange(half, dtype=F32) / half)
    ang = pos.astype(F32)[:, None] * freqs[None, :]
    cos = jnp.cos(ang)
    sin = jnp.sin(ang)
    cos_h = jnp.concatenate([cos, cos], axis=1)
    sin_h = jnp.concatenate([-sin, sin], axis=1)
    return jnp.tile(cos_h, (1, dm.rh)), jnp.tile(sin_h, (1, dm.rh))


def _state_to_kernel_t(s):
    b, h, n, p = s.shape
    return s.reshape(b, h * n, p)


def _state_from_kernel_t(s, h):
    b, hn, p = s.shape
    return s.reshape(b, h, hn // h, p)


def _ssd_to_kernel(s):
    b, h, n, p = s.shape
    return jnp.transpose(s, (0, 2, 1, 3)).reshape(b, n, h * p)


def _ssd_from_kernel(s, h):
    b, n, hp = s.shape
    return jnp.transpose(s.reshape(b, n, h, hp // h), (0, 2, 1, 3))


def _tiles(seq):
    if seq % SCAN_CHUNK == 0:
        tt = 512 if seq % 512 == 0 else SCAN_CHUNK
        return dict(tt=tt, lg=SCAN_CHUNK, ls=min(tt, 128), lr=min(tt, 128))
    return dict(tt=seq, lg=seq, ls=seq, lr=seq)


def kernel(x_prompt, x_sample, state_gla, state_ssd, cache_conv, state_ret, norm_ffn1, ffn1_w_in, ffn1_w_out,
           norm_mix, w_in, gla_w_gate2, gla_b_gate, gla_norm, ssd_conv_w, ssd_conv_b, ssd_dt_bias, ssd_a_log,
           ssd_d, ssd_norm, ret_norm, w_branch_gla, w_branch_ssd, w_branch_ret, w_out, norm_ffn2, ffn2_w_in,
           ffn2_w_out, norm_final):
    depth = w_in.shape[0]
    d = x_prompt.shape[-1]
    d_ff = ffn1_w_out.shape[1]
    dm = _Dims(state_gla, state_ssd, cache_conv, state_ret, gla_w_gate2, ssd_conv_w, d)

    split_sizes = (dm.gh * dm.gn, dm.gh * dm.gn, dm.gh * dm.gp, dm.gh * dm.gp, dm.rank,
                   dm.s_inner, dm.conv_dim, dm.sh,
                   dm.rh * dm.rn, dm.rh * dm.rn, dm.rh * dm.rp, dm.rh * dm.rp, d, d, d)
    pts, acc = [], 0
    for s in split_sizes[:-1]:
        acc += s
        pts.append(acc)

    def mixer_weights(l):
        (s_gq, s_gk, s_gv, s_gr, s_glr, s_mz, s_mxbc, s_mdt,
         s_rq, s_rk, s_rv, s_rg, s_ga, s_gb, s_gc) = jnp.split(w_in[l], pts, axis=1)
        win = jnp.concatenate(
            [s_gq, s_gk, _pad_cols(s_glr, LANES), _pad_cols(s_mdt, LANES), s_gv, s_gr, s_mz, s_mxbc,
             s_rq, s_rk, s_rv, s_rg, s_ga, s_gb, s_gc], axis=1).astype(BF16)
        assert win.shape[1] == dm.in_cols
        wg2 = jnp.pad(gla_w_gate2[l], ((0, LANES - dm.rank), (0, 0))).astype(BF16)
        return [
            win, wg2, _row(gla_b_gate[l]), _row(gla_norm[l]),
            ssd_conv_w[l].astype(F32), _row(ssd_conv_b[l]), _row(ssd_dt_bias[l], LANES), _row(ssd_a_log[l], LANES),
            _row(jnp.repeat(ssd_d[l], dm.sp)), _row(ssd_norm[l]), _row(ret_norm[l]),
            w_branch_gla[l].astype(BF16), w_branch_ssd[l].astype(BF16), w_branch_ret[l].astype(BF16),
            w_out[l].astype(BF16),
        ]

    def ffn_weights(nrm, w_i, w_o, l):
        return _row(nrm[l]), w_i[l][:, :d_ff].astype(BF16), w_i[l][:, d_ff:].astype(BF16), w_o[l].astype(BF16)

    g_final = _row(norm_final)

    def run_group(x, pos, states):
        bsz, seq, _ = x.shape
        m = bsz * seq
        tm = 1024 if m % 1024 == 0 else m
        cos, sin = _rope_tables(pos, dm)
        tl = _tiles(seq)
        new_states = []
        for l in range(depth):
            sg0, ss0, cv0, sr0 = states(l)
            g1, wg1, wu1, wo1 = ffn_weights(norm_ffn1, ffn1_w_in, ffn1_w_out, l)
            x, h = _ffn(x.reshape(m, d), g1, wg1, wu1, wo1, _row(norm_mix[l]), tm=tm, fc=256, post="emit")
            x, sg, ss, cv, sr = _mixer(x.reshape(bsz, seq, d), h.reshape(bsz, seq, d), cos, sin,
                                       _state_to_kernel_t(sg0), _ssd_to_kernel(ss0), cv0,
                                       _state_to_kernel_t(sr0), mixer_weights(l), dm, **tl)
            g2, wg2_, wu2, wo2 = ffn_weights(norm_ffn2, ffn2_w_in, ffn2_w_out, l)
            last = l == depth - 1
            x = _ffn(x.reshape(m, d), g2, wg2_, wu2, wo2, g_final if last else None, tm=tm, fc=256,
                     post="norm" if last else None).reshape(bsz, seq, d)
            new_states.append((_state_from_kernel_t(sg, dm.gh), _ssd_from_kernel(ss, dm.sh), cv,
                               _state_from_kernel_t(sr, dm.rh)))
        return x, new_states

    bp, tp = x_prompt.shape[:2]
    ts = x_sample.shape[1]
    dt_ = x_prompt.dtype

    def zero_states(_):
        return (jnp.zeros((bp, dm.gh, dm.gn, dm.gp), dt_), jnp.zeros((bp, dm.sh, dm.sn, dm.sp), dt_),
                jnp.zeros((bp, dm.conv_k - 1, dm.conv_dim), dt_), jnp.zeros((bp, dm.rh, dm.rn, dm.rp), dt_))

    def carried_states(l):
        return state_gla[l], state_ssd[l], cache_conv[l], state_ret[l]

    y_p, st_p = run_group(x_prompt, jnp.arange(tp, dtype=jnp.int32), zero_states)
    y_s, st_s = run_group(x_sample, PAST_LEN + jnp.arange(ts, dtype=jnp.int32), carried_states)

    def stack(sts, i):
        return jnp.stack([s[i] for s in sts])

    return (y_p, y_s, stack(st_p, 0), stack(st_p, 1), stack(st_p, 2), stack(st_p, 3),
            stack(st_s, 0), stack(st_s, 1), stack(st_s, 2), stack(st_s, 3))
```

```python
import functools
import math

import jax
import jax.numpy as jnp
from jax import lax
from jax.experimental import pallas as pl
from jax.experimental.pallas import tpu as pltpu

F32 = jnp.float32
BF16 = jnp.bfloat16

EPS = 1e-6
PAST_LEN = 1024
GLA_TAU = 16.0
ROPE_BASE = 10000.0
SCAN_CHUNK = 64

LANES = 128
SUBLANES = 8
MXU_COLS = 256
FFN_OUT_ROWS = 256
VMEM_LIMIT_BYTES = 56 * 1024 * 1024


def _dot(a, b):
    return jnp.dot(a, b, preferred_element_type=F32)


def _dot_nt(a, b):
    return lax.dot_general(a, b, (((1,), (1,)), ((), ())), preferred_element_type=F32)


def _dot_tn(a, b):
    return lax.dot_general(a, b, (((0,), (0,)), ((), ())), preferred_element_type=F32)


def _split3(x):
    hi = x.astype(BF16)
    r1 = x - hi.astype(F32)
    mid = r1.astype(BF16)
    lo = (r1 - mid.astype(F32)).astype(BF16)
    return hi, mid, lo


def _dot_exact01(m01, x):
    hi, mid, lo = _split3(x)
    return _dot(m01, hi) + _dot(m01, mid) + _dot(m01, lo)


def _chunk_cumsum(x, L):
    T = x.shape[0]
    rows = MXU_COLS if (T % MXU_COLS == 0 and MXU_COLS % L == 0) else T
    tril = _block_tril(rows, L)
    out = [_dot_exact01(tril, x[i:i + rows]) for i in range(0, T, rows)]
    return out[0] if len(out) == 1 else jnp.concatenate(out, axis=0)


def _rms(x, g):
    return x * lax.rsqrt(jnp.mean(x * x, axis=-1, keepdims=True) + EPS) * g


def _silu(x):
    return x * jax.nn.sigmoid(x)


def _softplus(x):
    return jnp.maximum(x, 0.0) + jnp.log(1.0 + jnp.exp(-jnp.abs(x)))


def _log_sigmoid(x):
    return jnp.minimum(x, 0.0) - jnp.log(1.0 + jnp.exp(-jnp.abs(x)))


def _iota(shape, axis):
    return lax.broadcasted_iota(jnp.int32, shape, axis)


def _log2(n):
    k = int(math.log2(n))
    assert (1 << k) == n, f"{n} must be a power of two"
    return k


def _ffn_kernel(*refs, fc, post):
    if post is None:
        x_ref, g_ref, wg_ref, wu_ref, wo_ref, o_ref, h_ref, a_ref = refs
    elif post == "norm":
        x_ref, g_ref, wg_ref, wu_ref, wo_ref, g2_ref, o_ref, h_ref, a_ref = refs
    else:
        x_ref, g_ref, wg_ref, wu_ref, wo_ref, g2_ref, o_ref, hn_ref, h_ref, a_ref = refs
    x = x_ref[...]
    h_ref[...] = _rms(x, g_ref[...]).astype(BF16)
    d_ff = wg_ref.shape[1]
    for c in range(d_ff // fc):
        sl = slice(c * fc, (c + 1) * fc)
        gate = _dot(h_ref[...], wg_ref[:, sl])
        up = _dot(h_ref[...], wu_ref[:, sl])
        a_ref[:, sl] = (_silu(gate) * up).astype(BF16)
    tm = x_ref.shape[0]
    rb = FFN_OUT_ROWS if tm % FFN_OUT_ROWS == 0 else tm
    for i in range(tm // rb):
        r = slice(i * rb, (i + 1) * rb)
        y = x_ref[r, :] + 0.5 * _dot(a_ref[r, :], wo_ref[...])
        if post == "norm":
            y = _rms(y, g2_ref[...])
        o_ref[r, :] = y
        if post == "emit":
            hn_ref[r, :] = _rms(y, g2_ref[...]).astype(BF16)


def _const_spec(shape):
    nd = len(shape)
    return pl.BlockSpec(shape, lambda *_: (0,) * nd, pipeline_mode=pl.Buffered(1))


def _ffn(x2d, g, w_gate, w_up, w_out, g2, *, tm, fc, post):
    m, d = x2d.shape
    d_ff = w_gate.shape[1]
    assert m % tm == 0 and d_ff % fc == 0 and (g2 is None) == (post is None)
    kern = functools.partial(_ffn_kernel, fc=fc, post=post)
    row_spec = pl.BlockSpec((tm, d), lambda i: (i, 0))
    in_specs = [row_spec, _const_spec((1, d)), _const_spec((d, d_ff)), _const_spec((d, d_ff)),
                _const_spec((d_ff, d))]
    args = [x2d, g, w_gate, w_up, w_out]
    if post is not None:
        in_specs.append(_const_spec((1, d)))
        args.append(g2)
    out_specs, out_shape = row_spec, jax.ShapeDtypeStruct((m, d), F32)
    if post == "emit":
        out_specs, out_shape = [row_spec, row_spec], [out_shape, jax.ShapeDtypeStruct((m, d), BF16)]
    return pl.pallas_call(
        kern,
        grid=(m // tm,),
        in_specs=in_specs,
        out_specs=out_specs,
        out_shape=out_shape,
        scratch_shapes=[pltpu.VMEM((tm, d), BF16), pltpu.VMEM((tm, d_ff), BF16)],
        compiler_params=pltpu.CompilerParams(
            dimension_semantics=("arbitrary",), vmem_limit_bytes=VMEM_LIMIT_BYTES),
        name="ffn",
    )(*args)


class _Dims:
    def __init__(self, state_gla, state_ssd, cache_conv, state_ret, gla_w_gate2, ssd_conv_w, d_model):
        _, _, self.gh, self.gn, self.gp = state_gla.shape
        _, _, self.sh, self.sn, self.sp = state_ssd.shape
        _, _, self.rh, self.rn, self.rp = state_ret.shape
        self.conv_k = ssd_conv_w.shape[1]
        self.conv_dim = cache_conv.shape[-1]
        self.rank = gla_w_gate2.shape[1]
        self.d = d_model
        self.s_inner = self.sh * self.sp
        self.sg = (self.conv_dim - self.s_inner) // (2 * self.sn)
        assert self.gh * self.gn == 2 * LANES and self.gp == LANES
        assert self.rh * self.rn == 2 * LANES and self.rp == LANES
        assert self.sp * 2 == LANES and self.sg * self.sn == LANES and self.sh <= LANES
        assert self.rank <= LANES and self.conv_k - 1 <= SUBLANES
        w = [self.gh * self.gn, self.gh * self.gn, 2 * LANES, self.gh * self.gp, self.gh * self.gp,
             self.s_inner, self.conv_dim,
             self.rh * self.rn, self.rh * self.rn, self.rh * self.rp, self.rh * self.rp,
             d_model, d_model, d_model]
        self.seg_names = ["gq", "gk", "lrdt", "gv", "gr", "mz", "mxbc", "rq", "rk", "rv", "rg", "ga", "gb", "gc"]
        self.seg = {}
        off = 0
        for n, wd in zip(self.seg_names, w):
            assert wd % MXU_COLS == 0
            self.seg[n] = (off, off + wd)
            off += wd
        self.in_cols = off


def _head_lane_mask(shape, h, width):
    lane = _iota(shape, 1)
    return (lane >> _log2(width)) == h


def _stack_heads(x, nh, width):
    parts = [jnp.where(_head_lane_mask(x.shape, h, width), x, 0.0).astype(BF16) for h in range(nh)]
    return jnp.concatenate(parts, axis=0)


def _linear_attention_chunk(q_sc, k_sc, sc_mul, q_in, k_up, v, dec_row, dec_col, st_ref, nh, n, p, issue):
    L = q_sc.shape[0]
    k_big = _stack_heads(k_sc, nh, n)
    sc = _dot_nt(q_sc.astype(BF16), k_big)
    st = st_ref[...]
    rowhead = _iota(st.shape, 0) >> _log2(n)
    st_bd = jnp.concatenate([jnp.where(rowhead == h, st, 0.0).astype(BF16) for h in range(nh)], axis=1)
    inter = _dot(q_in.astype(BF16), st_bd)
    rows = [k_up]
    if dec_col is None:
        rows.append(jnp.broadcast_to(dec_row, (SUBLANES, nh * n)))
    pad = LANES - L - (len(rows) - 1) * SUBLANES
    assert pad >= 0
    if pad:
        rows.append(jnp.zeros((pad, nh * n), F32))
    kt = jnp.concatenate(rows, axis=0).T if len(rows) > 1 else k_up.T
    issue()
    lane = _iota(sc.shape, 1)
    row = _iota(sc.shape, 0)
    causal = (lane & (L - 1)) <= row
    sc = jnp.where(causal, sc if sc_mul is None else sc * sc_mul, 0.0)
    sc = sc.astype(BF16)
    if L % LANES == 0:
        intra = jnp.concatenate(
            [_dot(sc[:, h * L:(h + 1) * L], v[:, h * p:(h + 1) * p].astype(BF16)) for h in range(nh)], axis=1)
    else:
        intra = _dot(sc, _stack_heads(v, nh, p))
    o = intra + inter
    if dec_col is None:
        dec_col = kt[:, L:L + 1]
    kt = kt.astype(BF16)
    upd = []
    for h in range(nh):
        v_h = v[:, h * p:(h + 1) * p].astype(BF16)
        if L < LANES:
            v_h = jnp.concatenate([v_h, jnp.zeros((LANES - L, p), BF16)], axis=0)
        upd.append(_dot(kt[h * n:(h + 1) * n, :], v_h))
    st_ref[...] = st * dec_col + jnp.concatenate(upd, axis=0)
    return o


def _group_norm_heads(o, nh, p):
    outs = []
    for h in range(nh):
        oh = o[:, h * p:(h + 1) * p]
        mu = jnp.mean(oh, axis=-1, keepdims=True)
        dlt = oh - mu
        var = jnp.mean(dlt * dlt, axis=-1, keepdims=True)
        outs.append(dlt * lax.rsqrt(var + EPS))
    return jnp.concatenate(outs, axis=1)


def _block_tril(tt, L):
    row = _iota((tt, tt), 0)
    col = _iota((tt, tt), 1)
    k = _log2(L)
    return jnp.where((col <= row) & ((row >> k) == (col >> k)), 1.0, 0.0).astype(BF16)


def _ret_log_gamma(head, nh):
    out = jnp.full(head.shape, math.log1p(-2.0 ** (-5.0 - (nh - 1))), F32)
    for h in range(nh - 1):
        out = jnp.where(head == h, math.log1p(-2.0 ** (-5.0 - h)), out)
    return out


def _ssd_decay_terms(b8, dtc, dm):
    L = b8.shape[0]
    bl8 = b8[L - 1:L, :]
    w8 = dtc * jnp.exp(bl8 - b8)
    b8t = _transpose_rows(b8)
    dtt = _transpose_rows(dtc)
    lane8 = _iota(b8.shape, 1)
    ebl8 = jnp.exp(bl8)
    bcols, eb_cols, w_cols, dec_cols = [], [], [], []
    for hh in range(dm.sh):
        bcol = jnp.sum(jnp.where(lane8 == hh, b8, 0.0), axis=-1, keepdims=True)
        bcols.append(bcol)
        eb_cols.append(jnp.broadcast_to(jnp.exp(bcol), (L, LANES)))
        w_cols.append(jnp.broadcast_to(
            jnp.sum(jnp.where(lane8 == hh, w8, 0.0), axis=-1, keepdims=True), (L, LANES)))
        dec_cols.append(jnp.broadcast_to(
            jnp.sum(jnp.where(_iota(bl8.shape, 1) == hh, ebl8, 0.0), axis=-1, keepdims=True), (1, LANES)))

    def pair_lanes(cols):
        first = _iota(cols[0].shape, 1) < dm.sp
        return jnp.concatenate(
            [jnp.where(first, cols[2 * j], cols[2 * j + 1]) for j in range(dm.sh // 2)], axis=1)

    return bcols, b8t, dtt, pair_lanes(eb_cols), pair_lanes(w_cols), pair_lanes(dec_cols)


def _retention_tables(dm, L):
    nlane = dm.rh * dm.rn
    lgam_q = _ret_log_gamma(_iota((L, nlane), 1) >> _log2(dm.rn), dm.rh)
    pos_q = _iota((L, nlane), 0).astype(F32)
    q_dec = jnp.exp((pos_q + 1.0) * lgam_q)
    k_dec = jnp.exp((L - 1.0 - pos_q) * lgam_q)
    lgam_r = _ret_log_gamma(_iota((nlane, dm.rp), 0) >> _log2(dm.rn), dm.rh)
    st_dec = jnp.exp(L * lgam_r)
    lgam_s = _ret_log_gamma(_iota((L, dm.rh * L), 1) >> _log2(L), dm.rh)
    dpos = (_iota((L, dm.rh * L), 0) - (_iota((L, dm.rh * L), 1) & (L - 1))).astype(F32)
    sc_dec = jnp.exp(jnp.maximum(dpos, 0.0) * lgam_s)
    return q_dec, k_dec, st_dec, sc_dec


def _transpose_rows(x):
    L = x.shape[0]
    if L % LANES:
        pad = LANES - L % LANES
        x = jnp.concatenate([x, jnp.zeros((pad, x.shape[1]), x.dtype)], axis=0)
    return x.T[:, :L]


def _swap_halves(x, n):
    width = x.shape[1]
    half = n // 2
    first = (_iota(x.shape, 1) & (n - 1)) < half
    return jnp.where(first, pltpu.roll(x, width - half, 1), pltpu.roll(x, half, 1))


def _mixer_kernel(x_ref, h_ref, cos_ref, sin_ref, sg0_ref, ss0_ref, cv0_ref, sr0_ref,
                  win_ref, wg2_ref, bg_ref, gnorm_ref, cw_ref, cb_ref, dtb_ref, alog_ref,
                  dexp_ref, snorm_ref, rnorm_ref, wbg_ref, wbs_ref, wbr_ref, wout_ref,
                  y_ref, sg_ref, ss_ref, cv_ref, sr_ref,
                  stg, sts, cbuf, strt, ogla, ossd, oret,
                  *, dm, tt, lg, ls, lr):
    t = pl.program_id(1)
    nt = pl.num_programs(1)
    ck = dm.conv_k - 1

    @pl.when(t == 0)
    def _():
        stg[...] = sg0_ref[...]
        strt[...] = sr0_ref[...]
        z0 = ss0_ref[...]
        hpg = dm.sh // dm.sg
        sts[...] = jnp.concatenate(
            [jnp.where(_head_lane_mask(z0.shape, g, dm.sp * hpg), z0, 0.0) for g in range(dm.sg)], axis=0)
        cbuf[SUBLANES - ck:SUBLANES, :] = cv0_ref[...]

    pending = []
    for name in ["lrdt", "mxbc", "gq", "gk", "gv", "gr", "mz", "rq", "rk", "rv", "rg", "ga", "gb", "gc"]:
        a, b = dm.seg[name]
        pending += [(name, s) for s in range(a, b, MXU_COLS)]
    parts = {name: [] for name in dm.seg_names}

    def issue(n=1):
        for _ in range(n):
            if pending:
                name, s = pending.pop(0)
                parts[name].append(_dot(h_ref[...], win_ref[:, s:s + MXU_COLS]))

    every = max(1, tt // MXU_COLS)
    calls = [0]

    def issue_thin(n=1):
        for _ in range(n):
            calls[0] += 1
            if calls[0] % every == 0:
                issue()

    def take(name):
        a, b = dm.seg[name]
        while len(parts[name]) * MXU_COLS < b - a:
            issue()
        return parts[name][0] if len(parts[name]) == 1 else jnp.concatenate(parts[name], axis=1)

    ret_tabs = _retention_tables(dm, lr)
    lrdt = take("lrdt")
    xbc_pre = take("mxbc")
    gate = _dot(lrdt[:, :LANES].astype(BF16), wg2_ref[...]) + bg_ref[...]
    cbuf[SUBLANES:SUBLANES + tt, :] = xbc_pre
    conv = cb_ref[...] + cw_ref[ck:ck + 1, :] * xbc_pre
    for j in range(ck):
        conv = conv + cw_ref[j:j + 1, :] * cbuf[SUBLANES - ck + j:SUBLANES - ck + j + tt, :]
    new_carry = cbuf[SUBLANES + tt - ck:SUBLANES + tt, :]
    cbuf[SUBLANES - ck:SUBLANES, :] = new_carry
    q = take("gq") * (dm.gn ** -0.5)
    k = take("gk")
    xbc = _silu(conv)
    xs = xbc[:, :dm.s_inner]
    bm = xbc[:, dm.s_inner:dm.s_inner + LANES]
    cm = xbc[:, dm.s_inner + LANES:dm.s_inner + 2 * LANES]
    dt8 = _softplus(lrdt[:, LANES:] + dtb_ref[...])
    a8 = dt8 * (-jnp.exp(alog_ref[...]))
    b8_all = _chunk_cumsum(a8, ls)
    issue(3)

    log_a = _log_sigmoid(gate) * (1.0 / GLA_TAU)
    b_all = _chunk_cumsum(log_a, lg)
    issue(3)
    qd_all = q * jnp.exp(b_all)
    kd_all = k * jnp.exp(-b_all)
    v = take("gv").astype(BF16)
    for c in range(tt // lg):
        r = slice(c * lg, (c + 1) * lg)
        b = b_all[r]
        bl = b[lg - 1:lg, :]
        ks = k[r] * jnp.exp(bl - b)
        ogla[r, :] = _linear_attention_chunk(qd_all[r], kd_all[r], None, qd_all[r], ks, v[r],
                                             jnp.exp(bl), None, stg, dm.gh, dm.gn, dm.gp,
                                             issue_thin if c else (lambda: None))
    y_gla = (_group_norm_heads(ogla[...], dm.gh, dm.gp) * gnorm_ref[...] * _silu(take("gr"))).astype(BF16)
    issue(2)

    hpg = dm.sh // dm.sg
    for c in range(tt // ls):
        r = slice(c * ls, (c + 1) * ls)
        bcols, b8t, dtt, eb_exp, w_exp, dec_row = _ssd_decay_terms(b8_all[r], dt8[r], dm)
        xs_c = xs[r]
        xs_bf = xs_c.astype(BF16)
        cm_c = cm[r]
        bm_c = bm[r].astype(BF16)
        causal = _iota((ls, ls), 1) <= _iota((ls, ls), 0)
        gmats = [_dot_nt(jnp.where(_head_lane_mask(cm_c.shape, g, dm.sn), cm_c, 0.0).astype(BF16), bm_c)
                 for g in range(dm.sg)]
        st = sts[...]
        y_inter = _dot(cm_c.astype(BF16), st.astype(BF16)) * eb_exp
        issue_thin(2)
        y_pairs = []
        for hh in range(dm.sh):
            seg = jnp.where(causal, bcols[hh] - b8t[hh:hh + 1, :], -jnp.inf)
            m_h = gmats[hh // hpg] * dtt[hh:hh + 1, :] * jnp.exp(seg)
            pr = hh // 2
            y_pairs.append(_dot(m_h.astype(BF16), xs_bf[:, pr * LANES:(pr + 1) * LANES]))
            if hh % 2:
                issue_thin()
        first = _iota((ls, LANES), 1) < dm.sp
        y_intra = jnp.concatenate(
            [jnp.where(first, y_pairs[2 * j], y_pairs[2 * j + 1]) for j in range(dm.sh // 2)], axis=1)
        ossd[r, :] = y_intra + y_inter
        upd = _dot_tn(bm_c, (xs_c * w_exp).astype(BF16))
        blk = (_iota(upd.shape, 0) >> _log2(dm.sn)) == (_iota(upd.shape, 1) >> _log2(dm.sp * hpg))
        sts[...] = st * dec_row + jnp.where(blk, upd, 0.0)
    y = ossd[...] + dexp_ref[...] * xs
    y = y * _silu(take("mz"))
    issue()
    y_ssd = _rms(y, snorm_ref[...]).astype(BF16)

    late = [(y_gla, wbg_ref, s) for s in range(0, dm.d, MXU_COLS)]
    late += [(y_ssd, wbs_ref, s) for s in range(0, dm.d, MXU_COLS)]
    yb_parts = []

    def issue_late(n=1):
        for _ in range(n):
            if late:
                yv, w_ref, s = late.pop(0)
                yb_parts.append(_dot(yv, w_ref[:, s:s + MXU_COLS]))

    cos = cos_ref[...]
    sin = sin_ref[...]
    rq_pre = take("rq")
    rk_pre = take("rk")
    issue(len(pending))
    rq = rq_pre * cos + _swap_halves(rq_pre, dm.rn) * sin
    issue_late()
    rk = (rk_pre * cos + _swap_halves(rk_pre, dm.rn) * sin) * (dm.rn ** -0.5)
    issue_late()
    rv = take("rv").astype(BF16)
    q_dec, k_dec, st_dec, sc_dec = ret_tabs
    for c in range(tt // lr):
        r = slice(c * lr, (c + 1) * lr)
        oret[r, :] = _linear_attention_chunk(rq[r], rk[r], sc_dec, rq[r] * q_dec, rk[r] * k_dec, rv[r],
                                             None, st_dec, strt, dm.rh, dm.rn, dm.rp, issue_late)
        issue_late()
    issue_late(len(late))
    ncol = dm.d // MXU_COLS
    yb_gla = jnp.concatenate(yb_parts[:ncol], axis=1)
    yb_ssd = jnp.concatenate(yb_parts[ncol:], axis=1)
    ga, gb, gc = take("ga"), take("gb"), take("gc")
    m2 = jax.nn.sigmoid(ga) * yb_gla + jax.nn.sigmoid(gb) * yb_ssd
    y_ret = (_group_norm_heads(oret[...], dm.rh, dm.rp) * rnorm_ref[...] * _silu(take("rg"))).astype(BF16)

    halves = 2 if tt % (2 * SUBLANES * 2) == 0 else 1
    hr = tt // halves
    rs = [slice(i * hr, (i + 1) * hr) for i in range(halves)]
    yb_ret = [_dot(y_ret[r], wbr_ref[...]) for r in rs]
    for r, yb in zip(rs, yb_ret):
        m = m2[r] + jax.nn.sigmoid(gc[r]) * yb
        y_ref[r, :] = x_ref[r, :] + _dot(m.astype(BF16), wout_ref[...])

    @pl.when(t == nt - 1)
    def _():
        sg_ref[...] = stg[...]
        sr_ref[...] = strt[...]
        sz = sts[...]
        acc = sz[0:dm.sn]
        for g in range(1, dm.sg):
            acc = acc + sz[g * dm.sn:(g + 1) * dm.sn]
        ss_ref[...] = acc
        cv_ref[...] = cbuf[SUBLANES - ck:SUBLANES, :]


def _mixer(x, h, cos, sin, sg0, ss0, cv0, sr0, wts, dm, *, tt, lg, ls, lr):
    bsz, seq, d = x.shape
    assert seq % tt == 0 and tt % lg == 0 and tt % ls == 0 and tt % lr == 0
    kern = functools.partial(_mixer_kernel, dm=dm, tt=tt, lg=lg, ls=ls, lr=lr)
    gq = dm.gh * dm.gn
    rq = dm.rh * dm.rn

    def per_b(shape):
        nd = len(shape)
        return pl.BlockSpec((None,) + shape, lambda b, t: (b,) + (0,) * nd)

    in_specs = [
        pl.BlockSpec((None, tt, d), lambda b, t: (b, t, 0)),
        pl.BlockSpec((None, tt, d), lambda b, t: (b, t, 0)),
        pl.BlockSpec((tt, rq), lambda b, t: (t, 0)),
        pl.BlockSpec((tt, rq), lambda b, t: (t, 0)),
        per_b((gq, dm.gp)), per_b((dm.sn, dm.s_inner)), per_b((dm.conv_k - 1, dm.conv_dim)), per_b((rq, dm.rp)),
    ] + [_const_spec(w.shape) for w in wts]
    out_specs = [
        pl.BlockSpec((None, tt, d), lambda b, t: (b, t, 0)),
        per_b((gq, dm.gp)), per_b((dm.sn, dm.s_inner)), per_b((dm.conv_k - 1, dm.conv_dim)), per_b((rq, dm.rp)),
    ]
    out_shape = [
        jax.ShapeDtypeStruct((bsz, seq, d), F32),
        jax.ShapeDtypeStruct((bsz, gq, dm.gp), F32),
        jax.ShapeDtypeStruct((bsz, dm.sn, dm.s_inner), F32),
        jax.ShapeDtypeStruct((bsz, dm.conv_k - 1, dm.conv_dim), F32),
        jax.ShapeDtypeStruct((bsz, rq, dm.rp), F32),
    ]
    scratch = [
        pltpu.VMEM((gq, dm.gp), F32),
        pltpu.VMEM((dm.sg * dm.sn, dm.s_inner), F32),
        pltpu.VMEM((tt + SUBLANES, dm.conv_dim), F32),
        pltpu.VMEM((rq, dm.rp), F32),
        pltpu.VMEM((tt, dm.gh * dm.gp), F32),
        pltpu.VMEM((tt, dm.s_inner), F32),
        pltpu.VMEM((tt, dm.rh * dm.rp), F32),
    ]
    return pl.pallas_call(
        kern,
        grid=(bsz, seq // tt),
        in_specs=in_specs,
        out_specs=out_specs,
        out_shape=out_shape,
        scratch_shapes=scratch,
        compiler_params=pltpu.CompilerParams(
            dimension_semantics=("arbitrary", "arbitrary"), vmem_limit_bytes=VMEM_LIMIT_BYTES),
        name="mixer",
    )(x, h, cos, sin, sg0, ss0, cv0, sr0, *wts)


def _pad_cols(w, n):
    return jnp.pad(w, ((0, 0), (0, n - w.shape[1])))


def _row(v, n=None):
    v = v.reshape(1, -1).astype(F32)
    return v if n is None else _pad_cols(v, n)


def _rope_tables(pos, dm):
    half = dm.rn // 2
    freqs = ROPE_BASE ** (-jnp.arange(half, dtype=F32) / half)
    ang = pos.astype(F32)[:, None] * freqs[None, :]
    cos = jnp.cos(ang)
    sin = jnp.sin(ang)
    cos_h = jnp.concatenate([cos, cos], axis=1)
    sin_h = jnp.concatenate([-sin, sin], axis=1)
    return jnp.tile(cos_h, (1, dm.rh)), jnp.tile(sin_h, (1, dm.rh))


def _state_to_kernel_t(s):
    b, h, n, p = s.shape
    return s.reshape(b, h * n, p)


def _state_from_kernel_t(s, h):
    b, hn, p = s.shape
    return s.reshape(b, h, hn // h, p)


def _ssd_to_kernel(s):
    b, h, n, p = s.shape
    return jnp.transpose(s, (0, 2, 1, 3)).reshape(b, n, h * p)


def _ssd_from_kernel(s, h):
    b, n, hp = s.shape
    return jnp.transpose(s.reshape(b, n, h, hp // h), (0, 2, 1, 3))


def _tiles(seq):
    if seq % SCAN_CHUNK == 0:
        tt = 512 if seq % 512 == 0 else SCAN_CHUNK
        return dict(tt=tt, lg=SCAN_CHUNK, ls=min(tt, 128), lr=min(tt, 128))
    return dict(tt=seq, lg=seq, ls=seq, lr=seq)


def kernel(x_prompt, x_sample, state_gla, state_ssd, cache_conv, state_ret, norm_ffn1, ffn1_w_in, ffn1_w_out,
           norm_mix, w_in, gla_w_gate2, gla_b_gate, gla_norm, ssd_conv_w, ssd_conv_b, ssd_dt_bias, ssd_a_log,
           ssd_d, ssd_norm, ret_norm, w_branch_gla, w_branch_ssd, w_branch_ret, w_out, norm_ffn2, ffn2_w_in,
           ffn2_w_out, norm_final):
    depth = w_in.shape[0]
    d = x_prompt.shape[-1]
    d_ff = ffn1_w_out.shape[1]
    dm = _Dims(state_gla, state_ssd, cache_conv, state_ret, gla_w_gate2, ssd_conv_w, d)

    split_sizes = (dm.gh * dm.gn, dm.gh * dm.gn, dm.gh * dm.gp, dm.gh * dm.gp, dm.rank,
                   dm.s_inner, dm.conv_dim, dm.sh,
                   dm.rh * dm.rn, dm.rh * dm.rn, dm.rh * dm.rp, dm.rh * dm.rp, d, d, d)
    pts, acc = [], 0
    for s in split_sizes[:-1]:
        acc += s
        pts.append(acc)

    def mixer_weights(l):
        (s_gq, s_gk, s_gv, s_gr, s_glr, s_mz, s_mxbc, s_mdt,
         s_rq, s_rk, s_rv, s_rg, s_ga, s_gb, s_gc) = jnp.split(w_in[l], pts, axis=1)
        win = jnp.concatenate(
            [s_gq, s_gk, _pad_cols(s_glr, LANES), _pad_cols(s_mdt, LANES), s_gv, s_gr, s_mz, s_mxbc,
             s_rq, s_rk, s_rv, s_rg, s_ga, s_gb, s_gc], axis=1).astype(BF16)
        assert win.shape[1] == dm.in_cols
        wg2 = jnp.pad(gla_w_gate2[l], ((0, LANES - dm.rank), (0, 0))).astype(BF16)
        return [
            win, wg2, _row(gla_b_gate[l]), _row(gla_norm[l]),
            ssd_conv_w[l].astype(F32), _row(ssd_conv_b[l]), _row(ssd_dt_bias[l], LANES), _row(ssd_a_log[l], LANES),
            _row(jnp.repeat(ssd_d[l], dm.sp)), _row(ssd_norm[l]), _row(ret_norm[l]),
            w_branch_gla[l].astype(BF16), w_branch_ssd[l].astype(BF16), w_branch_ret[l].astype(BF16),
            w_out[l].astype(BF16),
        ]

    def ffn_weights(nrm, w_i, w_o, l):
        return _row(nrm[l]), w_i[l][:, :d_ff].astype(BF16), w_i[l][:, d_ff:].astype(BF16), w_o[l].astype(BF16)

    g_final = _row(norm_final)

    def run_group(x, pos, states):
        bsz, seq, _ = x.shape
        m = bsz * seq
        tm = 1024 if m % 1024 == 0 else m
        cos, sin = _rope_tables(pos, dm)
        tl = _tiles(seq)
        new_states = []
        for l in range(depth):
            sg0, ss0, cv0, sr0 = states(l)
            g1, wg1, wu1, wo1 = ffn_weights(norm_ffn1, ffn1_w_in, ffn1_w_out, l)
            x, h = _ffn(x.reshape(m, d), g1, wg1, wu1, wo1, _row(norm_mix[l]), tm=tm, fc=256, post="emit")
            x, sg, ss, cv, sr = _mixer(x.reshape(bsz, seq, d), h.reshape(bsz, seq, d), cos, sin,
                                       _state_to_kernel_t(sg0), _ssd_to_kernel(ss0), cv0,
                                       _state_to_kernel_t(sr0), mixer_weights(l), dm, **tl)
            g2, wg2_, wu2, wo2 = ffn_weights(norm_ffn2, ffn2_w_in, ffn2_w_out, l)
            last = l == depth - 1
            x = _ffn(x.reshape(m, d), g2, wg2_, wu2, wo2, g_final if last else None, tm=tm, fc=256,
                     post="norm" if last else None).reshape(bsz, seq, d)
            new_states.append((_state_from_kernel_t(sg, dm.gh), _ssd_from_kernel(ss, dm.sh), cv,
                               _state_from_kernel_t(sr, dm.rh)))
        return x, new_states

    bp, tp = x_prompt.shape[:2]
    ts = x_sample.shape[1]
    dt_ = x_prompt.dtype

    def zero_states(_):
        return (jnp.zeros((bp, dm.gh, dm.gn, dm.gp), dt_), jnp.zeros((bp, dm.sh, dm.sn, dm.sp), dt_),
                jnp.zeros((bp, dm.conv_k - 1, dm.conv_dim), dt_), jnp.zeros((bp, dm.rh, dm.rn, dm.rp), dt_))

    def carried_states(l):
        return state_gla[l], state_ssd[l], cache_conv[l], state_ret[l]

    y_p, st_p = run_group(x_prompt, jnp.arange(tp, dtype=jnp.int32), zero_states)
    y_s, st_s = run_group(x_sample, PAST_LEN + jnp.arange(ts, dtype=jnp.int32), carried_states)

    def stack(sts, i):
        return jnp.stack([s[i] for s in sts])

    return (y_p, y_s, stack(st_p, 0), stack(st_p, 1), stack(st_p, 2), stack(st_p, 3),
            stack(st_s, 0), stack(st_s, 1), stack(st_s, 2), stack(st_s, 3))
```

```python
import functools
import math

import jax
import jax.numpy as jnp
from jax import lax
from jax.experimental import pallas as pl
from jax.experimental.pallas import tpu as pltpu

F32 = jnp.float32
BF16 = jnp.bfloat16

EPS = 1e-6
PAST_LEN = 1024
GLA_TAU = 16.0
ROPE_BASE = 10000.0
SCAN_CHUNK = 64

LANES = 128
SUBLANES = 8
MXU_COLS = 256
FFN_OUT_ROWS = 256
VMEM_LIMIT_BYTES = 56 * 1024 * 1024


def _dot(a, b):
    return jnp.dot(a, b, preferred_element_type=F32)


def _dot_nt(a, b):
    return lax.dot_general(a, b, (((1,), (1,)), ((), ())), preferred_element_type=F32)


def _dot_tn(a, b):
    return lax.dot_general(a, b, (((0,), (0,)), ((), ())), preferred_element_type=F32)


def _split3(x):
    hi = x.astype(BF16)
    r1 = x - hi.astype(F32)
    mid = r1.astype(BF16)
    lo = (r1 - mid.astype(F32)).astype(BF16)
    return hi, mid, lo


def _dot_exact01(m01, x):
    hi, mid, lo = _split3(x)
    return _dot(m01, hi) + _dot(m01, mid) + _dot(m01, lo)


def _chunk_cumsum(x, L):
    T = x.shape[0]
    rows = MXU_COLS if (T % MXU_COLS == 0 and MXU_COLS % L == 0) else T
    tril = _block_tril(rows, L)
    out = [_dot_exact01(tril, x[i:i + rows]) for i in range(0, T, rows)]
    return out[0] if len(out) == 1 else jnp.concatenate(out, axis=0)


def _rms(x, g):
    return x * lax.rsqrt(jnp.mean(x * x, axis=-1, keepdims=True) + EPS) * g


def _silu(x):
    return x * jax.nn.sigmoid(x)


def _softplus(x):
    return jnp.maximum(x, 0.0) + jnp.log(1.0 + jnp.exp(-jnp.abs(x)))


def _log_sigmoid(x):
    return jnp.minimum(x, 0.0) - jnp.log(1.0 + jnp.exp(-jnp.abs(x)))


def _iota(shape, axis):
    return lax.broadcasted_iota(jnp.int32, shape, axis)


def _log2(n):
    k = int(math.log2(n))
    assert (1 << k) == n, f"{n} must be a power of two"
    return k


def _ffn_kernel(*refs, fc, post):
    if post is None:
        x_ref, g_ref, wg_ref, wu_ref, wo_ref, o_ref, h_ref, a_ref = refs
    elif post == "norm":
        x_ref, g_ref, wg_ref, wu_ref, wo_ref, g2_ref, o_ref, h_ref, a_ref = refs
    else:
        x_ref, g_ref, wg_ref, wu_ref, wo_ref, g2_ref, o_ref, hn_ref, h_ref, a_ref = refs
    x = x_ref[...]
    h_ref[...] = _rms(x, g_ref[...]).astype(BF16)
    d_ff = wg_ref.shape[1]
    for c in range(d_ff // fc):
        sl = slice(c * fc, (c + 1) * fc)
        gate = _dot(h_ref[...], wg_ref[:, sl])
        up = _dot(h_ref[...], wu_ref[:, sl])
        a_ref[:, sl] = (_silu(gate) * up).astype(BF16)
    tm = x_ref.shape[0]
    rb = FFN_OUT_ROWS if tm % FFN_OUT_ROWS == 0 else tm
    for i in range(tm // rb):
        r = slice(i * rb, (i + 1) * rb)
        y = x_ref[r, :] + 0.5 * _dot(a_ref[r, :], wo_ref[...])
        if post == "norm":
            y = _rms(y, g2_ref[...])
        o_ref[r, :] = y
        if post == "emit":
            hn_ref[r, :] = _rms(y, g2_ref[...]).astype(BF16)


def _const_spec(shape):
    nd = len(shape)
    return pl.BlockSpec(shape, lambda *_: (0,) * nd, pipeline_mode=pl.Buffered(1))


def _ffn(x2d, g, w_gate, w_up, w_out, g2, *, tm, fc, post):
    m, d = x2d.shape
    d_ff = w_gate.shape[1]
    assert m % tm == 0 and d_ff % fc == 0 and (g2 is None) == (post is None)
    kern = functools.partial(_ffn_kernel, fc=fc, post=post)
    row_spec = pl.BlockSpec((tm, d), lambda i: (i, 0))
    in_specs = [row_spec, _const_spec((1, d)), _const_spec((d, d_ff)), _const_spec((d, d_ff)),
                _const_spec((d_ff, d))]
    args = [x2d, g, w_gate, w_up, w_out]
    if post is not None:
        in_specs.append(_const_spec((1, d)))
        args.append(g2)
    out_specs, out_shape = row_spec, jax.ShapeDtypeStruct((m, d), F32)
    if post == "emit":
        out_specs, out_shape = [row_spec, row_spec], [out_shape, jax.ShapeDtypeStruct((m, d), BF16)]
    return pl.pallas_call(
        kern,
        grid=(m // tm,),
        in_specs=in_specs,
        out_specs=out_specs,
        out_shape=out_shape,
        scratch_shapes=[pltpu.VMEM((tm, d), BF16), pltpu.VMEM((tm, d_ff), BF16)],
        compiler_params=pltpu.CompilerParams(
            dimension_semantics=("arbitrary",), vmem_limit_bytes=VMEM_LIMIT_BYTES),
        name="ffn",
    )(*args)


class _Dims:
    def __init__(self, state_gla, state_ssd, cache_conv, state_ret, gla_w_gate2, ssd_conv_w, d_model):
        _, _, self.gh, self.gn, self.gp = state_gla.shape
        _, _, self.sh, self.sn, self.sp = state_ssd.shape
        _, _, self.rh, self.rn, self.rp = state_ret.shape
        self.conv_k = ssd_conv_w.shape[1]
        self.conv_dim = cache_conv.shape[-1]
        self.rank = gla_w_gate2.shape[1]
        self.d = d_model
        self.s_inner = self.sh * self.sp
        self.sg = (self.conv_dim - self.s_inner) // (2 * self.sn)
        assert self.gh * self.gn == 2 * LANES and self.gp == LANES
        assert self.rh * self.rn == 2 * LANES and self.rp == LANES
        assert self.sp * 2 == LANES and self.sg * self.sn == LANES and self.sh <= LANES
        assert self.rank <= LANES and self.conv_k - 1 <= SUBLANES
        w = [self.gh * self.gn, self.gh * self.gn, 2 * LANES, self.gh * self.gp, self.gh * self.gp,
             self.s_inner, self.conv_dim,
             self.rh * self.rn, self.rh * self.rn, self.rh * self.rp, self.rh * self.rp,
             d_model, d_model, d_model]
        self.seg_names = ["gq", "gk", "lrdt", "gv", "gr", "mz", "mxbc", "rq", "rk", "rv", "rg", "ga", "gb", "gc"]
        self.seg = {}
        off = 0
        for n, wd in zip(self.seg_names, w):
            assert wd % MXU_COLS == 0
            self.seg[n] = (off, off + wd)
            off += wd
        self.in_cols = off


def _head_lane_mask(shape, h, width):
    lane = _iota(shape, 1)
    return (lane >> _log2(width)) == h


def _stack_heads(x, nh, width):
    parts = [jnp.where(_head_lane_mask(x.shape, h, width), x, 0.0).astype(BF16) for h in range(nh)]
    return jnp.concatenate(parts, axis=0)


def _linear_attention_chunk(q_sc, k_sc, sc_mul, q_in, k_up, v, dec_row, dec_col, st_ref, nh, n, p, issue):
    L = q_sc.shape[0]
    k_big = _stack_heads(k_sc, nh, n)
    sc = _dot_nt(q_sc.astype(BF16), k_big)
    st = st_ref[...]
    rowhead = _iota(st.shape, 0) >> _log2(n)
    st_bd = jnp.concatenate([jnp.where(rowhead == h, st, 0.0).astype(BF16) for h in range(nh)], axis=1)
    inter = _dot(q_in.astype(BF16), st_bd)
    rows = [k_up]
    if dec_col is None:
        rows.append(jnp.broadcast_to(dec_row, (SUBLANES, nh * n)))
    pad = LANES - L - (len(rows) - 1) * SUBLANES
    assert pad >= 0
    if pad:
        rows.append(jnp.zeros((pad, nh * n), F32))
    kt = jnp.concatenate(rows, axis=0).T if len(rows) > 1 else k_up.T
    issue()
    lane = _iota(sc.shape, 1)
    row = _iota(sc.shape, 0)
    causal = (lane & (L - 1)) <= row
    sc = jnp.where(causal, sc if sc_mul is None else sc * sc_mul, 0.0)
    sc = sc.astype(BF16)
    if L % LANES == 0:
        intra = jnp.concatenate(
            [_dot(sc[:, h * L:(h + 1) * L], v[:, h * p:(h + 1) * p].astype(BF16)) for h in range(nh)], axis=1)
    else:
        intra = _dot(sc, _stack_heads(v, nh, p))
    o = intra + inter
    if dec_col is None:
        dec_col = kt[:, L:L + 1]
    kt = kt.astype(BF16)
    upd = []
    for h in range(nh):
        v_h = v[:, h * p:(h + 1) * p].astype(BF16)
        if L < LANES:
            v_h = jnp.concatenate([v_h, jnp.zeros((LANES - L, p), BF16)], axis=0)
        upd.append(_dot(kt[h * n:(h + 1) * n, :], v_h))
    st_ref[...] = st * dec_col + jnp.concatenate(upd, axis=0)
    return o


def _group_norm_heads(o, nh, p):
    outs = []
    for h in range(nh):
        oh = o[:, h * p:(h + 1) * p]
        mu = jnp.mean(oh, axis=-1, keepdims=True)
        dlt = oh - mu
        var = jnp.mean(dlt * dlt, axis=-1, keepdims=True)
        outs.append(dlt * lax.rsqrt(var + EPS))
    return jnp.concatenate(outs, axis=1)


def _block_tril(tt, L):
    row = _iota((tt, tt), 0)
    col = _iota((tt, tt), 1)
    k = _log2(L)
    return jnp.where((col <= row) & ((row >> k) == (col >> k)), 1.0, 0.0).astype(BF16)


def _ret_log_gamma(head, nh):
    out = jnp.full(head.shape, math.log1p(-2.0 ** (-5.0 - (nh - 1))), F32)
    for h in range(nh - 1):
        out = jnp.where(head == h, math.log1p(-2.0 ** (-5.0 - h)), out)
    return out


def _ssd_decay_terms(b8, dtc, dm):
    L = b8.shape[0]
    bl8 = b8[L - 1:L, :]
    w8 = dtc * jnp.exp(bl8 - b8)
    b8t = _transpose_rows(b8)
    dtt = _transpose_rows(dtc)
    lane8 = _iota(b8.shape, 1)
    ebl8 = jnp.exp(bl8)
    bcols, eb_cols, w_cols, dec_cols = [], [], [], []
    for hh in range(dm.sh):
        bcol = jnp.sum(jnp.where(lane8 == hh, b8, 0.0), axis=-1, keepdims=True)
        bcols.append(bcol)
        eb_cols.append(jnp.broadcast_to(jnp.exp(bcol), (L, LANES)))
        w_cols.append(jnp.broadcast_to(
            jnp.sum(jnp.where(lane8 == hh, w8, 0.0), axis=-1, keepdims=True), (L, LANES)))
        dec_cols.append(jnp.broadcast_to(
            jnp.sum(jnp.where(_iota(bl8.shape, 1) == hh, ebl8, 0.0), axis=-1, keepdims=True), (1, LANES)))

    def pair_lanes(cols):
        first = _iota(cols[0].shape, 1) < dm.sp
        return jnp.concatenate(
            [jnp.where(first, cols[2 * j], cols[2 * j + 1]) for j in range(dm.sh // 2)], axis=1)

    return bcols, b8t, dtt, pair_lanes(eb_cols), pair_lanes(w_cols), pair_lanes(dec_cols)


def _retention_tables(dm, L):
    nlane = dm.rh * dm.rn
    lgam_q = _ret_log_gamma(_iota((L, nlane), 1) >> _log2(dm.rn), dm.rh)
    pos_q = _iota((L, nlane), 0).astype(F32)
    q_dec = jnp.exp((pos_q + 1.0) * lgam_q)
    k_dec = jnp.exp((L - 1.0 - pos_q) * lgam_q)
    lgam_r = _ret_log_gamma(_iota((nlane, dm.rp), 0) >> _log2(dm.rn), dm.rh)
    st_dec = jnp.exp(L * lgam_r)
    lgam_s = _ret_log_gamma(_iota((L, dm.rh * L), 1) >> _log2(L), dm.rh)
    dpos = (_iota((L, dm.rh * L), 0) - (_iota((L, dm.rh * L), 1) & (L - 1))).astype(F32)
    sc_dec = jnp.exp(jnp.maximum(dpos, 0.0) * lgam_s)
    return q_dec, k_dec, st_dec, sc_dec


def _transpose_rows(x):
    L = x.shape[0]
    if L % LANES:
        pad = LANES - L % LANES
        x = jnp.concatenate([x, jnp.zeros((pad, x.shape[1]), x.dtype)], axis=0)
    return x.T[:, :L]


def _swap_halves(x, n):
    width = x.shape[1]
    half = n // 2
    first = (_iota(x.shape, 1) & (n - 1)) < half
    return jnp.where(first, pltpu.roll(x, width - half, 1), pltpu.roll(x, half, 1))


def _mixer_kernel(x_ref, h_ref, cos_ref, sin_ref, sg0_ref, ss0_ref, cv0_ref, sr0_ref,
                  win_ref, wg2_ref, bg_ref, gnorm_ref, cw_ref, cb_ref, dtb_ref, alog_ref,
                  dexp_ref, snorm_ref, rnorm_ref, wbg_ref, wbs_ref, wbr_ref, wout_ref,
                  y_ref, sg_ref, ss_ref, cv_ref, sr_ref,
                  stg, sts, cbuf, strt, ogla, ossd, oret,
                  *, dm, tt, lg, ls, lr):
    t = pl.program_id(1)
    nt = pl.num_programs(1)
    ck = dm.conv_k - 1

    @pl.when(t == 0)
    def _():
        stg[...] = sg0_ref[...]
        strt[...] = sr0_ref[...]
        z0 = ss0_ref[...]
        hpg = dm.sh // dm.sg
        sts[...] = jnp.concatenate(
            [jnp.where(_head_lane_mask(z0.shape, g, dm.sp * hpg), z0, 0.0) for g in range(dm.sg)], axis=0)
        cbuf[SUBLANES - ck:SUBLANES, :] = cv0_ref[...]

    pending = []
    for name in ["lrdt", "mxbc", "gq", "gk", "gv", "gr", "mz", "rq", "rk", "rv", "rg", "ga", "gb", "gc"]:
        a, b = dm.seg[name]
        pending += [(name, s) for s in range(a, b, MXU_COLS)]
    parts = {name: [] for name in dm.seg_names}

    def issue(n=1):
        for _ in range(n):
            if pending:
                name, s = pending.pop(0)
                parts[name].append(_dot(h_ref[...], win_ref[:, s:s + MXU_COLS]))

    every = max(1, tt // MXU_COLS)
    calls = [0]

    def issue_thin(n=1):
        for _ in range(n):
            calls[0] += 1
            if calls[0] % every == 0:
                issue()

    def take(name):
        a, b = dm.seg[name]
        while len(parts[name]) * MXU_COLS < b - a:
            issue()
        return parts[name][0] if len(parts[name]) == 1 else jnp.concatenate(parts[name], axis=1)

    ret_tabs = _retention_tables(dm, lr)
    lrdt = take("lrdt")
    xbc_pre = take("mxbc")
    gate = _dot(lrdt[:, :LANES].astype(BF16), wg2_ref[...]) + bg_ref[...]
    cbuf[SUBLANES:SUBLANES + tt, :] = xbc_pre
    conv = cb_ref[...] + cw_ref[ck:ck + 1, :] * xbc_pre
    for j in range(ck):
        conv = conv + cw_ref[j:j + 1, :] * cbuf[SUBLANES - ck + j:SUBLANES - ck + j + tt, :]
    new_carry = cbuf[SUBLANES + tt - ck:SUBLANES + tt, :]
    cbuf[SUBLANES - ck:SUBLANES, :] = new_carry
    q = take("gq") * (dm.gn ** -0.5)
    k = take("gk")
    xbc = _silu(conv)
    xs = xbc[:, :dm.s_inner]
    bm = xbc[:, dm.s_inner:dm.s_inner + LANES]
    cm = xbc[:, dm.s_inner + LANES:dm.s_inner + 2 * LANES]
    dt8 = _softplus(lrdt[:, LANES:] + dtb_ref[...])
    a8 = dt8 * (-jnp.exp(alog_ref[...]))
    b8_all = _chunk_cumsum(a8, ls)
    issue(3)

    log_a = _log_sigmoid(gate) * (1.0 / GLA_TAU)
    b_all = _chunk_cumsum(log_a, lg)
    issue(3)
    qd_all = q * jnp.exp(b_all)
    kd_all = k * jnp.exp(-b_all)
    v = take("gv").astype(BF16)
    for c in range(tt // lg):
        r = slice(c * lg, (c + 1) * lg)
        b = b_all[r]
        bl = b[lg - 1:lg, :]
        ks = k[r] * jnp.exp(bl - b)
        ogla[r, :] = _linear_attention_chunk(qd_all[r], kd_all[r], None, qd_all[r], ks, v[r],
                                             jnp.exp(bl), None, stg, dm.gh, dm.gn, dm.gp,
                                             issue_thin if c else (lambda: None))
    y_gla = (_group_norm_heads(ogla[...], dm.gh, dm.gp) * gnorm_ref[...] * _silu(take("gr"))).astype(BF16)
    issue(2)

    hpg = dm.sh // dm.sg
    for c in range(tt // ls):
        r = slice(c * ls, (c + 1) * ls)
        bcols, b8t, dtt, eb_exp, w_exp, dec_row = _ssd_decay_terms(b8_all[r], dt8[r], dm)
        xs_c = xs[r]
        xs_bf = xs_c.astype(BF16)
        cm_c = cm[r]
        bm_c = bm[r].astype(BF16)
        causal = _iota((ls, ls), 1) <= _iota((ls, ls), 0)
        gmats = [_dot_nt(jnp.where(_head_lane_mask(cm_c.shape, g, dm.sn), cm_c, 0.0).astype(BF16), bm_c)
                 for g in range(dm.sg)]
        st = sts[...]
        y_inter = _dot(cm_c.astype(BF16), st.astype(BF16)) * eb_exp
        issue_thin(2)
        y_pairs = []
        for hh in range(dm.sh):
            seg = jnp.where(causal, bcols[hh] - b8t[hh:hh + 1, :], -jnp.inf)
            m_h = gmats[hh // hpg] * dtt[hh:hh + 1, :] * jnp.exp(seg)
            pr = hh // 2
            y_pairs.append(_dot(m_h.astype(BF16), xs_bf[:, pr * LANES:(pr + 1) * LANES]))
            if hh % 2:
                issue_thin()
        first = _iota((ls, LANES), 1) < dm.sp
        y_intra = jnp.concatenate(
            [jnp.where(first, y_pairs[2 * j], y_pairs[2 * j + 1]) for j in range(dm.sh // 2)], axis=1)
        ossd[r, :] = y_intra + y_inter
        upd = _dot_tn(bm_c, (xs_c * w_exp).astype(BF16))
        blk = (_iota(upd.shape, 0) >> _log2(dm.sn)) == (_iota(upd.shape, 1) >> _log2(dm.sp * hpg))
        sts[...] = st * dec_row + jnp.where(blk, upd, 0.0)
    y = ossd[...] + dexp_ref[...] * xs
    y = y * _silu(take("mz"))
    issue()
    y_ssd = _rms(y, snorm_ref[...]).astype(BF16)

    late = [(y_gla, wbg_ref, s) for s in range(0, dm.d, MXU_COLS)]
    late += [(y_ssd, wbs_ref, s) for s in range(0, dm.d, MXU_COLS)]
    yb_parts = []

    def issue_late(n=1):
        for _ in range(n):
            if late:
                yv, w_ref, s = late.pop(0)
                yb_parts.append(_dot(yv, w_ref[:, s:s + MXU_COLS]))

    cos = cos_ref[...]
    sin = sin_ref[...]
    rq_pre = take("rq")
    rk_pre = take("rk")
    issue(len(pending))
    rq = rq_pre * cos + _swap_halves(rq_pre, dm.rn) * sin
    issue_late()
    rk = (rk_pre * cos + _swap_halves(rk_pre, dm.rn) * sin) * (dm.rn ** -0.5)
    issue_late()
    rv = take("rv").astype(BF16)
    q_dec, k_dec, st_dec, sc_dec = ret_tabs
    for c in range(tt // lr):
        r = slice(c * lr, (c + 1) * lr)
        oret[r, :] = _linear_attention_chunk(rq[r], rk[r], sc_dec, rq[r] * q_dec, rk[r] * k_dec, rv[r],
                                             None, st_dec, strt, dm.rh, dm.rn, dm.rp, issue_late)
        issue_late()
    issue_late(len(late))
    ncol = dm.d // MXU_COLS
    yb_gla = jnp.concatenate(yb_parts[:ncol], axis=1)
    yb_ssd = jnp.concatenate(yb_parts[ncol:], axis=1)
    ga, gb, gc = take("ga"), take("gb"), take("gc")
    m2 = jax.nn.sigmoid(ga) * yb_gla + jax.nn.sigmoid(gb) * yb_ssd
    y_ret = (_group_norm_heads(oret[...], dm.rh, dm.rp) * rnorm_ref[...] * _silu(take("rg"))).astype(BF16)

    halves = 4 if tt % (4 * SUBLANES * 2) == 0 else 1
    hr = tt // halves
    rs = [slice(i * hr, (i + 1) * hr) for i in range(halves)]
    yb_ret = [_dot(y_ret[r], wbr_ref[...]) for r in rs]
    for r, yb in zip(rs, yb_ret):
        m = m2[r] + jax.nn.sigmoid(gc[r]) * yb
        y_ref[r, :] = x_ref[r, :] + _dot(m.astype(BF16), wout_ref[...])

    @pl.when(t == nt - 1)
    def _():
        sg_ref[...] = stg[...]
        sr_ref[...] = strt[...]
        sz = sts[...]
        acc = sz[0:dm.sn]
        for g in range(1, dm.sg):
            acc = acc + sz[g * dm.sn:(g + 1) * dm.sn]
        ss_ref[...] = acc
        cv_ref[...] = cbuf[SUBLANES - ck:SUBLANES, :]


def _mixer(x, h, cos, sin, sg0, ss0, cv0, sr0, wts, dm, *, tt, lg, ls, lr):
    bsz, seq, d = x.shape
    assert seq % tt == 0 and tt % lg == 0 and tt % ls == 0 and tt % lr == 0
    kern = functools.partial(_mixer_kernel, dm=dm, tt=tt, lg=lg, ls=ls, lr=lr)
    gq = dm.gh * dm.gn
    rq = dm.rh * dm.rn

    def per_b(shape):
        nd = len(shape)
        return pl.BlockSpec((None,) + shape, lambda b, t: (b,) + (0,) * nd)

    in_specs = [
        pl.BlockSpec((None, tt, d), lambda b, t: (b, t, 0)),
        pl.BlockSpec((None, tt, d), lambda b, t: (b, t, 0)),
        pl.BlockSpec((tt, rq), lambda b, t: (t, 0)),
        pl.BlockSpec((tt, rq), lambda b, t: (t, 0)),
        per_b((gq, dm.gp)), per_b((dm.sn, dm.s_inner)), per_b((dm.conv_k - 1, dm.conv_dim)), per_b((rq, dm.rp)),
    ] + [_const_spec(w.shape) for w in wts]
    out_specs = [
        pl.BlockSpec((None, tt, d), lambda b, t: (b, t, 0)),
        per_b((gq, dm.gp)), per_b((dm.sn, dm.s_inner)), per_b((dm.conv_k - 1, dm.conv_dim)), per_b((rq, dm.rp)),
    ]
    out_shape = [
        jax.ShapeDtypeStruct((bsz, seq, d), F32),
        jax.ShapeDtypeStruct((bsz, gq, dm.gp), F32),
        jax.ShapeDtypeStruct((bsz, dm.sn, dm.s_inner), F32),
        jax.ShapeDtypeStruct((bsz, dm.conv_k - 1, dm.conv_dim), F32),
        jax.ShapeDtypeStruct((bsz, rq, dm.rp), F32),
    ]
    scratch = [
        pltpu.VMEM((gq, dm.gp), F32),
        pltpu.VMEM((dm.sg * dm.sn, dm.s_inner), F32),
        pltpu.VMEM((tt + SUBLANES, dm.conv_dim), F32),
        pltpu.VMEM((rq, dm.rp), F32),
        pltpu.VMEM((tt, dm.gh * dm.gp), F32),
        pltpu.VMEM((tt, dm.s_inner), F32),
        pltpu.VMEM((tt, dm.rh * dm.rp), F32),
    ]
    return pl.pallas_call(
        kern,
        grid=(bsz, seq // tt),
        in_specs=in_specs,
        out_specs=out_specs,
        out_shape=out_shape,
        scratch_shapes=scratch,
        compiler_params=pltpu.CompilerParams(
            dimension_semantics=("arbitrary", "arbitrary"), vmem_limit_bytes=VMEM_LIMIT_BYTES),
        name="mixer",
    )(x, h, cos, sin, sg0, ss0, cv0, sr0, *wts)


def _pad_cols(w, n):
    return jnp.pad(w, ((0, 0), (0, n - w.shape[1])))


def _row(v, n=None):
    v = v.reshape(1, -1).astype(F32)
    return v if n is None else _pad_cols(v, n)


def _rope_tables(pos, dm):
    half = dm.rn // 2
    freqs = ROPE_BASE ** (-jnp.arange(half, dtype=F32) / half)
    ang = pos.astype(F32)[:, None] * freqs[None, :]
    cos = jnp.cos(ang)
    sin = jnp.sin(ang)
    cos_h = jnp.concatenate([cos, cos], axis=1)
    sin_h = jnp.concatenate([-sin, sin], axis=1)
    return jnp.tile(cos_h, (1, dm.rh)), jnp.tile(sin_h, (1, dm.rh))


def _state_to_kernel_t(s):
    b, h, n, p = s.shape
    return s.reshape(b, h * n, p)


def _state_from_kernel_t(s, h):
    b, hn, p = s.shape
    return s.reshape(b, h, hn // h, p)


def _ssd_to_kernel(s):
    b, h, n, p = s.shape
    return jnp.transpose(s, (0, 2, 1, 3)).reshape(b, n, h * p)


def _ssd_from_kernel(s, h):
    b, n, hp = s.shape
    return jnp.transpose(s.reshape(b, n, h, hp // h), (0, 2, 1, 3))


def _tiles(seq):
    if seq % SCAN_CHUNK == 0:
        tt = 512 if seq % 512 == 0 else SCAN_CHUNK
        return dict(tt=tt, lg=SCAN_CHUNK, ls=min(tt, 128), lr=min(tt, 128))
    return dict(tt=seq, lg=seq, ls=seq, lr=seq)


def kernel(x_prompt, x_sample, state_gla, state_ssd, cache_conv, state_ret, norm_ffn1, ffn1_w_in, ffn1_w_out,
           norm_mix, w_in, gla_w_gate2, gla_b_gate, gla_norm, ssd_conv_w, ssd_conv_b, ssd_dt_bias, ssd_a_log,
           ssd_d, ssd_norm, ret_norm, w_branch_gla, w_branch_ssd, w_branch_ret, w_out, norm_ffn2, ffn2_w_in,
           ffn2_w_out, norm_final):
    depth = w_in.shape[0]
    d = x_prompt.shape[-1]
    d_ff = ffn1_w_out.shape[1]
    dm = _Dims(state_gla, state_ssd, cache_conv, state_ret, gla_w_gate2, ssd_conv_w, d)

    split_sizes = (dm.gh * dm.gn, dm.gh * dm.gn, dm.gh * dm.gp, dm.gh * dm.gp, dm.rank,
                   dm.s_inner, dm.conv_dim, dm.sh,
                   dm.rh * dm.rn, dm.rh * dm.rn, dm.rh * dm.rp, dm.rh * dm.rp, d, d, d)
    pts, acc = [], 0
    for s in split_sizes[:-1]:
        acc += s
        pts.append(acc)

    def mixer_weights(l):
        (s_gq, s_gk, s_gv, s_gr, s_glr, s_mz, s_mxbc, s_mdt,
         s_rq, s_rk, s_rv, s_rg, s_ga, s_gb, s_gc) = jnp.split(w_in[l], pts, axis=1)
        win = jnp.concatenate(
            [s_gq, s_gk, _pad_cols(s_glr, LANES), _pad_cols(s_mdt, LANES), s_gv, s_gr, s_mz, s_mxbc,
             s_rq, s_rk, s_rv, s_rg, s_ga, s_gb, s_gc], axis=1).astype(BF16)
        assert win.shape[1] == dm.in_cols
        wg2 = jnp.pad(gla_w_gate2[l], ((0, LANES - dm.rank), (0, 0))).astype(BF16)
        return [
            win, wg2, _row(gla_b_gate[l]), _row(gla_norm[l]),
            ssd_conv_w[l].astype(F32), _row(ssd_conv_b[l]), _row(ssd_dt_bias[l], LANES), _row(ssd_a_log[l], LANES),
            _row(jnp.repeat(ssd_d[l], dm.sp)), _row(ssd_norm[l]), _row(ret_norm[l]),
            w_branch_gla[l].astype(BF16), w_branch_ssd[l].astype(BF16), w_branch_ret[l].astype(BF16),
            w_out[l].astype(BF16),
        ]

    def ffn_weights(nrm, w_i, w_o, l):
        return _row(nrm[l]), w_i[l][:, :d_ff].astype(BF16), w_i[l][:, d_ff:].astype(BF16), w_o[l].astype(BF16)

    g_final = _row(norm_final)

    def run_group(x, pos, states):
        bsz, seq, _ = x.shape
        m = bsz * seq
        tm = 1024 if m % 1024 == 0 else m
        cos, sin = _rope_tables(pos, dm)
        tl = _tiles(seq)
        new_states = []
        for l in range(depth):
            sg0, ss0, cv0, sr0 = states(l)
            g1, wg1, wu1, wo1 = ffn_weights(norm_ffn1, ffn1_w_in, ffn1_w_out, l)
            x, h = _ffn(x.reshape(m, d), g1, wg1, wu1, wo1, _row(norm_mix[l]), tm=tm, fc=256, post="emit")
            x, sg, ss, cv, sr = _mixer(x.reshape(bsz, seq, d), h.reshape(bsz, seq, d), cos, sin,
                                       _state_to_kernel_t(sg0), _ssd_to_kernel(ss0), cv0,
                                       _state_to_kernel_t(sr0), mixer_weights(l), dm, **tl)
            g2, wg2_, wu2, wo2 = ffn_weights(norm_ffn2, ffn2_w_in, ffn2_w_out, l)
            last = l == depth - 1
            x = _ffn(x.reshape(m, d), g2, wg2_, wu2, wo2, g_final if last else None, tm=tm, fc=256,
                     post="norm" if last else None).reshape(bsz, seq, d)
            new_states.append((_state_from_kernel_t(sg, dm.gh), _ssd_from_kernel(ss, dm.sh), cv,
                               _state_from_kernel_t(sr, dm.rh)))
        return x, new_states

    bp, tp = x_prompt.shape[:2]
    ts = x_sample.shape[1]
    dt_ = x_prompt.dtype

    def zero_states(_):
        return (jnp.zeros((bp, dm.gh, dm.gn, dm.gp), dt_), jnp.zeros((bp, dm.sh, dm.sn, dm.sp), dt_),
                jnp.zeros((bp, dm.conv_k - 1, dm.conv_dim), dt_), jnp.zeros((bp, dm.rh, dm.rn, dm.rp), dt_))

    def carried_states(l):
        return state_gla[l], state_ssd[l], cache_conv[l], state_ret[l]

    y_p, st_p = run_group(x_prompt, jnp.arange(tp, dtype=jnp.int32), zero_states)
    y_s, st_s = run_group(x_sample, PAST_LEN + jnp.arange(ts, dtype=jnp.int32), carried_states)

    def stack(sts, i):
        return jnp.stack([s[i] for s in sts])

    return (y_p, y_s, stack(st_p, 0), stack(st_p, 1), stack(st_p, 2), stack(st_p, 3),
            stack(st_s, 0), stack(st_s, 1), stack(st_s, 2), stack(st_s, 3))
```
